```python
import math
import jax, jax.numpy as jnp
from jax import lax
import numpy as np

D_MODEL = 1024
BATCH = 8
SEQ = 2048
DEPTH = 4
DEC_BATCH = 128
DEC_SEQ = 1
PAST_LEN = 16384
PAGE_SIZE = 128

MIX_WIDTH = D_MODEL
HG_WIDTH = 3 * D_MODEL // 8
HG_DK = 64
HG_DV = 64
HG_HEADS = HG_WIDTH // HG_DV
RET_WIDTH = 3 * D_MODEL // 8
RET_DV = 64
RET_HEADS = RET_WIDTH // RET_DV
RET_DK = RET_DV // 2
RET_QK = RET_HEADS * RET_DK
S5_WIDTH = MIX_WIDTH - HG_WIDTH - RET_WIDTH
S5_GROUP = 16
S5_GROUPS = S5_WIDTH // S5_GROUP
S5_STATE = 64
D_FF = 4 * D_MODEL
IN_WIDTH = 4 * HG_WIDTH + 2 * RET_QK + 2 * RET_WIDTH + S5_WIDTH
CHUNK = 16
ROPE_BASE = 10000.0
EPS = 1e-5
ALPHA = (2.0 * DEPTH) ** 0.25
BETA = (8.0 * DEPTH) ** -0.25

kernel_name = "hymba_hgrn2_retnet_s5_deepnorm_step"


def _layer_norm(x, w, b):
    xf = x.astype(jnp.float32)
    mu = jnp.mean(xf, -1, keepdims=True)
    var = jnp.mean(jnp.square(xf - mu), -1, keepdims=True)
    return ((xf - mu) * lax.rsqrt(var + EPS) * w + b).astype(x.dtype)


def _rotary(x, pos):
    half = x.shape[-1] // 2
    inv = ROPE_BASE ** (-jnp.arange(half, dtype=jnp.float32) / half)
    ang = pos[:, None] * inv[None, :]
    cos = jnp.cos(ang)[:, None, :]
    sin = jnp.sin(ang)[:, None, :]
    x1, x2 = x[..., :half], x[..., half:]
    return jnp.concatenate([x1 * cos - x2 * sin, x1 * sin + x2 * cos], axis=-1)


def _chunked_gated_linear(q, k, v, log_f, s0):
    bsz, t, h, _ = q.shape
    dv = v.shape[-1]
    q, k, v, log_f = [a.astype(jnp.float32) for a in (q, k, v, log_f)]
    pad = (-t) % CHUNK
    if pad:
        cfg = ((0, 0), (0, pad), (0, 0), (0, 0))
        q, k, v, log_f = [jnp.pad(a, cfg) for a in (q, k, v, log_f)]
    n = (t + pad) // CHUNK

    def chunks(a):
        return a.reshape(bsz, n, CHUNK, h, a.shape[-1])

    q, k, v, log_f = chunks(q), chunks(k), chunks(v), chunks(log_f)
    b = jnp.cumsum(log_f, axis=2)
    causal = jnp.tril(jnp.ones((CHUNK, CHUNK), dtype=bool))[None, None, :, :, None, None]
    diff = b[:, :, :, None] - b[:, :, None, :]
    rel = jnp.where(causal, jnp.exp(jnp.where(causal, diff, 0.0)), 0.0)
    scores = jnp.einsum('bntshk,bnthk,bnshk->bnhts', rel, q, k)
    o_intra = jnp.einsum('bnhts,bnshv->bnthv', scores, v)
    q_dec = q * jnp.exp(b)
    k_dec = k * jnp.exp(b[:, :, -1:] - b)
    chunk_decay = jnp.exp(b[:, :, -1])

    def step(state, inp):
        qd, kd, vc, cd = inp
        o = jnp.einsum('bthk,bhkv->bthv', qd, state)
        state = cd[..., None] * state + jnp.einsum('bshk,bshv->bhkv', kd, vc)
        return state, o

    xs = tuple(jnp.moveaxis(a, 1, 0) for a in (q_dec, k_dec, v, chunk_decay))
    s_final, o_inter = lax.scan(step, s0.astype(jnp.float32), xs)
    o = o_intra + jnp.moveaxis(o_inter, 0, 1)
    o = o.reshape(bsz, n * CHUNK, h, dv)[:, :t]
    return o, s_final.astype(s0.dtype)


def _s5_combine(e1, e2):
    a1r, a1i, b1r, b1i = e1
    a2r, a2i, b2r, b2i = e2
    return (a2r * a1r - a2i * a1i,
            a2r * a1i + a2i * a1r,
            a2r * b1r - a2i * b1i + b2r,
            a2r * b1i + a2i * b1r + b2i)


def _s5_scan(u, log_dt, a_re, a_im, b_re, b_im, c_re, c_im, d, x0_re, x0_im):
    bsz, t, _ = u.shape
    f32 = jnp.float32
    uf = u.astype(f32).reshape(bsz, t, S5_GROUPS, S5_GROUP)
    a_re, a_im = a_re.astype(f32), a_im.astype(f32)
    dt = jnp.exp(log_dt.astype(f32))[:, None]
    mag = jnp.exp(dt * a_re)
    ab_re = mag * jnp.cos(dt * a_im)
    ab_im = mag * jnp.sin(dt * a_im)
    den = a_re * a_re + a_im * a_im
    nr = ab_re - 1.0
    g_re = (nr * a_re + ab_im * a_im) / den
    g_im = (ab_im * a_re - nr * a_im) / den
    b_re, b_im = b_re.astype(f32), b_im.astype(f32)
    bb_re = g_re[..., None] * b_re - g_im[..., None] * b_im
    bb_im = g_re[..., None] * b_im + g_im[..., None] * b_re
    bu_re = jnp.einsum('btgc,gnc->btgn', uf, bb_re)
    bu_im = jnp.einsum('btgc,gnc->btgn', uf, bb_im)
    x0r, x0i = x0_re.astype(f32), x0_im.astype(f32)
    bu_re = bu_re.at[:, 0].add(ab_re * x0r - ab_im * x0i)
    bu_im = bu_im.at[:, 0].add(ab_re * x0i + ab_im * x0r)
    ar = jnp.broadcast_to(ab_re, bu_re.shape)
    ai = jnp.broadcast_to(ab_im, bu_re.shape)
    _, _, xr, xi = lax.associative_scan(_s5_combine, (ar, ai, bu_re, bu_im), axis=1)
    y = (jnp.einsum('gcn,btgn->btgc', c_re.astype(f32), xr)
         - jnp.einsum('gcn,btgn->btgc', c_im.astype(f32), xi))
    y = y.reshape(bsz, t, S5_WIDTH) + d * uf.reshape(bsz, t, S5_WIDTH)
    return y, xr[:, -1].astype(x0_re.dtype), xi[:, -1].astype(x0_im.dtype)


def _layer(x, pos, s_hg, s_rt, s_re, s_im, w_in, lb, hg_norm_w, ret_norm_w,
           s5_log_dt, s5_a_re, s5_a_im, s5_b_re, s5_b_im, s5_c_re, s5_c_im, s5_d,
           s5_glu_w, s5_glu_b, w_out, ln1_w, ln1_b, w_up, w_down, ln2_w, ln2_b):
    f32 = jnp.float32
    bsz, t, _ = x.shape
    proj = jnp.einsum('btd,de->bte', x, w_in)
    sizes = [HG_WIDTH] * 4 + [RET_QK] * 2 + [RET_WIDTH] * 2 + [S5_WIDTH]
    cuts = [int(c) for c in np.cumsum(sizes)[:-1]]
    hq, hf, hi, hg, rq, rk, rv, rg, su = jnp.split(proj, cuts, axis=-1)

    def heads(a, nh):
        return a.reshape(bsz, t, nh, -1)

    lbf = lb.astype(f32)
    zf = hf.astype(f32)
    f_gate = lbf + (1.0 - lbf) * jax.nn.sigmoid(zf)
    log_f = jnp.log(f_gate)
    k_hg = (1.0 - lbf) * jax.nn.sigmoid(-zf)
    o_hg, s_hg_new = _chunked_gated_linear(heads(jax.nn.silu(hq), HG_HEADS), heads(k_hg, HG_HEADS),
                                           heads(hi, HG_HEADS), heads(log_f, HG_HEADS), s_hg)
    o_hg = o_hg * lax.rsqrt(jnp.mean(jnp.square(o_hg), -1, keepdims=True) + EPS)
    o_hg = o_hg * hg_norm_w * jax.nn.silu(heads(hg, HG_HEADS).astype(f32))
    o_hg = o_hg.reshape(bsz, t, HG_WIDTH)

    q_r = _rotary(heads(rq, RET_HEADS).astype(f32), pos)
    k_r = _rotary(heads(rk, RET_HEADS).astype(f32), pos) * (RET_DK ** -0.5)
    log_gamma = jnp.log1p(-jnp.exp2(-5.0 - jnp.arange(RET_HEADS, dtype=f32)))
    log_f_r = jnp.broadcast_to(log_gamma[:, None], (bsz, t, RET_HEADS, RET_DK))
    o_rt, s_rt_new = _chunked_gated_linear(q_r, k_r, heads(rv, RET_HEADS), log_f_r, s_rt)
    mu = jnp.mean(o_rt, -1, keepdims=True)
    var = jnp.mean(jnp.square(o_rt - mu), -1, keepdims=True)
    o_rt = (o_rt - mu) * lax.rsqrt(var + EPS) * ret_norm_w.reshape(RET_HEADS, RET_DV)
    o_rt = o_rt * jax.nn.silu(heads(rg, RET_HEADS).astype(f32))
    o_rt = o_rt.reshape(bsz, t, RET_WIDTH)

    y5, s_re_new, s_im_new = _s5_scan(su, s5_log_dt, s5_a_re, s5_a_im, s5_b_re, s5_b_im,
                                      s5_c_re, s5_c_im, s5_d, s_re, s_im)
    y5 = jax.nn.gelu(y5)
    y5 = y5 * jax.nn.sigmoid(y5 @ s5_glu_w + s5_glu_b)

    mixed = jnp.concatenate([o_hg, o_rt, y5], axis=-1).astype(x.dtype) @ w_out
    x = _layer_norm(ALPHA * x + mixed, ln1_w, ln1_b)
    ff = jnp.square(jax.nn.relu(x @ w_up)) @ w_down
    x = _layer_norm(ALPHA * x + ff, ln2_w, ln2_b)
    return x, s_hg_new, s_rt_new, s_re_new, s_im_new


def setup_inputs(seed: int = 0) -> dict:
    key = jax.random.key(seed)
    ks = jax.random.split(key, 32)
    f32 = jnp.float32

    def nrm(k, shape, scale):
        return scale * jax.random.normal(k, shape, f32)

    n_idx = jnp.arange(S5_STATE, dtype=f32)
    return {
        "x_prompt": nrm(ks[0], (BATCH, SEQ, D_MODEL), 1.0),
        "x_sample": nrm(ks[1], (DEC_BATCH, DEC_SEQ, D_MODEL), 1.0),
        "state_hgrn": nrm(ks[2], (DEPTH, DEC_BATCH, HG_HEADS, HG_DK, HG_DV), 0.3),
        "state_ret": nrm(ks[3], (DEPTH, DEC_BATCH, RET_HEADS, RET_DK, RET_DV), 1.0),
        "state_s5_re": nrm(ks[4], (DEPTH, DEC_BATCH, S5_GROUPS, S5_STATE), 0.3),
        "state_s5_im": nrm(ks[5], (DEPTH, DEC_BATCH, S5_GROUPS, S5_STATE), 0.3),
        "w_in": nrm(ks[6], (DEPTH, D_MODEL, IN_WIDTH), D_MODEL ** -0.5),
        "hgrn_lb_logits": nrm(ks[7], (DEPTH, HG_WIDTH), 0.1),
        "hgrn_norm_w": 1.0 + nrm(ks[8], (DEPTH, HG_DV), 0.02),
        "ret_norm_w": 1.0 + nrm(ks[9], (DEPTH, RET_WIDTH), 0.02),
        "s5_log_dt": jax.random.uniform(ks[10], (DEPTH, S5_GROUPS), f32, math.log(1e-3), math.log(1e-1)),
        "s5_a_re": -0.5 + nrm(ks[11], (DEPTH, S5_GROUPS, S5_STATE), 0.01),
        "s5_a_im": math.pi * n_idx + nrm(ks[12], (DEPTH, S5_GROUPS, S5_STATE), 0.01),
        "s5_b_re": nrm(ks[13], (DEPTH, S5_GROUPS, S5_STATE, S5_GROUP), (2 * S5_GROUP) ** -0.5),
        "s5_b_im": nrm(ks[14], (DEPTH, S5_GROUPS, S5_STATE, S5_GROUP), (2 * S5_GROUP) ** -0.5),
        "s5_c_re": nrm(ks[15], (DEPTH, S5_GROUPS, S5_GROUP, S5_STATE), (2 * S5_STATE) ** -0.5),
        "s5_c_im": nrm(ks[16], (DEPTH, S5_GROUPS, S5_GROUP, S5_STATE), (2 * S5_STATE) ** -0.5),
        "s5_d": nrm(ks[17], (DEPTH, S5_WIDTH), 0.5),
        "s5_glu_w": nrm(ks[18], (DEPTH, S5_WIDTH, S5_WIDTH), S5_WIDTH ** -0.5),
        "s5_glu_b": nrm(ks[19], (DEPTH, S5_WIDTH), 0.01),
        "w_out": nrm(ks[20], (DEPTH, MIX_WIDTH, D_MODEL), BETA * MIX_WIDTH ** -0.5),
        "ln1_w": 1.0 + nrm(ks[21], (DEPTH, D_MODEL), 0.02),
        "ln1_b": nrm(ks[22], (DEPTH, D_MODEL), 0.02),
        "w_up": nrm(ks[23], (DEPTH, D_MODEL, D_FF), D_MODEL ** -0.5),
        "w_down": nrm(ks[24], (DEPTH, D_FF, D_MODEL), BETA * D_FF ** -0.5),
        "ln2_w": 1.0 + nrm(ks[25], (DEPTH, D_MODEL), 0.02),
        "ln2_b": nrm(ks[26], (DEPTH, D_MODEL), 0.02),
    }


def reference(x_prompt, x_sample, state_hgrn, state_ret, state_s5_re, state_s5_im,
              w_in, hgrn_lb_logits, hgrn_norm_w, ret_norm_w, s5_log_dt, s5_a_re, s5_a_im,
              s5_b_re, s5_b_im, s5_c_re, s5_c_im, s5_d, s5_glu_w, s5_glu_b, w_out,
              ln1_w, ln1_b, w_up, w_down, ln2_w, ln2_b):
    f32 = jnp.float32
    lb_prob = jax.nn.softmax(hgrn_lb_logits.astype(f32), axis=0)
    lower_bounds = jnp.cumsum(lb_prob, axis=0) - lb_prob[0:1]

    bp, tp, _ = x_prompt.shape
    pos_p = jnp.arange(tp, dtype=f32)
    pos_s = PAST_LEN + jnp.arange(x_sample.shape[1], dtype=f32)
    zero_hg = jnp.zeros((bp, HG_HEADS, HG_DK, HG_DV), f32)
    zero_rt = jnp.zeros((bp, RET_HEADS, RET_DK, RET_DV), f32)
    zero_s5 = jnp.zeros((bp, S5_GROUPS, S5_STATE), f32)

    yp, ys = x_prompt, x_sample
    hg_p, rt_p, re_p, im_p = [], [], [], []
    hg_s, rt_s, re_s, im_s = [], [], [], []
    for l in range(DEPTH):
        params = (w_in[l], lower_bounds[l], hgrn_norm_w[l], ret_norm_w[l], s5_log_dt[l],
                  s5_a_re[l], s5_a_im[l], s5_b_re[l], s5_b_im[l], s5_c_re[l], s5_c_im[l],
                  s5_d[l], s5_glu_w[l], s5_glu_b[l], w_out[l], ln1_w[l], ln1_b[l],
                  w_up[l], w_down[l], ln2_w[l], ln2_b[l])
        yp, a, b, c, d = _layer(yp, pos_p, zero_hg, zero_rt, zero_s5, zero_s5, *params)
        ys, e, f, g, h = _layer(ys, pos_s, state_hgrn[l], state_ret[l], state_s5_re[l],
                                state_s5_im[l], *params)
        hg_p.append(a); rt_p.append(b); re_p.append(c); im_p.append(d)
        hg_s.append(e); rt_s.append(f); re_s.append(g); im_s.append(h)

    new_hgrn_prompt = jnp.stack(hg_p)
    new_ret_prompt = jnp.stack(rt_p)
    new_s5_re_prompt = jnp.stack(re_p)
    new_s5_im_prompt = jnp.stack(im_p)
    new_hgrn_sample = jnp.stack(hg_s)
    new_ret_sample = jnp.stack(rt_s)
    new_s5_re_sample = jnp.stack(re_s)
    new_s5_im_sample = jnp.stack(im_s)
    return (yp, ys, new_hgrn_prompt, new_ret_prompt, new_s5_re_prompt, new_s5_im_prompt,
            new_hgrn_sample, new_ret_sample, new_s5_re_sample, new_s5_im_sample)
```

```python
import functools
import math

import jax
import jax.numpy as jnp
import numpy as np
from jax import lax
from jax.experimental import pallas as pl
from jax.experimental.pallas import tpu as pltpu

F32 = jnp.float32
BF16 = jnp.bfloat16
HIGHEST = lax.Precision.HIGHEST

D_MODEL = 1024
DEPTH = 4
PAST_LEN = 16384
HG_WIDTH = 384
HG_HEADS = 6
HG_DK = 64
HG_DV = 64
RET_WIDTH = 384
RET_HEADS = 6
RET_DK = 32
RET_DV = 64
RET_QK = RET_HEADS * RET_DK
RET_QK_PAD = 256
S5_WIDTH = 256
S5_GROUP = 16
S5_GROUPS = 16
S5_STATE = 64
S5_N = S5_GROUPS * S5_STATE
D_FF = 4 * D_MODEL
ROPE_BASE = 10000.0
EPS = 1e-5
ALPHA = (2.0 * DEPTH) ** 0.25

HG_COLS = 4 * HG_WIDTH
RT_COLS = 4 * RET_QK_PAD + 2 * RET_WIDTH
PROJ_COLS = HG_COLS + RT_COLS + S5_WIDTH

HG_BLOCK = 16
TIME_BLOCK = 128
FF_CHUNK = 1024
VMEM_LIMIT = 56 * 1024 * 1024


def _dot(a, b):
    return jnp.dot(a, b, preferred_element_type=F32)


def _dot_nt(a, b):
    return lax.dot_general(a, b, (((1,), (1,)), ((), ())), preferred_element_type=F32)


def _dot_tn(a, b):
    return lax.dot_general(a, b, (((0,), (0,)), ((), ())), preferred_element_type=F32)


def _dot_hi(a, b):
    return jnp.dot(a, b, preferred_element_type=F32, precision=HIGHEST)


def _sigmoid(x):
    return jax.nn.sigmoid(x)


def _silu(x):
    return x * jax.nn.sigmoid(x)


def _layer_norm(y, w, b):
    mu = jnp.mean(y, -1, keepdims=True)
    yc = y - mu
    var = jnp.mean(yc * yc, -1, keepdims=True)
    return yc * lax.rsqrt(var + EPS) * w + b


def _proj_kernel(x_ref, w_ref, hg_ref, rt_ref, s5_ref):
    x = x_ref[...].astype(BF16)
    hg_ref[...] = _dot(x, w_ref[:, 0:HG_COLS])
    rt_ref[...] = _dot(x, w_ref[:, HG_COLS:HG_COLS + RT_COLS])
    s5_ref[...] = _dot(x, w_ref[:, HG_COLS + RT_COLS:PROJ_COLS])


def _proj(x, w, tm):
    m = x.shape[0]
    return pl.pallas_call(
        _proj_kernel,
        grid=(m // tm,),
        in_specs=[pl.BlockSpec((tm, D_MODEL), lambda i: (i, 0)),
                  pl.BlockSpec((D_MODEL, PROJ_COLS), lambda i: (0, 0),
                               pipeline_mode=pl.Buffered(1))],
        out_specs=[pl.BlockSpec((tm, HG_COLS), lambda i: (i, 0)),
                   pl.BlockSpec((tm, RT_COLS), lambda i: (i, 0)),
                   pl.BlockSpec((tm, S5_WIDTH), lambda i: (i, 0))],
        out_shape=[jax.ShapeDtypeStruct((m, HG_COLS), F32),
                   jax.ShapeDtypeStruct((m, RT_COLS), F32),
                   jax.ShapeDtypeStruct((m, S5_WIDTH), F32)],
        compiler_params=pltpu.CompilerParams(
            dimension_semantics=("parallel",), vmem_limit_bytes=VMEM_LIMIT),
        name="proj",
    )(x, w)


def _ffn_kernel(x_ref, a_ref, b_ref, c_ref, wo_ref, l1w_ref, l1b_ref, wu_ref, wd_ref,
                l2w_ref, l2b_ref, o_ref):
    x = x_ref[...]
    mixed = (_dot(a_ref[...].astype(BF16), wo_ref[0:HG_WIDTH, :])
             + _dot(b_ref[...].astype(BF16), wo_ref[HG_WIDTH:HG_WIDTH + RET_WIDTH, :])
             + _dot(c_ref[...].astype(BF16), wo_ref[HG_WIDTH + RET_WIDTH:D_MODEL, :]))
    x1 = _layer_norm(ALPHA * x + mixed, l1w_ref[...], l1b_ref[...])
    xb = x1.astype(BF16)
    ff = jnp.zeros_like(x1)
    for c in range(D_FF // FF_CHUNK):
        h = _dot(xb, wu_ref[:, c * FF_CHUNK:(c + 1) * FF_CHUNK])
        h = jnp.square(jnp.maximum(h, 0.0))
        ff = ff + _dot(h.astype(BF16), wd_ref[c * FF_CHUNK:(c + 1) * FF_CHUNK, :])
    o_ref[...] = _layer_norm(ALPHA * x1 + ff, l2w_ref[...], l2b_ref[...])


def _ffn(x, a, b, c, wo, l1w, l1b, wu, wd, l2w, l2b, tm):
    m = x.shape[0]
    row = lambda i: (i, 0)
    fixed = lambda i: (0, 0)
    once = pl.Buffered(1)
    return pl.pallas_call(
        _ffn_kernel,
        grid=(m // tm,),
        in_specs=[pl.BlockSpec((tm, D_MODEL), row),
                  pl.BlockSpec((tm, HG_WIDTH), row),
                  pl.BlockSpec((tm, RET_WIDTH), row),
                  pl.BlockSpec((tm, S5_WIDTH), row),
                  pl.BlockSpec((D_MODEL, D_MODEL), fixed, pipeline_mode=once),
                  pl.BlockSpec((1, D_MODEL), fixed),
                  pl.BlockSpec((1, D_MODEL), fixed),
                  pl.BlockSpec((D_MODEL, D_FF), fixed, pipeline_mode=once),
                  pl.BlockSpec((D_FF, D_MODEL), fixed, pipeline_mode=once),
                  pl.BlockSpec((1, D_MODEL), fixed),
                  pl.BlockSpec((1, D_MODEL), fixed)],
        out_specs=pl.BlockSpec((tm, D_MODEL), row),
        out_shape=jax.ShapeDtypeStruct((m, D_MODEL), F32),
        compiler_params=pltpu.CompilerParams(
            dimension_semantics=("parallel",), vmem_limit_bytes=VMEM_LIMIT),
        name="ffn",
    )(x, a, b, c, wo, l1w, l1b, wu, wd, l2w, l2b)


def _hgrn_kernel(x_ref, lb_ref, nw_ref, ones_ref, avg_ref, bd_ref, o_ref, st_ref, s_scr, oi_scr):
    tb = TIME_BLOCK
    t = pl.program_id(1)

    @pl.when(t == 0)
    def _():
        s_scr[...] = jnp.zeros_like(s_scr)

    hq = x_ref[:, 0:HG_WIDTH]
    z = x_ref[:, HG_WIDTH:2 * HG_WIDTH]
    v = x_ref[:, 2 * HG_WIDTH:3 * HG_WIDTH]
    gate = x_ref[:, 3 * HG_WIDTH:4 * HG_WIDTH]
    lb = lb_ref[...]
    f = lb + (1.0 - lb) * _sigmoid(z)
    kk = (1.0 - lb) * _sigmoid(-z)
    q = _silu(hq)
    pos = lax.broadcasted_iota(jnp.int32, (tb, HG_WIDTH), 0) & (HG_BLOCK - 1)

    ones = ones_ref[...]
    o = jnp.zeros((tb, HG_WIDTH), F32)
    g = kk
    vd = v
    for d in range(HG_BLOCK):
        p = _dot((q * g).astype(BF16), ones)
        if d > 0:
            p = jnp.where(pos >= d, p, 0.0)
        o = o + p * vd
        if d + 1 < HG_BLOCK:
            g = f * pltpu.roll(g, 1, 0)
            vd = pltpu.roll(vd, 1, 0)

    a = f
    for d in (1, 2, 4, 8):
        a = a * jnp.where(pos >= d, pltpu.roll(a, d, 0), 1.0)
    r = jnp.where(pos <= HG_BLOCK - 2, pltpu.roll(f, tb - 1, 0), 1.0)
    for d in (1, 2, 4, 8):
        r = r * jnp.where(pos + d <= HG_BLOCK - 1, pltpu.roll(r, tb - d, 0), 1.0)
    qt = (q * a).astype(BF16)
    kt = (kk * r).astype(BF16)
    vb = v.astype(BF16)
    for j in range(tb // HG_BLOCK):
        rows = slice(j * HG_BLOCK, (j + 1) * HG_BLOCK)
        last = (j + 1) * HG_BLOCK - 1
        for p in range(HG_HEADS // 2):
            lanes = slice(128 * p, 128 * (p + 1))
            st = s_scr[p]
            oi_scr[rows, lanes] = _dot_nt(qt[rows, lanes], st.astype(BF16))
            u = _dot_tn(vb[rows, lanes], kt[rows, lanes])
            s_scr[p] = st * a[last:last + 1, lanes] + u * bd_ref[...]

    ot = o + oi_scr[...]
    ms = _dot_hi(ot * ot, avg_ref[...])
    o_ref[...] = ot * lax.rsqrt(ms + EPS) * nw_ref[...] * _silu(gate)

    @pl.when(t == pl.num_programs(1) - 1)
    def _():
        st_ref[...] = s_scr[...]


def _hgrn(hg_in, lb, nw, ones, avg, bd, bsz, tlen):
    tb = TIME_BLOCK
    nt = tlen // tb
    fixed2 = lambda b, t: (0, 0)
    return pl.pallas_call(
        _hgrn_kernel,
        grid=(bsz, nt),
        in_specs=[pl.BlockSpec((tb, HG_COLS), lambda b, t: (b * nt + t, 0)),
                  pl.BlockSpec((1, HG_WIDTH), fixed2),
                  pl.BlockSpec((1, HG_WIDTH), fixed2),
                  pl.BlockSpec((HG_WIDTH, HG_WIDTH), fixed2),
                  pl.BlockSpec((HG_WIDTH, HG_WIDTH), fixed2),
                  pl.BlockSpec((128, 128), fixed2)],
        out_specs=[pl.BlockSpec((tb, HG_WIDTH), lambda b, t: (b * nt + t, 0)),
                   pl.BlockSpec((None, HG_HEADS // 2, 128, 128), lambda b, t: (b, 0, 0, 0))],
        out_shape=[jax.ShapeDtypeStruct((bsz * tlen, HG_WIDTH), F32),
                   jax.ShapeDtypeStruct((bsz, HG_HEADS // 2, 128, 128), F32)],
        scratch_shapes=[pltpu.VMEM((HG_HEADS // 2, 128, 128), F32),
                        pltpu.VMEM((tb, HG_WIDTH), F32)],
        compiler_params=pltpu.CompilerParams(
            dimension_semantics=("parallel", "arbitrary"), vmem_limit_bytes=VMEM_LIMIT),
        name="hgrn",
    )(hg_in, lb, nw, ones, avg, bd)


def _ret_kernel(x_ref, cos_ref, sin_ref, dq_ref, dk_ref, dmat_ref, gam_ref, bd_ref, nw_ref,
                avg_ref, o_ref, st_ref, s_scr):
    tb = TIME_BLOCK
    t = pl.program_id(1)

    @pl.when(t == 0)
    def _():
        s_scr[...] = jnp.zeros_like(s_scr)

    w = RET_QK_PAD
    cos = cos_ref[...]
    sin = sin_ref[...]
    q = x_ref[:, 0:w] * cos + x_ref[:, w:2 * w] * sin
    k = (x_ref[:, 2 * w:3 * w] * cos + x_ref[:, 3 * w:4 * w] * sin) * (RET_DK ** -0.5)
    v = x_ref[:, 4 * w:4 * w + RET_WIDTH]
    gate = x_ref[:, 4 * w + RET_WIDTH:4 * w + 2 * RET_WIDTH]
    vb = v.astype(BF16)

    lane = lax.broadcasted_iota(jnp.int32, (tb, w), 1)
    qs = jnp.concatenate(
        [jnp.where((lane >= RET_DK * h) & (lane < RET_DK * (h + 1)), q, 0.0)
         for h in range(RET_HEADS)], axis=0).astype(BF16)
    sc = _dot_nt(qs, k.astype(BF16))
    lane128 = lax.broadcasted_iota(jnp.int32, (tb, 128), 1)
    outs = []
    for p in range(RET_HEADS // 2):
        p0 = sc[(2 * p) * tb:(2 * p + 1) * tb] * dmat_ref[2 * p]
        p1 = sc[(2 * p + 1) * tb:(2 * p + 2) * tb] * dmat_ref[2 * p + 1]
        pc = jnp.concatenate([p0, p1], axis=1).astype(BF16)
        vp = v[:, 128 * p:128 * (p + 1)]
        vs = jnp.concatenate([jnp.where(lane128 < RET_DV, vp, 0.0),
                              jnp.where(lane128 >= RET_DV, vp, 0.0)], axis=0).astype(BF16)
        outs.append(_dot(pc, vs))
    o = jnp.concatenate(outs, axis=1)

    s = s_scr[...]
    o = o + _dot((q * dq_ref[...]).astype(BF16), s.astype(BF16))
    u = _dot_tn((k * dk_ref[...]).astype(BF16), vb)
    s_scr[...] = s * gam_ref[...] + u * bd_ref[...]

    mu = _dot_hi(o, avg_ref[...])
    oc = o - mu
    var = _dot_hi(oc * oc, avg_ref[...])
    o_ref[...] = oc * lax.rsqrt(var + EPS) * nw_ref[...] * _silu(gate)

    @pl.when(t == pl.num_programs(1) - 1)
    def _():
        st_ref[...] = s_scr[...]


def _ret(rt_in, cos, sin, dq, dk, dmat, gam, bd, nw, avg, bsz, tlen):
    tb = TIME_BLOCK
    nt = tlen // tb
    fixed2 = lambda b, t: (0, 0)
    return pl.pallas_call(
        _ret_kernel,
        grid=(bsz, nt),
        in_specs=[pl.BlockSpec((tb, RT_COLS), lambda b, t: (b * nt + t, 0)),
                  pl.BlockSpec((tb, RET_QK_PAD), lambda b, t: (t, 0)),
                  pl.BlockSpec((tb, RET_QK_PAD), lambda b, t: (t, 0)),
                  pl.BlockSpec((tb, RET_QK_PAD), fixed2),
                  pl.BlockSpec((tb, RET_QK_PAD), fixed2),
                  pl.BlockSpec((RET_HEADS, tb, tb), lambda b, t: (0, 0, 0)),
                  pl.BlockSpec((RET_QK_PAD, RET_WIDTH), fixed2),
                  pl.BlockSpec((RET_QK_PAD, RET_WIDTH), fixed2),
                  pl.BlockSpec((1, RET_WIDTH), fixed2),
                  pl.BlockSpec((RET_WIDTH, RET_WIDTH), fixed2)],
        out_specs=[pl.BlockSpec((tb, RET_WIDTH), lambda b, t: (b * nt + t, 0)),
                   pl.BlockSpec((None, RET_QK_PAD, RET_WIDTH), lambda b, t: (b, 0, 0))],
        out_shape=[jax.ShapeDtypeStruct((bsz * tlen, RET_WIDTH), F32),
                   jax.ShapeDtypeStruct((bsz, RET_QK_PAD, RET_WIDTH), F32)],
        scratch_shapes=[pltpu.VMEM((RET_QK_PAD, RET_WIDTH), F32)],
        compiler_params=pltpu.CompilerParams(
            dimension_semantics=("parallel", "arbitrary"), vmem_limit_bytes=VMEM_LIMIT),
        name="ret",
    )(rt_in, cos, sin, dq, dk, dmat, gam, bd, nw, avg)


def _s5_kernel(u_ref, bblk_ref, cblk_ref, apow_ref, d_ref, gw_ref, gb_ref, y_ref, st_ref, x_scr):
    tb = TIME_BLOCK
    n = S5_N
    t = pl.program_id(1)

    @pl.when(t == 0)
    def _():
        x_scr[...] = jnp.zeros_like(x_scr)

    u = u_ref[...]
    bu = _dot(u.astype(BF16), bblk_ref[...])
    xr = bu[:, 0:n]
    xi = bu[:, n:2 * n]
    row = lax.broadcasted_iota(jnp.int32, (tb, n), 0)
    d = 1
    while d < tb:
        ar = apow_ref[d - 1:d, 0:n]
        ai = apow_ref[d - 1:d, n:2 * n]
        sr = jnp.where(row >= d, pltpu.roll(xr, d, 0), 0.0)
        si = jnp.where(row >= d, pltpu.roll(xi, d, 0), 0.0)
        xr, xi = xr + ar * sr - ai * si, xi + ar * si + ai * sr
        d *= 2
    cr = x_scr[:, 0:n]
    ci = x_scr[:, n:2 * n]
    pr = apow_ref[:, 0:n]
    pi = apow_ref[:, n:2 * n]
    xr, xi = xr + pr * cr - pi * ci, xi + pr * ci + pi * cr
    x_scr[:, 0:n] = xr[tb - 1:tb]
    x_scr[:, n:2 * n] = xi[tb - 1:tb]

    xc = jnp.concatenate([xr, xi], axis=1).astype(BF16)
    y = _dot(xc, cblk_ref[...]) + d_ref[...] * u
    y = jax.nn.gelu(y, approximate=True)
    y_ref[...] = y * _sigmoid(_dot(y.astype(BF16), gw_ref[...]) + gb_ref[...])

    @pl.when(t == pl.num_programs(1) - 1)
    def _():
        st_ref[...] = x_scr[...]


def _s5(u, bblk, cblk, apow, dvec, gw, gb, bsz, tlen):
    tb = TIME_BLOCK
    nt = tlen // tb
    fixed2 = lambda b, t: (0, 0)
    return pl.pallas_call(
        _s5_kernel,
        grid=(bsz, nt),
        in_specs=[pl.BlockSpec((tb, S5_WIDTH), lambda b, t: (b * nt + t, 0)),
                  pl.BlockSpec((S5_WIDTH, 2 * S5_N), fixed2),
                  pl.BlockSpec((2 * S5_N, S5_WIDTH), fixed2),
                  pl.BlockSpec((tb, 2 * S5_N), fixed2),
                  pl.BlockSpec((1, S5_WIDTH), fixed2),
                  pl.BlockSpec((S5_WIDTH, S5_WIDTH), fixed2),
                  pl.BlockSpec((1, S5_WIDTH), fixed2)],
        out_specs=[pl.BlockSpec((tb, S5_WIDTH), lambda b, t: (b * nt + t, 0)),
                   pl.BlockSpec((None, 1, 2 * S5_N), lambda b, t: (b, 0, 0))],
        out_shape=[jax.ShapeDtypeStruct((bsz * tlen, S5_WIDTH), F32),
                   jax.ShapeDtypeStruct((bsz, 1, 2 * S5_N), F32)],
        scratch_shapes=[pltpu.VMEM((1, 2 * S5_N), F32)],
        compiler_params=pltpu.CompilerParams(
            dimension_semantics=("parallel", "arbitrary"), vmem_limit_bytes=VMEM_LIMIT),
        name="s5",
    )(u, bblk, cblk, apow, dvec, gw, gb)


STEP_ROWS = 8


def _hgrn_step_kernel(hq_ref, hf_ref, hi_ref, hg_ref, lb_ref, nw_ref, s_ref, o_ref, sn_ref):
    lb = lb_ref[...]
    z = hf_ref[...]
    f = lb + (1.0 - lb) * _sigmoid(z)
    kk = (1.0 - lb) * _sigmoid(-z)
    q = _silu(hq_ref[...])
    sn = f * s_ref[...] + kk * hi_ref[...]
    sn_ref[...] = sn
    o = jnp.sum(q * sn, axis=2, keepdims=True)
    ms = jnp.mean(o * o, -1, keepdims=True)
    o_ref[...] = o * lax.rsqrt(ms + EPS) * nw_ref[...] * _silu(hg_ref[...])


def _hgrn_step(hq_c, hf_c, hi_r, hg_r, lb_c, nw_r, s0):
    n = s0.shape[0]
    r = STEP_ROWS
    col = pl.BlockSpec((r, HG_HEADS, HG_DK, 1), lambda i: (i, 0, 0, 0))
    rowv = pl.BlockSpec((r, HG_HEADS, 1, HG_DV), lambda i: (i, 0, 0, 0))
    st = pl.BlockSpec((r, HG_HEADS, HG_DK, HG_DV), lambda i: (i, 0, 0, 0))
    return pl.pallas_call(
        _hgrn_step_kernel,
        grid=(n // r,),
        in_specs=[col, col, rowv, rowv,
                  pl.BlockSpec((HG_HEADS, HG_DK, 1), lambda i: (0, 0, 0)),
                  pl.BlockSpec((1, 1, 1, HG_DV), lambda i: (0, 0, 0, 0)),
                  st],
        out_specs=[rowv, st],
        out_shape=[jax.ShapeDtypeStruct((n, HG_HEADS, 1, HG_DV), F32),
                   jax.ShapeDtypeStruct(s0.shape, F32)],
        compiler_params=pltpu.CompilerParams(
            dimension_semantics=("parallel",), vmem_limit_bytes=VMEM_LIMIT),
        name="hgrn_step",
    )(hq_c, hf_c, hi_r, hg_r, lb_c, nw_r, s0)


def _ret_step_kernel(rq_ref, rqs_ref, rk_ref, rks_ref, rv_ref, rg_ref, cos_ref, sin_ref,
                     gam_ref, nw_ref, s_ref, o_ref, sn_ref):
    cos = cos_ref[...]
    sin = sin_ref[...]
    q = rq_ref[...] * cos + rqs_ref[...] * sin
    k = (rk_ref[...] * cos + rks_ref[...] * sin) * (RET_DK ** -0.5)
    sn = gam_ref[...] * s_ref[...] + k * rv_ref[...]
    sn_ref[...] = sn
    o = jnp.sum(q * sn, axis=2, keepdims=True)
    mu = jnp.mean(o, -1, keepdims=True)
    oc = o - mu
    var = jnp.mean(oc * oc, -1, keepdims=True)
    o_ref[...] = oc * lax.rsqrt(var + EPS) * nw_ref[...] * _silu(rg_ref[...])


def _ret_step(rq_c, rqs_c, rk_c, rks_c, rv_r, rg_r, cos_c, sin_c, gam, nw_r, s0):
    n = s0.shape[0]
    r = STEP_ROWS
    col = pl.BlockSpec((r, RET_HEADS, RET_DK, 1), lambda i: (i, 0, 0, 0))
    rowv = pl.BlockSpec((r, RET_HEADS, 1, RET_DV), lambda i: (i, 0, 0, 0))
    st = pl.BlockSpec((r, RET_HEADS, RET_DK, RET_DV), lambda i: (i, 0, 0, 0))
    par = pl.BlockSpec((RET_HEADS, RET_DK, 1), lambda i: (0, 0, 0))
    return pl.pallas_call(
        _ret_step_kernel,
        grid=(n // r,),
        in_specs=[col, col, col, col, rowv, rowv, par, par,
                  pl.BlockSpec((RET_HEADS, 1, 1), lambda i: (0, 0, 0)),
                  pl.BlockSpec((RET_HEADS, 1, RET_DV), lambda i: (0, 0, 0)),
                  st],
        out_specs=[rowv, st],
        out_shape=[jax.ShapeDtypeStruct((n, RET_HEADS, 1, RET_DV), F32),
                   jax.ShapeDtypeStruct(s0.shape, F32)],
        compiler_params=pltpu.CompilerParams(
            dimension_semantics=("parallel",), vmem_limit_bytes=VMEM_LIMIT),
        name="ret_step",
    )(rq_c, rqs_c, rk_c, rks_c, rv_r, rg_r, cos_c, sin_c, gam, nw_r, s0)


def _s5_step_kernel(u_ref, xr_ref, xi_ref, bblk_ref, cblk_ref, ab_ref, d_ref, gw_ref, gb_ref,
                    y_ref, xrn_ref, xin_ref):
    n = S5_N
    u = u_ref[...]
    bu = _dot_hi(u, bblk_ref[...])
    abr = ab_ref[:, 0:n]
    abi = ab_ref[:, n:2 * n]
    x0r = xr_ref[...]
    x0i = xi_ref[...]
    xr = abr * x0r - abi * x0i + bu[:, 0:n]
    xi = abr * x0i + abi * x0r + bu[:, n:2 * n]
    xrn_ref[...] = xr
    xin_ref[...] = xi
    y = _dot_hi(jnp.concatenate([xr, xi], axis=1), cblk_ref[...]) + d_ref[...] * u
    y = jax.nn.gelu(y, approximate=True)
    y_ref[...] = y * _sigmoid(_dot(y.astype(BF16), gw_ref[...]) + gb_ref[...])


def _s5_step(u, x0r, x0i, bblk, cblk, ab, dvec, gw, gb):
    n = u.shape[0]
    return pl.pallas_call(
        _s5_step_kernel,
        out_shape=[jax.ShapeDtypeStruct((n, S5_WIDTH), F32),
                   jax.ShapeDtypeStruct((n, S5_N), F32),
                   jax.ShapeDtypeStruct((n, S5_N), F32)],
        compiler_params=pltpu.CompilerParams(vmem_limit_bytes=VMEM_LIMIT),
        name="s5_step",
    )(u, x0r, x0i, bblk, cblk, ab, dvec, gw, gb)


def _swap_halves_perm():
    half = RET_DK // 2
    idx = np.arange(RET_QK)
    j = idx % RET_DK
    return np.where(j < half, idx + half, idx - half)


def _pad_cols(w, width):
    return jnp.pad(w, ((0, 0), (0, width - w.shape[1])))


def _proj_weight(w_in_l):
    cuts = np.cumsum([HG_WIDTH] * 4 + [RET_QK] * 2 + [RET_WIDTH] * 2 + [S5_WIDTH])[:-1]
    hq, hf, hi, hg, rq, rk, rv, rg, su = jnp.split(w_in_l, [int(c) for c in cuts], axis=1)
    perm = _swap_halves_perm()
    pieces = [hq, hf, hi, hg,
              _pad_cols(rq, RET_QK_PAD), _pad_cols(rq[:, perm], RET_QK_PAD),
              _pad_cols(rk, RET_QK_PAD), _pad_cols(rk[:, perm], RET_QK_PAD),
              rv, rg, su]
    return jnp.concatenate(pieces, axis=1).astype(BF16)


def _rope_tables(pos):
    half = RET_DK // 2
    inv = ROPE_BASE ** (-jnp.arange(half, dtype=F32) / half)
    ang = pos[:, None] * inv[None, :]
    cos = jnp.cos(ang)
    sin = jnp.sin(ang)
    cos_h = jnp.concatenate([cos, cos], axis=1)
    sin_h = jnp.concatenate([-sin, sin], axis=1)
    return jnp.tile(cos_h, (1, RET_HEADS)), jnp.tile(sin_h, (1, RET_HEADS))


def _block_diag_mask(rows_per, cols_per, nblk):
    r = np.arange(rows_per * nblk)[:, None] // rows_per
    c = np.arange(cols_per * nblk)[None, :] // cols_per
    return (r == c).astype(np.float32)


def _s5_params(log_dt, a_re, a_im, b_re, b_im, c_re, c_im):
    dt = jnp.exp(log_dt)[:, None]
    mag = jnp.exp(dt * a_re)
    ab_re = mag * jnp.cos(dt * a_im)
    ab_im = mag * jnp.sin(dt * a_im)
    den = a_re * a_re + a_im * a_im
    nr = ab_re - 1.0
    g_re = (nr * a_re + ab_im * a_im) / den
    g_im = (ab_im * a_re - nr * a_im) / den
    bb_re = g_re[..., None] * b_re - g_im[..., None] * b_im
    bb_im = g_re[..., None] * b_im + g_im[..., None] * b_re
    eye = jnp.eye(S5_GROUPS, dtype=F32)

    def in_blk(bb):
        m = eye[:, None, :, None] * jnp.transpose(bb, (0, 2, 1))[:, :, None, :]
        return m.reshape(S5_WIDTH, S5_N)

    def out_blk(c):
        m = eye[:, None, :, None] * jnp.transpose(c, (0, 2, 1))[:, :, None, :]
        return m.reshape(S5_N, S5_WIDTH)

    bblk = jnp.concatenate([in_blk(bb_re), in_blk(bb_im)], axis=1)
    cblk = jnp.concatenate([out_blk(c_re), -out_blk(c_im)], axis=0)
    m = jnp.arange(1, TIME_BLOCK + 1, dtype=F32)[:, None, None]
    pmag = jnp.exp(m * (dt * a_re)[None])
    pang = m * (dt * a_im)[None]
    apow = jnp.concatenate([(pmag * jnp.cos(pang)).reshape(TIME_BLOCK, S5_N),
                            (pmag * jnp.sin(pang)).reshape(TIME_BLOCK, S5_N)], axis=1)
    ab = jnp.concatenate([ab_re.reshape(1, S5_N), ab_im.reshape(1, S5_N)], axis=1)
    return bblk, cblk, apow, ab


def kernel(x_prompt, x_sample, state_hgrn, state_ret, state_s5_re, state_s5_im, w_in, hgrn_lb_logits, hgrn_norm_w, ret_norm_w, s5_log_dt, s5_a_re, s5_a_im, s5_b_re, s5_b_im, s5_c_re, s5_c_im, s5_d, s5_glu_w, s5_glu_b, w_out, ln1_w, ln1_b, w_up, w_down, ln2_w, ln2_b):
    bp, tp, _ = x_prompt.shape
    bs = x_sample.shape[0]
    tb = TIME_BLOCK

    lb_prob = jax.nn.softmax(hgrn_lb_logits.astype(F32), axis=0)
    lower_bounds = jnp.cumsum(lb_prob, axis=0) - lb_prob[0:1]

    cos_p, sin_p = _rope_tables(jnp.arange(tp, dtype=F32))
    cos_s, sin_s = _rope_tables(PAST_LEN + jnp.arange(1, dtype=F32))
    cos_pp = _pad_cols(cos_p, RET_QK_PAD)
    sin_pp = _pad_cols(sin_p, RET_QK_PAD)
    log_gamma = jnp.log1p(-jnp.exp2(-5.0 - jnp.arange(RET_HEADS, dtype=F32)))
    gamma = jnp.exp(log_gamma)
    tt = jnp.arange(tb, dtype=F32)
    lg_lane = _pad_cols(jnp.repeat(log_gamma, RET_DK)[None, :], RET_QK_PAD)
    dq = jnp.exp((tt[:, None] + 1.0) * lg_lane)
    dk = jnp.exp((tb - 1.0 - tt[:, None]) * lg_lane)
    diff = tt[:, None] - tt[None, :]
    dmat = jnp.where(diff >= 0, jnp.exp(jnp.maximum(diff, 0.0)[None] * log_gamma[:, None, None]), 0.0)
    gam_rows = jnp.exp(tb * jnp.pad(jnp.repeat(log_gamma, RET_DK), (0, RET_QK_PAD - RET_QK)))
    gam_tbl = jnp.broadcast_to(gam_rows[:, None], (RET_QK_PAD, RET_WIDTH))
    ret_bd = jnp.asarray(np.pad(_block_diag_mask(RET_DK, RET_DV, RET_HEADS),
                                ((0, RET_QK_PAD - RET_QK), (0, 0))))
    hg_bd = jnp.asarray(_block_diag_mask(HG_DV, HG_DK, 2))
    seg = _block_diag_mask(64, 64, 6)
    ones_bf = jnp.asarray(seg).astype(BF16)
    avg = jnp.asarray(seg / 64.0)

    xp = x_prompt.reshape(bp * tp, D_MODEL)
    xs = x_sample.reshape(bs, D_MODEL)
    outs = {k: [] for k in ("hg_p", "rt_p", "re_p", "im_p", "hg_s", "rt_s", "re_s", "im_s")}
    for l in range(DEPTH):
        wp = _proj_weight(w_in[l])
        wo = w_out[l].astype(BF16)
        wu = w_up[l].astype(BF16)
        wd = w_down[l].astype(BF16)
        lb = lower_bounds[l]
        hnw = jnp.tile(hgrn_norm_w[l], HG_HEADS)[None, :]
        rnw = ret_norm_w[l][None, :]
        bblk, cblk, apow, ab = _s5_params(s5_log_dt[l], s5_a_re[l], s5_a_im[l], s5_b_re[l],
                                          s5_b_im[l], s5_c_re[l], s5_c_im[l])
        dvec = s5_d[l][None, :]
        gw = s5_glu_w[l].astype(BF16)
        gb = s5_glu_b[l][None, :]
        ln = (ln1_w[l][None, :], ln1_b[l][None, :], wu, wd, ln2_w[l][None, :], ln2_b[l][None, :])

        hg_in, rt_in, s5_in = _proj(xp, wp, 512)
        o_hg, st_hg = _hgrn(hg_in, lb[None, :], hnw, ones_bf, avg, hg_bd, bp, tp)
        o_rt, st_rt = _ret(rt_in, cos_pp, sin_pp, dq, dk, dmat, gam_tbl, ret_bd, rnw, avg, bp, tp)
        y5, st_s5 = _s5(s5_in, bblk.astype(BF16), cblk.astype(BF16), apow, dvec, gw, gb, bp, tp)
        xp = _ffn(xp, o_hg, o_rt, y5, wo, *ln, 512)
        hg_state = jnp.stack(
            [st_hg[:, h // 2, 64 * (h % 2):64 * (h % 2 + 1), 64 * (h % 2):64 * (h % 2 + 1)]
             for h in range(HG_HEADS)], axis=1)
        outs["hg_p"].append(jnp.swapaxes(hg_state, -1, -2))
        outs["rt_p"].append(jnp.stack(
            [st_rt[:, RET_DK * h:RET_DK * (h + 1), RET_DV * h:RET_DV * (h + 1)]
             for h in range(RET_HEADS)], axis=1))
        outs["re_p"].append(st_s5[:, 0, :S5_N].reshape(bp, S5_GROUPS, S5_STATE))
        outs["im_p"].append(st_s5[:, 0, S5_N:].reshape(bp, S5_GROUPS, S5_STATE))

        hg_in, rt_in, s5_in = _proj(xs, wp, bs)

        def col(a, heads, dk_):
            return a.reshape(bs, heads, dk_, 1)

        def rowv(a, heads, dv_):
            return a.reshape(bs, heads, 1, dv_)

        o_hg, sn_hg = _hgrn_step(
            col(hg_in[:, 0:HG_WIDTH], HG_HEADS, HG_DK),
            col(hg_in[:, HG_WIDTH:2 * HG_WIDTH], HG_HEADS, HG_DK),
            rowv(hg_in[:, 2 * HG_WIDTH:3 * HG_WIDTH], HG_HEADS, HG_DV),
            rowv(hg_in[:, 3 * HG_WIDTH:4 * HG_WIDTH], HG_HEADS, HG_DV),
            lb.reshape(HG_HEADS, HG_DK, 1), hgrn_norm_w[l].reshape(1, 1, 1, HG_DV), state_hgrn[l])
        w = RET_QK_PAD
        o_rt, sn_rt = _ret_step(
            col(rt_in[:, 0:RET_QK], RET_HEADS, RET_DK),
            col(rt_in[:, w:w + RET_QK], RET_HEADS, RET_DK),
            col(rt_in[:, 2 * w:2 * w + RET_QK], RET_HEADS, RET_DK),
            col(rt_in[:, 3 * w:3 * w + RET_QK], RET_HEADS, RET_DK),
            rowv(rt_in[:, 4 * w:4 * w + RET_WIDTH], RET_HEADS, RET_DV),
            rowv(rt_in[:, 4 * w + RET_WIDTH:4 * w + 2 * RET_WIDTH], RET_HEADS, RET_DV),
            cos_s.reshape(RET_HEADS, RET_DK, 1), sin_s.reshape(RET_HEADS, RET_DK, 1),
            gamma.reshape(RET_HEADS, 1, 1), ret_norm_w[l].reshape(RET_HEADS, 1, RET_DV), state_ret[l])
        y5, xr_n, xi_n = _s5_step(s5_in, state_s5_re[l].reshape(bs, S5_N),
                                  state_s5_im[l].reshape(bs, S5_N), bblk, cblk, ab, dvec, gw, gb)
        xs = _ffn(xs, o_hg.reshape(bs, HG_WIDTH), o_rt.reshape(bs, RET_WIDTH), y5, wo, *ln, bs)
        outs["hg_s"].append(sn_hg)
        outs["rt_s"].append(sn_rt)
        outs["re_s"].append(xr_n.reshape(bs, S5_GROUPS, S5_STATE))
        outs["im_s"].append(xi_n.reshape(bs, S5_GROUPS, S5_STATE))

    return (xp.reshape(bp, tp, D_MODEL), xs.reshape(bs, 1, D_MODEL),
            jnp.stack(outs["hg_p"]), jnp.stack(outs["rt_p"]), jnp.stack(outs["re_p"]),
            jnp.stack(outs["im_p"]), jnp.stack(outs["hg_s"]), jnp.stack(outs["rt_s"]),
            jnp.stack(outs["re_s"]), jnp.stack(outs["im_s"]))
```

```python
import jax
import jax.numpy as jnp
import numpy as np
from jax import lax
from jax.experimental import pallas as pl
from jax.experimental.pallas import tpu as pltpu

F32 = jnp.float32
BF16 = jnp.bfloat16
HIGHEST = lax.Precision.HIGHEST

D_MODEL = 1024
DEPTH = 4
PAST_LEN = 16384
HG_WIDTH = 384
HG_HEADS = 6
HG_DK = 64
HG_DV = 64
RET_WIDTH = 384
RET_HEADS = 6
RET_DK = 32
RET_DV = 64
RET_QK = RET_HEADS * RET_DK
RET_QK_PAD = 256
S5_WIDTH = 256
S5_GROUP = 16
S5_GROUPS = 16
S5_STATE = 64
S5_N = S5_GROUPS * S5_STATE
D_FF = 4 * D_MODEL
ROPE_BASE = 10000.0
EPS = 1e-5
ALPHA = (2.0 * DEPTH) ** 0.25

HG_COLS = 4 * HG_WIDTH
RT_COLS = 4 * RET_QK_PAD + 2 * RET_WIDTH
PROJ_COLS = HG_COLS + RT_COLS + S5_WIDTH

SUBLANES = 8
TIME_BLOCK = 128
BLOCK_LEN = TIME_BLOCK // SUBLANES
N_PAIRS = BLOCK_LEN * (BLOCK_LEN + 1) // 2
FF_CHUNK = 1024
VMEM_LIMIT = 56 * 1024 * 1024


def _dot(a, b):
    return jnp.dot(a, b, preferred_element_type=F32)


def _dot_nt(a, b):
    return lax.dot_general(a, b, (((1,), (1,)), ((), ())), preferred_element_type=F32)


def _dot_tn(a, b):
    return lax.dot_general(a, b, (((0,), (0,)), ((), ())), preferred_element_type=F32)


def _dot_hi(a, b):
    return jnp.dot(a, b, preferred_element_type=F32, precision=HIGHEST)


def _sigmoid(x):
    return jax.nn.sigmoid(x)


def _silu(x):
    return x * jax.nn.sigmoid(x)


def _layer_norm(y, w, b):
    mu = jnp.mean(y, -1, keepdims=True)
    yc = y - mu
    var = jnp.mean(yc * yc, -1, keepdims=True)
    return yc * lax.rsqrt(var + EPS) * w + b


def _head_sum_bf16(x, ones256, ones128):
    return jnp.concatenate([_dot(x[:, 0:256], ones256), _dot(x[:, 256:384], ones128)], axis=1)


def _head_sum(x, ones256, ones128):
    hi = x.astype(BF16)
    lo = (x - hi.astype(F32)).astype(BF16)
    return _head_sum_bf16(hi, ones256, ones128) + _head_sum_bf16(lo, ones256, ones128)


def _proj_kernel(x_ref, w_ref, hg_ref, rt_ref, s5_ref):
    x = x_ref[...].astype(BF16)
    hg_ref[...] = _dot(x, w_ref[:, 0:HG_COLS])
    rt_ref[...] = _dot(x, w_ref[:, HG_COLS:HG_COLS + RT_COLS])
    s5_ref[...] = _dot(x, w_ref[:, HG_COLS + RT_COLS:PROJ_COLS])


def _proj(x, w, tm):
    m = x.shape[0]
    return pl.pallas_call(
        _proj_kernel,
        grid=(m // tm,),
        in_specs=[pl.BlockSpec((tm, D_MODEL), lambda i: (i, 0)),
                  pl.BlockSpec((D_MODEL, PROJ_COLS), lambda i: (0, 0),
                               pipeline_mode=pl.Buffered(1))],
        out_specs=[pl.BlockSpec((tm, HG_COLS), lambda i: (i, 0)),
                   pl.BlockSpec((tm, RT_COLS), lambda i: (i, 0)),
                   pl.BlockSpec((tm, S5_WIDTH), lambda i: (i, 0))],
        out_shape=[jax.ShapeDtypeStruct((m, HG_COLS), F32),
                   jax.ShapeDtypeStruct((m, RT_COLS), F32),
                   jax.ShapeDtypeStruct((m, S5_WIDTH), F32)],
        compiler_params=pltpu.CompilerParams(
            dimension_semantics=("parallel",), vmem_limit_bytes=VMEM_LIMIT),
        name="proj",
    )(x, w)


def _ffn_kernel(x_ref, a_ref, b_ref, c_ref, wo_ref, l1w_ref, l1b_ref, wu_ref, wd_ref,
                l2w_ref, l2b_ref, o_ref):
    x = x_ref[...]
    mixed = (_dot(a_ref[...].astype(BF16), wo_ref[0:HG_WIDTH, :])
             + _dot(b_ref[...].astype(BF16), wo_ref[HG_WIDTH:HG_WIDTH + RET_WIDTH, :])
             + _dot(c_ref[...].astype(BF16), wo_ref[HG_WIDTH + RET_WIDTH:D_MODEL, :]))
    x1 = _layer_norm(ALPHA * x + mixed, l1w_ref[...], l1b_ref[...])
    xb = x1.astype(BF16)
    ff = jnp.zeros_like(x1)
    for c in range(D_FF // FF_CHUNK):
        h = _dot(xb, wu_ref[:, c * FF_CHUNK:(c + 1) * FF_CHUNK])
        h = jnp.square(jnp.maximum(h, 0.0))
        ff = ff + _dot(h.astype(BF16), wd_ref[c * FF_CHUNK:(c + 1) * FF_CHUNK, :])
    o_ref[...] = _layer_norm(ALPHA * x1 + ff, l2w_ref[...], l2b_ref[...])


def _ffn(x, a, b, c, wo, l1w, l1b, wu, wd, l2w, l2b, tm):
    m = x.shape[0]
    row = lambda i: (i, 0)
    fixed = lambda i: (0, 0)
    once = pl.Buffered(1)
    return pl.pallas_call(
        _ffn_kernel,
        grid=(m // tm,),
        in_specs=[pl.BlockSpec((tm, D_MODEL), row),
                  pl.BlockSpec((tm, HG_WIDTH), row),
                  pl.BlockSpec((tm, RET_WIDTH), row),
                  pl.BlockSpec((tm, S5_WIDTH), row),
                  pl.BlockSpec((D_MODEL, D_MODEL), fixed, pipeline_mode=once),
                  pl.BlockSpec((1, D_MODEL), fixed),
                  pl.BlockSpec((1, D_MODEL), fixed),
                  pl.BlockSpec((D_MODEL, D_FF), fixed, pipeline_mode=once),
                  pl.BlockSpec((D_FF, D_MODEL), fixed, pipeline_mode=once),
                  pl.BlockSpec((1, D_MODEL), fixed),
                  pl.BlockSpec((1, D_MODEL), fixed)],
        out_specs=pl.BlockSpec((tm, D_MODEL), row),
        out_shape=jax.ShapeDtypeStruct((m, D_MODEL), F32),
        compiler_params=pltpu.CompilerParams(
            dimension_semantics=("parallel",), vmem_limit_bytes=VMEM_LIMIT),
        name="ffn",
    )(x, a, b, c, wo, l1w, l1b, wu, wd, l2w, l2b)


def _hgrn_kernel(x_ref, lb_ref, nw_ref, ones256_ref, ones128_ref, bd_ref, o_ref, st_ref,
                 s_scr, qt_scr, kt_scr, v_scr, oi_scr):
    npos = BLOCK_LEN
    sl = SUBLANES
    t = pl.program_id(1)

    @pl.when(t == 0)
    def _():
        s_scr[...] = jnp.zeros_like(s_scr)

    lb = lb_ref[...]
    ones256 = ones256_ref[...]
    ones128 = ones128_ref[...]
    f, kk, q, v = [], [], [], []
    for p in range(npos):
        rows = slice(sl * p, sl * (p + 1))
        hq = x_ref[rows, 0:HG_WIDTH]
        z = x_ref[rows, HG_WIDTH:2 * HG_WIDTH]
        f.append(lb + (1.0 - lb) * _sigmoid(z))
        kk.append((1.0 - lb) * _sigmoid(-z))
        q.append(_silu(hq))
        v.append(x_ref[rows, 2 * HG_WIDTH:3 * HG_WIDTH])

    g = list(kk)
    e_rows = []
    for d in range(npos):
        if d > 0:
            g = [None] * d + [f[p] * g[p - 1] for p in range(d, npos)]
        e_rows.extend(q[p] * g[p] for p in range(d, npos))
    e_all = jnp.concatenate(e_rows, axis=0).astype(BF16)
    p_all = _head_sum_bf16(e_all, ones256, ones128)
    o = [None] * npos
    i = 0
    for d in range(npos):
        for p in range(d, npos):
            term = p_all[sl * i:sl * (i + 1)] * v[p - d]
            o[p] = term if o[p] is None else o[p] + term
            i += 1

    a = [f[0]]
    for p in range(1, npos):
        a.append(a[-1] * f[p])
    r = [None] * npos
    r[npos - 1] = jnp.ones_like(f[0])
    for p in range(npos - 2, -1, -1):
        r[p] = r[p + 1] * f[p + 1]
    npair = HG_HEADS // 2
    for p in range(npos):
        rows = slice(sl * p, sl * (p + 1))
        qt = q[p] * a[p]
        kt = kk[p] * r[p]
        for pr in range(npair):
            lanes = slice(128 * pr, 128 * (pr + 1))
            qt_scr[pr, rows, :] = qt[:, lanes]
            kt_scr[pr, rows, :] = kt[:, lanes]
            v_scr[pr, rows, :] = v[p][:, lanes]
    cd = a[npos - 1]
    for j in range(sl):
        blk = pl.ds(j, npos, stride=sl)
        for pr in range(npair):
            lanes = slice(128 * pr, 128 * (pr + 1))
            st = s_scr[pr]
            oi_scr[pr, blk, :] = _dot_nt(qt_scr[pr, blk, :].astype(BF16), st.astype(BF16))
            u = _dot_tn(v_scr[pr, blk, :].astype(BF16), kt_scr[pr, blk, :].astype(BF16))
            s_scr[pr] = st * cd[j:j + 1, lanes] + u * bd_ref[...]

    ot = (jnp.concatenate(o, axis=0)
          + jnp.concatenate([oi_scr[pr] for pr in range(npair)], axis=1))
    ms = _head_sum(ot * ot, ones256, ones128) * (1.0 / HG_DV)
    gate = x_ref[:, 3 * HG_WIDTH:4 * HG_WIDTH]
    o_ref[...] = ot * lax.rsqrt(ms + EPS) * nw_ref[...] * _silu(gate)

    @pl.when(t == pl.num_programs(1) - 1)
    def _():
        st_ref[...] = s_scr[...]


def _hgrn(hg_in, lb, nw, ones256, ones128, bd, bsz, tlen):
    tb = TIME_BLOCK
    nt = tlen // tb
    fixed2 = lambda b, t: (0, 0)
    return pl.pallas_call(
        _hgrn_kernel,
        grid=(bsz, nt),
        in_specs=[pl.BlockSpec((tb, HG_COLS), lambda b, t: (b * nt + t, 0)),
                  pl.BlockSpec((1, HG_WIDTH), fixed2),
                  pl.BlockSpec((1, HG_WIDTH), fixed2),
                  pl.BlockSpec((256, 256), fixed2),
                  pl.BlockSpec((128, 128), fixed2),
                  pl.BlockSpec((128, 128), fixed2)],
        out_specs=[pl.BlockSpec((tb, HG_WIDTH), lambda b, t: (b * nt + t, 0)),
                   pl.BlockSpec((None, HG_HEADS // 2, 128, 128), lambda b, t: (b, 0, 0, 0))],
        out_shape=[jax.ShapeDtypeStruct((bsz * tlen, HG_WIDTH), F32),
                   jax.ShapeDtypeStruct((bsz, HG_HEADS // 2, 128, 128), F32)],
        scratch_shapes=[pltpu.VMEM((HG_HEADS // 2, 128, 128), F32),
                        pltpu.VMEM((HG_HEADS // 2, tb, 128), F32),
                        pltpu.VMEM((HG_HEADS // 2, tb, 128), F32),
                        pltpu.VMEM((HG_HEADS // 2, tb, 128), F32),
                        pltpu.VMEM((HG_HEADS // 2, tb, 128), F32)],
        compiler_params=pltpu.CompilerParams(
            dimension_semantics=("parallel", "arbitrary"), vmem_limit_bytes=VMEM_LIMIT),
        name="hgrn",
    )(hg_in, lb, nw, ones256, ones128, bd)


def _ret_kernel(x_ref, cos_ref, sin_ref, dq_ref, dk_ref, dmat_ref, gam_ref, bd_ref, nw_ref,
                ones256_ref, ones128_ref, o_ref, st_ref, s_scr):
    tb = TIME_BLOCK
    t = pl.program_id(1)

    @pl.when(t == 0)
    def _():
        s_scr[...] = jnp.zeros_like(s_scr)

    w = RET_QK_PAD
    cos = cos_ref[...]
    sin = sin_ref[...]
    q = x_ref[:, 0:w] * cos + x_ref[:, w:2 * w] * sin
    k = (x_ref[:, 2 * w:3 * w] * cos + x_ref[:, 3 * w:4 * w] * sin) * (RET_DK ** -0.5)
    v = x_ref[:, 4 * w:4 * w + RET_WIDTH]
    gate = x_ref[:, 4 * w + RET_WIDTH:4 * w + 2 * RET_WIDTH]
    vb = v.astype(BF16)

    lane = lax.broadcasted_iota(jnp.int32, (tb, w), 1)
    qs = jnp.concatenate(
        [jnp.where((lane >= RET_DK * h) & (lane < RET_DK * (h + 1)), q, 0.0)
         for h in range(RET_HEADS)], axis=0).astype(BF16)
    sc = _dot_nt(qs, k.astype(BF16))
    lane128 = lax.broadcasted_iota(jnp.int32, (tb, 128), 1)
    outs = []
    for p in range(RET_HEADS // 2):
        p0 = sc[(2 * p) * tb:(2 * p + 1) * tb] * dmat_ref[2 * p]
        p1 = sc[(2 * p + 1) * tb:(2 * p + 2) * tb] * dmat_ref[2 * p + 1]
        pc = jnp.concatenate([p0, p1], axis=1).astype(BF16)
        vp = v[:, 128 * p:128 * (p + 1)]
        vs = jnp.concatenate([jnp.where(lane128 < RET_DV, vp, 0.0),
                              jnp.where(lane128 >= RET_DV, vp, 0.0)], axis=0).astype(BF16)
        outs.append(_dot(pc, vs))
    o = jnp.concatenate(outs, axis=1)

    s = s_scr[...]
    o = o + _dot((q * dq_ref[...]).astype(BF16), s.astype(BF16))
    u = _dot_tn((k * dk_ref[...]).astype(BF16), vb)
    s_scr[...] = s * gam_ref[...] + u * bd_ref[...]

    ones256 = ones256_ref[...]
    ones128 = ones128_ref[...]
    mu = _head_sum(o, ones256, ones128) * (1.0 / RET_DV)
    oc = o - mu
    var = _head_sum(oc * oc, ones256, ones128) * (1.0 / RET_DV)
    o_ref[...] = oc * lax.rsqrt(var + EPS) * nw_ref[...] * _silu(gate)

    @pl.when(t == pl.num_programs(1) - 1)
    def _():
        st_ref[...] = s_scr[...]


def _ret(rt_in, cos, sin, dq, dk, dmat, gam, bd, nw, ones256, ones128, bsz, tlen):
    tb = TIME_BLOCK
    nt = tlen // tb
    fixed2 = lambda b, t: (0, 0)
    return pl.pallas_call(
        _ret_kernel,
        grid=(bsz, nt),
        in_specs=[pl.BlockSpec((tb, RT_COLS), lambda b, t: (b * nt + t, 0)),
                  pl.BlockSpec((tb, RET_QK_PAD), lambda b, t: (t, 0)),
                  pl.BlockSpec((tb, RET_QK_PAD), lambda b, t: (t, 0)),
                  pl.BlockSpec((tb, RET_QK_PAD), fixed2),
                  pl.BlockSpec((tb, RET_QK_PAD), fixed2),
                  pl.BlockSpec((RET_HEADS, tb, tb), lambda b, t: (0, 0, 0)),
                  pl.BlockSpec((RET_QK_PAD, RET_WIDTH), fixed2),
                  pl.BlockSpec((RET_QK_PAD, RET_WIDTH), fixed2),
                  pl.BlockSpec((1, RET_WIDTH), fixed2),
                  pl.BlockSpec((256, 256), fixed2),
                  pl.BlockSpec((128, 128), fixed2)],
        out_specs=[pl.BlockSpec((tb, RET_WIDTH), lambda b, t: (b * nt + t, 0)),
                   pl.BlockSpec((None, RET_QK_PAD, RET_WIDTH), lambda b, t: (b, 0, 0))],
        out_shape=[jax.ShapeDtypeStruct((bsz * tlen, RET_WIDTH), F32),
                   jax.ShapeDtypeStruct((bsz, RET_QK_PAD, RET_WIDTH), F32)],
        scratch_shapes=[pltpu.VMEM((RET_QK_PAD, RET_WIDTH), F32)],
        compiler_params=pltpu.CompilerParams(
            dimension_semantics=("parallel", "arbitrary"), vmem_limit_bytes=VMEM_LIMIT),
        name="ret",
    )(rt_in, cos, sin, dq, dk, dmat, gam, bd, nw, ones256, ones128)


def _cmul(ar, ai, xr, xi):
    return ar * xr - ai * xi, ar * xi + ai * xr


def _s5_kernel(u_ref, bblk_ref, cblk_ref, apow_ref, aend_ref, d_ref, gw_ref, gb_ref, y_ref, st_ref,
               x_scr):
    npos = BLOCK_LEN
    sl = SUBLANES
    n = S5_N
    t = pl.program_id(1)

    @pl.when(t == 0)
    def _():
        x_scr[...] = jnp.zeros_like(x_scr)

    def power(m):
        return apow_ref[m - 1:m, 0:n], apow_ref[m - 1:m, n:2 * n]

    u = u_ref[...]
    bu = _dot(u.astype(BF16), bblk_ref[...])
    a1r, a1i = power(1)
    xr = [bu[0:sl, 0:n]]
    xi = [bu[0:sl, n:2 * n]]
    for p in range(1, npos):
        rows = slice(sl * p, sl * (p + 1))
        mr, mi = _cmul(a1r, a1i, xr[-1], xi[-1])
        xr.append(bu[rows, 0:n] + mr)
        xi.append(bu[rows, n:2 * n] + mi)

    blk = lax.broadcasted_iota(jnp.int32, (sl, n), 0)
    cr, ci = xr[-1], xi[-1]
    s = 1
    while s < sl:
        pr_, pi_ = power(npos * s)
        sr = jnp.where(blk >= s, pltpu.roll(cr, s, 0), 0.0)
        si = jnp.where(blk >= s, pltpu.roll(ci, s, 0), 0.0)
        mr, mi = _cmul(pr_, pi_, sr, si)
        cr, ci = cr + mr, ci + mi
        s *= 2
    x0r = x_scr[:, 0:n]
    x0i = x_scr[:, n:2 * n]
    mr, mi = _cmul(aend_ref[:, 0:n], aend_ref[:, n:2 * n], x0r, x0i)
    cr, ci = cr + mr, ci + mi
    x_scr[:, 0:n] = cr[sl - 1:sl]
    x_scr[:, n:2 * n] = ci[sl - 1:sl]
    inr = jnp.where(blk >= 1, pltpu.roll(cr, 1, 0), x0r)
    ini = jnp.where(blk >= 1, pltpu.roll(ci, 1, 0), x0i)
    rows_out = []
    for p in range(npos):
        pr_, pi_ = power(p + 1)
        mr, mi = _cmul(pr_, pi_, inr, ini)
        rows_out.append(jnp.concatenate([xr[p] + mr, xi[p] + mi], axis=1))
    xc = jnp.concatenate(rows_out, axis=0).astype(BF16)

    y = _dot(xc, cblk_ref[...]) + d_ref[...] * u
    y = jax.nn.gelu(y, approximate=True)
    y_ref[...] = y * _sigmoid(_dot(y.astype(BF16), gw_ref[...]) + gb_ref[...])

    @pl.when(t == pl.num_programs(1) - 1)
    def _():
        st_ref[...] = x_scr[...]


def _s5(u, bblk, cblk, apow, aend, dvec, gw, gb, bsz, tlen):
    tb = TIME_BLOCK
    nt = tlen // tb
    fixed2 = lambda b, t: (0, 0)
    return pl.pallas_call(
        _s5_kernel,
        grid=(bsz, nt),
        in_specs=[pl.BlockSpec((tb, S5_WIDTH), lambda b, t: (b * nt + t, 0)),
                  pl.BlockSpec((S5_WIDTH, 2 * S5_N), fixed2),
                  pl.BlockSpec((2 * S5_N, S5_WIDTH), fixed2),
                  pl.BlockSpec((tb, 2 * S5_N), fixed2),
                  pl.BlockSpec((SUBLANES, 2 * S5_N), fixed2),
                  pl.BlockSpec((1, S5_WIDTH), fixed2),
                  pl.BlockSpec((S5_WIDTH, S5_WIDTH), fixed2),
                  pl.BlockSpec((1, S5_WIDTH), fixed2)],
        out_specs=[pl.BlockSpec((tb, S5_WIDTH), lambda b, t: (b * nt + t, 0)),
                   pl.BlockSpec((None, 1, 2 * S5_N), lambda b, t: (b, 0, 0))],
        out_shape=[jax.ShapeDtypeStruct((bsz * tlen, S5_WIDTH), F32),
                   jax.ShapeDtypeStruct((bsz, 1, 2 * S5_N), F32)],
        scratch_shapes=[pltpu.VMEM((1, 2 * S5_N), F32)],
        compiler_params=pltpu.CompilerParams(
            dimension_semantics=("parallel", "arbitrary"), vmem_limit_bytes=VMEM_LIMIT),
        name="s5",
    )(u, bblk, cblk, apow, aend, dvec, gw, gb)


STEP_ROWS = 8


def _hgrn_step_kernel(hq_ref, hf_ref, hi_ref, hg_ref, lb_ref, nw_ref, s_ref, o_ref, sn_ref):
    lb = lb_ref[...]
    z = hf_ref[...]
    f = lb + (1.0 - lb) * _sigmoid(z)
    kk = (1.0 - lb) * _sigmoid(-z)
    q = _silu(hq_ref[...])
    sn = f * s_ref[...] + kk * hi_ref[...]
    sn_ref[...] = sn
    o = jnp.sum(q * sn, axis=2, keepdims=True)
    ms = jnp.mean(o * o, -1, keepdims=True)
    o_ref[...] = o * lax.rsqrt(ms + EPS) * nw_ref[...] * _silu(hg_ref[...])


def _hgrn_step(hq_c, hf_c, hi_r, hg_r, lb_c, nw_r, s0):
    n = s0.shape[0]
    r = STEP_ROWS
    col = pl.BlockSpec((r, HG_HEADS, HG_DK, 1), lambda i: (i, 0, 0, 0))
    rowv = pl.BlockSpec((r, HG_HEADS, 1, HG_DV), lambda i: (i, 0, 0, 0))
    st = pl.BlockSpec((r, HG_HEADS, HG_DK, HG_DV), lambda i: (i, 0, 0, 0))
    return pl.pallas_call(
        _hgrn_step_kernel,
        grid=(n // r,),
        in_specs=[col, col, rowv, rowv,
                  pl.BlockSpec((HG_HEADS, HG_DK, 1), lambda i: (0, 0, 0)),
                  pl.BlockSpec((1, 1, 1, HG_DV), lambda i: (0, 0, 0, 0)),
                  st],
        out_specs=[rowv, st],
        out_shape=[jax.ShapeDtypeStruct((n, HG_HEADS, 1, HG_DV), F32),
                   jax.ShapeDtypeStruct(s0.shape, F32)],
        compiler_params=pltpu.CompilerParams(
            dimension_semantics=("parallel",), vmem_limit_bytes=VMEM_LIMIT),
        name="hgrn_step",
    )(hq_c, hf_c, hi_r, hg_r, lb_c, nw_r, s0)


def _ret_step_kernel(rq_ref, rqs_ref, rk_ref, rks_ref, rv_ref, rg_ref, cos_ref, sin_ref,
                     gam_ref, nw_ref, s_ref, o_ref, sn_ref):
    cos = cos_ref[...]
    sin = sin_ref[...]
    q = rq_ref[...] * cos + rqs_ref[...] * sin
    k = (rk_ref[...] * cos + rks_ref[...] * sin) * (RET_DK ** -0.5)
    sn = gam_ref[...] * s_ref[...] + k * rv_ref[...]
    sn_ref[...] = sn
    o = jnp.sum(q * sn, axis=2, keepdims=True)
    mu = jnp.mean(o, -1, keepdims=True)
    oc = o - mu
    var = jnp.mean(oc * oc, -1, keepdims=True)
    o_ref[...] = oc * lax.rsqrt(var + EPS) * nw_ref[...] * _silu(rg_ref[...])


def _ret_step(rq_c, rqs_c, rk_c, rks_c, rv_r, rg_r, cos_c, sin_c, gam, nw_r, s0):
    n = s0.shape[0]
    r = STEP_ROWS
    col = pl.BlockSpec((r, RET_HEADS, RET_DK, 1), lambda i: (i, 0, 0, 0))
    rowv = pl.BlockSpec((r, RET_HEADS, 1, RET_DV), lambda i: (i, 0, 0, 0))
    st = pl.BlockSpec((r, RET_HEADS, RET_DK, RET_DV), lambda i: (i, 0, 0, 0))
    par = pl.BlockSpec((RET_HEADS, RET_DK, 1), lambda i: (0, 0, 0))
    return pl.pallas_call(
        _ret_step_kernel,
        grid=(n // r,),
        in_specs=[col, col, col, col, rowv, rowv, par, par,
                  pl.BlockSpec((RET_HEADS, 1, 1), lambda i: (0, 0, 0)),
                  pl.BlockSpec((RET_HEADS, 1, RET_DV), lambda i: (0, 0, 0)),
                  st],
        out_specs=[rowv, st],
        out_shape=[jax.ShapeDtypeStruct((n, RET_HEADS, 1, RET_DV), F32),
                   jax.ShapeDtypeStruct(s0.shape, F32)],
        compiler_params=pltpu.CompilerParams(
            dimension_semantics=("parallel",), vmem_limit_bytes=VMEM_LIMIT),
        name="ret_step",
    )(rq_c, rqs_c, rk_c, rks_c, rv_r, rg_r, cos_c, sin_c, gam, nw_r, s0)


def _s5_step_kernel(u_ref, xr_ref, xi_ref, bblk_ref, cblk_ref, ab_ref, d_ref, gw_ref, gb_ref,
                    y_ref, xrn_ref, xin_ref):
    n = S5_N
    u = u_ref[...]
    bu = _dot_hi(u, bblk_ref[...])
    mr, mi = _cmul(ab_ref[:, 0:n], ab_ref[:, n:2 * n], xr_ref[...], xi_ref[...])
    xr = mr + bu[:, 0:n]
    xi = mi + bu[:, n:2 * n]
    xrn_ref[...] = xr
    xin_ref[...] = xi
    y = _dot_hi(jnp.concatenate([xr, xi], axis=1), cblk_ref[...]) + d_ref[...] * u
    y = jax.nn.gelu(y, approximate=True)
    y_ref[...] = y * _sigmoid(_dot(y.astype(BF16), gw_ref[...]) + gb_ref[...])


def _s5_step(u, x0r, x0i, bblk, cblk, ab, dvec, gw, gb):
    n = u.shape[0]
    return pl.pallas_call(
        _s5_step_kernel,
        out_shape=[jax.ShapeDtypeStruct((n, S5_WIDTH), F32),
                   jax.ShapeDtypeStruct((n, S5_N), F32),
                   jax.ShapeDtypeStruct((n, S5_N), F32)],
        compiler_params=pltpu.CompilerParams(vmem_limit_bytes=VMEM_LIMIT),
        name="s5_step",
    )(u, x0r, x0i, bblk, cblk, ab, dvec, gw, gb)


def _tile_order():
    rows = np.arange(TIME_BLOCK)
    return BLOCK_LEN * (rows % SUBLANES) + rows // SUBLANES


def _permute_tokens(x, inverse=False):
    bsz, tlen, dm = x.shape
    a, b = (BLOCK_LEN, SUBLANES) if inverse else (SUBLANES, BLOCK_LEN)
    x = x.reshape(bsz, tlen // TIME_BLOCK, a, b, dm)
    return jnp.swapaxes(x, 2, 3).reshape(bsz, tlen, dm)


def _swap_halves_perm():
    half = RET_DK // 2
    idx = np.arange(RET_QK)
    j = idx % RET_DK
    return np.where(j < half, idx + half, idx - half)


def _pad_cols(w, width):
    return jnp.pad(w, ((0, 0), (0, width - w.shape[1])))


def _proj_weight(w_in_l):
    cuts = np.cumsum([HG_WIDTH] * 4 + [RET_QK] * 2 + [RET_WIDTH] * 2 + [S5_WIDTH])[:-1]
    hq, hf, hi, hg, rq, rk, rv, rg, su = jnp.split(w_in_l, [int(c) for c in cuts], axis=1)
    perm = _swap_halves_perm()
    pieces = [hq, hf, hi, hg,
              _pad_cols(rq, RET_QK_PAD), _pad_cols(rq[:, perm], RET_QK_PAD),
              _pad_cols(rk, RET_QK_PAD), _pad_cols(rk[:, perm], RET_QK_PAD),
              rv, rg, su]
    return jnp.concatenate(pieces, axis=1).astype(BF16)


def _rope_tables(pos):
    half = RET_DK // 2
    inv = ROPE_BASE ** (-jnp.arange(half, dtype=F32) / half)
    ang = pos[:, None] * inv[None, :]
    cos = jnp.cos(ang)
    sin = jnp.sin(ang)
    cos_h = jnp.concatenate([cos, cos], axis=1)
    sin_h = jnp.concatenate([-sin, sin], axis=1)
    return jnp.tile(cos_h, (1, RET_HEADS)), jnp.tile(sin_h, (1, RET_HEADS))


def _block_diag_mask(rows_per, cols_per, nblk):
    r = np.arange(rows_per * nblk)[:, None] // rows_per
    c = np.arange(cols_per * nblk)[None, :] // cols_per
    return (r == c).astype(np.float32)


def _s5_params(log_dt, a_re, a_im, b_re, b_im, c_re, c_im):
    dt = jnp.exp(log_dt)[:, None]
    mag = jnp.exp(dt * a_re)
    ab_re = mag * jnp.cos(dt * a_im)
    ab_im = mag * jnp.sin(dt * a_im)
    den = a_re * a_re + a_im * a_im
    nr = ab_re - 1.0
    g_re = (nr * a_re + ab_im * a_im) / den
    g_im = (ab_im * a_re - nr * a_im) / den
    bb_re = g_re[..., None] * b_re - g_im[..., None] * b_im
    bb_im = g_re[..., None] * b_im + g_im[..., None] * b_re
    eye = jnp.eye(S5_GROUPS, dtype=F32)

    def in_blk(bb):
        m = eye[:, None, :, None] * jnp.transpose(bb, (0, 2, 1))[:, :, None, :]
        return m.reshape(S5_WIDTH, S5_N)

    def out_blk(c):
        m = eye[:, None, :, None] * jnp.transpose(c, (0, 2, 1))[:, :, None, :]
        return m.reshape(S5_N, S5_WIDTH)

    bblk = jnp.concatenate([in_blk(bb_re), in_blk(bb_im)], axis=1)
    cblk = jnp.concatenate([out_blk(c_re), -out_blk(c_im)], axis=0)
    m = jnp.arange(1, TIME_BLOCK + 1, dtype=F32)[:, None, None]
    pmag = jnp.exp(m * (dt * a_re)[None])
    pang = m * (dt * a_im)[None]
    apow = jnp.concatenate([(pmag * jnp.cos(pang)).reshape(TIME_BLOCK, S5_N),
                            (pmag * jnp.sin(pang)).reshape(TIME_BLOCK, S5_N)], axis=1)
    ab = jnp.concatenate([ab_re.reshape(1, S5_N), ab_im.reshape(1, S5_N)], axis=1)
    return bblk, cblk, apow, ab


def kernel(x_prompt, x_sample, state_hgrn, state_ret, state_s5_re, state_s5_im, w_in, hgrn_lb_logits, hgrn_norm_w, ret_norm_w, s5_log_dt, s5_a_re, s5_a_im, s5_b_re, s5_b_im, s5_c_re, s5_c_im, s5_d, s5_glu_w, s5_glu_b, w_out, ln1_w, ln1_b, w_up, w_down, ln2_w, ln2_b):
    bp, tp, _ = x_prompt.shape
    bs = x_sample.shape[0]
    tb = TIME_BLOCK

    lb_prob = jax.nn.softmax(hgrn_lb_logits.astype(F32), axis=0)
    lower_bounds = jnp.cumsum(lb_prob, axis=0) - lb_prob[0:1]

    order = _tile_order()
    pos_p = (np.arange(tp) // tb * tb + np.tile(order, tp // tb)).astype(np.float32)
    cos_p, sin_p = _rope_tables(jnp.asarray(pos_p))
    cos_s, sin_s = _rope_tables(PAST_LEN + jnp.arange(1, dtype=F32))
    cos_pp = _pad_cols(cos_p, RET_QK_PAD)
    sin_pp = _pad_cols(sin_p, RET_QK_PAD)
    log_gamma = jnp.log1p(-jnp.exp2(-5.0 - jnp.arange(RET_HEADS, dtype=F32)))
    gamma = jnp.exp(log_gamma)
    tt = jnp.asarray(order.astype(np.float32))
    lg_lane = _pad_cols(jnp.repeat(log_gamma, RET_DK)[None, :], RET_QK_PAD)
    dq = jnp.exp((tt[:, None] + 1.0) * lg_lane)
    dk = jnp.exp((tb - 1.0 - tt[:, None]) * lg_lane)
    diff = tt[:, None] - tt[None, :]
    dmat = jnp.where(diff >= 0, jnp.exp(jnp.maximum(diff, 0.0)[None] * log_gamma[:, None, None]), 0.0)
    gam_rows = jnp.exp(tb * jnp.pad(jnp.repeat(log_gamma, RET_DK), (0, RET_QK_PAD - RET_QK)))
    gam_tbl = jnp.broadcast_to(gam_rows[:, None], (RET_QK_PAD, RET_WIDTH))
    ret_bd = jnp.asarray(np.pad(_block_diag_mask(RET_DK, RET_DV, RET_HEADS),
                                ((0, RET_QK_PAD - RET_QK), (0, 0))))
    hg_bd = jnp.asarray(_block_diag_mask(HG_DV, HG_DK, 2))
    ones256 = jnp.asarray(_block_diag_mask(64, 64, 4)).astype(BF16)
    ones128 = jnp.asarray(_block_diag_mask(64, 64, 2)).astype(BF16)

    xp = _permute_tokens(x_prompt).reshape(bp * tp, D_MODEL)
    xs = x_sample.reshape(bs, D_MODEL)
    outs = {k: [] for k in ("hg_p", "rt_p", "re_p", "im_p", "hg_s", "rt_s", "re_s", "im_s")}
    for l in range(DEPTH):
        wp = _proj_weight(w_in[l])
        wo = w_out[l].astype(BF16)
        wu = w_up[l].astype(BF16)
        wd = w_down[l].astype(BF16)
        lb = lower_bounds[l]
        hnw = jnp.tile(hgrn_norm_w[l], HG_HEADS)[None, :]
        rnw = ret_norm_w[l][None, :]
        bblk, cblk, apow, ab = _s5_params(s5_log_dt[l], s5_a_re[l], s5_a_im[l], s5_b_re[l],
                                          s5_b_im[l], s5_c_re[l], s5_c_im[l])
        dvec = s5_d[l][None, :]
        gw = s5_glu_w[l].astype(BF16)
        gb = s5_glu_b[l][None, :]
        ln = (ln1_w[l][None, :], ln1_b[l][None, :], wu, wd, ln2_w[l][None, :], ln2_b[l][None, :])

        hg_in, rt_in, s5_in = _proj(xp, wp, 512)
        o_hg, st_hg = _hgrn(hg_in, lb[None, :], hnw, ones256, ones128, hg_bd, bp, tp)
        o_rt, st_rt = _ret(rt_in, cos_pp, sin_pp, dq, dk, dmat, gam_tbl, ret_bd, rnw,
                           ones256, ones128, bp, tp)
        y5, st_s5 = _s5(s5_in, bblk.astype(BF16), cblk.astype(BF16), apow,
                        apow[BLOCK_LEN - 1::BLOCK_LEN], dvec, gw, gb, bp, tp)
        xp = _ffn(xp, o_hg, o_rt, y5, wo, *ln, 512)
        hg_state = jnp.stack(
            [st_hg[:, h // 2, 64 * (h % 2):64 * (h % 2 + 1), 64 * (h % 2):64 * (h % 2 + 1)]
             for h in range(HG_HEADS)], axis=1)
        outs["hg_p"].append(jnp.swapaxes(hg_state, -1, -2))
        outs["rt_p"].append(jnp.stack(
            [st_rt[:, RET_DK * h:RET_DK * (h + 1), RET_DV * h:RET_DV * (h + 1)]
             for h in range(RET_HEADS)], axis=1))
        outs["re_p"].append(st_s5[:, 0, :S5_N].reshape(bp, S5_GROUPS, S5_STATE))
        outs["im_p"].append(st_s5[:, 0, S5_N:].reshape(bp, S5_GROUPS, S5_STATE))

        hg_in, rt_in, s5_in = _proj(xs, wp, bs)

        def col(a, heads, dk_):
            return a.reshape(bs, heads, dk_, 1)

        def rowv(a, heads, dv_):
            return a.reshape(bs, heads, 1, dv_)

        o_hg, sn_hg = _hgrn_step(
            col(hg_in[:, 0:HG_WIDTH], HG_HEADS, HG_DK),
            col(hg_in[:, HG_WIDTH:2 * HG_WIDTH], HG_HEADS, HG_DK),
            rowv(hg_in[:, 2 * HG_WIDTH:3 * HG_WIDTH], HG_HEADS, HG_DV),
            rowv(hg_in[:, 3 * HG_WIDTH:4 * HG_WIDTH], HG_HEADS, HG_DV),
            lb.reshape(HG_HEADS, HG_DK, 1), hgrn_norm_w[l].reshape(1, 1, 1, HG_DV), state_hgrn[l])
        w = RET_QK_PAD
        o_rt, sn_rt = _ret_step(
            col(rt_in[:, 0:RET_QK], RET_HEADS, RET_DK),
            col(rt_in[:, w:w + RET_QK], RET_HEADS, RET_DK),
            col(rt_in[:, 2 * w:2 * w + RET_QK], RET_HEADS, RET_DK),
            col(rt_in[:, 3 * w:3 * w + RET_QK], RET_HEADS, RET_DK),
            rowv(rt_in[:, 4 * w:4 * w + RET_WIDTH], RET_HEADS, RET_DV),
            rowv(rt_in[:, 4 * w + RET_WIDTH:4 * w + 2 * RET_WIDTH], RET_HEADS, RET_DV),
            cos_s.reshape(RET_HEADS, RET_DK, 1), sin_s.reshape(RET_HEADS, RET_DK, 1),
            gamma.reshape(RET_HEADS, 1, 1), ret_norm_w[l].reshape(RET_HEADS, 1, RET_DV), state_ret[l])
        y5, xr_n, xi_n = _s5_step(s5_in, state_s5_re[l].reshape(bs, S5_N),
                                  state_s5_im[l].reshape(bs, S5_N), bblk, cblk, ab, dvec, gw, gb)
        xs = _ffn(xs, o_hg.reshape(bs, HG_WIDTH), o_rt.reshape(bs, RET_WIDTH), y5, wo, *ln, bs)
        outs["hg_s"].append(sn_hg)
        outs["rt_s"].append(sn_rt)
        outs["re_s"].append(xr_n.reshape(bs, S5_GROUPS, S5_STATE))
        outs["im_s"].append(xi_n.reshape(bs, S5_GROUPS, S5_STATE))

    yp = _permute_tokens(xp.reshape(bp, tp, D_MODEL), inverse=True)
    return (yp, xs.reshape(bs, 1, D_MODEL),
            jnp.stack(outs["hg_p"]), jnp.stack(outs["rt_p"]), jnp.stack(outs["re_p"]),
            jnp.stack(outs["im_p"]), jnp.stack(outs["hg_s"]), jnp.stack(outs["rt_s"]),
            jnp.stack(outs["re_s"]), jnp.stack(outs["im_s"]))
```

```python
import jax
import jax.numpy as jnp
import numpy as np
from jax import lax
from jax.experimental import pallas as pl
from jax.experimental.pallas import tpu as pltpu

F32 = jnp.float32
BF16 = jnp.bfloat16
HIGHEST = lax.Precision.HIGHEST

D_MODEL = 1024
DEPTH = 4
PAST_LEN = 16384
HG_WIDTH = 384
HG_HEADS = 6
HG_DK = 64
HG_DV = 64
RET_WIDTH = 384
RET_HEADS = 6
RET_DK = 32
RET_DV = 64
RET_QK = RET_HEADS * RET_DK
S5_WIDTH = 256
S5_GROUP = 16
S5_GROUPS = 16
S5_STATE = 64
S5_N = S5_GROUPS * S5_STATE
D_FF = 4 * D_MODEL
ROPE_BASE = 10000.0
EPS = 1e-5
ALPHA = (2.0 * DEPTH) ** 0.25

HG_COLS = 4 * HG_WIDTH
RT_COLS = 2 * RET_QK + 2 * RET_WIDTH
PROJ_COLS = HG_COLS + RT_COLS + S5_WIDTH
LANES = 128

SUBLANES = 8
TIME_BLOCK = 128
BLOCK_LEN = TIME_BLOCK // SUBLANES
N_PAIRS = BLOCK_LEN * (BLOCK_LEN + 1) // 2
FF_CHUNK = 1024
VMEM_LIMIT = 56 * 1024 * 1024


def _dot(a, b):
    return jnp.dot(a, b, preferred_element_type=F32)


def _dot_nt(a, b):
    return lax.dot_general(a, b, (((1,), (1,)), ((), ())), preferred_element_type=F32)


def _dot_tn(a, b):
    return lax.dot_general(a, b, (((0,), (0,)), ((), ())), preferred_element_type=F32)


def _dot_hi(a, b):
    return jnp.dot(a, b, preferred_element_type=F32, precision=HIGHEST)


def _sigmoid(x):
    return jax.nn.sigmoid(x)


def _silu(x):
    return x * jax.nn.sigmoid(x)


def _layer_norm(y, w, b):
    mu = jnp.mean(y, -1, keepdims=True)
    yc = y - mu
    var = jnp.mean(yc * yc, -1, keepdims=True)
    return yc * lax.rsqrt(var + EPS) * w + b


def _head_sum_bf16(x, ones256, ones128):
    return jnp.concatenate([_dot(x[:, 0:256], ones256), _dot(x[:, 256:384], ones128)], axis=1)


def _head_sum(x, ones256, ones128):
    hi = x.astype(BF16)
    lo = (x - hi.astype(F32)).astype(BF16)
    return _head_sum_bf16(hi, ones256, ones128) + _head_sum_bf16(lo, ones256, ones128)


def _proj_kernel(x_ref, w_ref, hg_ref, rt_ref, s5_ref):
    x = x_ref[...].astype(BF16)
    hg_ref[...] = _dot(x, w_ref[:, 0:HG_COLS])
    rt_ref[...] = _dot(x, w_ref[:, HG_COLS:HG_COLS + RT_COLS])
    s5_ref[...] = _dot(x, w_ref[:, HG_COLS + RT_COLS:PROJ_COLS])


def _proj(x, w, tm):
    m = x.shape[0]
    return pl.pallas_call(
        _proj_kernel,
        grid=(m // tm,),
        in_specs=[pl.BlockSpec((tm, D_MODEL), lambda i: (i, 0)),
                  pl.BlockSpec((D_MODEL, PROJ_COLS), lambda i: (0, 0),
                               pipeline_mode=pl.Buffered(1))],
        out_specs=[pl.BlockSpec((tm, HG_COLS), lambda i: (i, 0)),
                   pl.BlockSpec((tm, RT_COLS), lambda i: (i, 0)),
                   pl.BlockSpec((tm, S5_WIDTH), lambda i: (i, 0))],
        out_shape=[jax.ShapeDtypeStruct((m, HG_COLS), F32),
                   jax.ShapeDtypeStruct((m, RT_COLS), F32),
                   jax.ShapeDtypeStruct((m, S5_WIDTH), F32)],
        compiler_params=pltpu.CompilerParams(
            dimension_semantics=("parallel",), vmem_limit_bytes=VMEM_LIMIT),
        name="proj",
    )(x, w)


def _ffn_kernel(x_ref, a_ref, b_ref, c_ref, wo_ref, l1w_ref, l1b_ref, wu_ref, wd_ref,
                l2w_ref, l2b_ref, o_ref):
    x = x_ref[...]
    mixed = (_dot(a_ref[...].astype(BF16), wo_ref[0:HG_WIDTH, :])
             + _dot(b_ref[...].astype(BF16), wo_ref[HG_WIDTH:HG_WIDTH + RET_WIDTH, :])
             + _dot(c_ref[...].astype(BF16), wo_ref[HG_WIDTH + RET_WIDTH:D_MODEL, :]))
    x1 = _layer_norm(ALPHA * x + mixed, l1w_ref[...], l1b_ref[...])
    xb = x1.astype(BF16)
    ff = jnp.zeros_like(x1)
    for c in range(D_FF // FF_CHUNK):
        h = _dot(xb, wu_ref[:, c * FF_CHUNK:(c + 1) * FF_CHUNK])
        h = jnp.square(jnp.maximum(h, 0.0))
        ff = ff + _dot(h.astype(BF16), wd_ref[c * FF_CHUNK:(c + 1) * FF_CHUNK, :])
    o_ref[...] = _layer_norm(ALPHA * x1 + ff, l2w_ref[...], l2b_ref[...])


def _ffn(x, a, b, c, wo, l1w, l1b, wu, wd, l2w, l2b, tm):
    m = x.shape[0]
    row = lambda i: (i, 0)
    fixed = lambda i: (0, 0)
    once = pl.Buffered(1)
    return pl.pallas_call(
        _ffn_kernel,
        grid=(m // tm,),
        in_specs=[pl.BlockSpec((tm, D_MODEL), row),
                  pl.BlockSpec((tm, HG_WIDTH), row),
                  pl.BlockSpec((tm, RET_WIDTH), row),
                  pl.BlockSpec((tm, S5_WIDTH), row),
                  pl.BlockSpec((D_MODEL, D_MODEL), fixed, pipeline_mode=once),
                  pl.BlockSpec((1, D_MODEL), fixed),
                  pl.BlockSpec((1, D_MODEL), fixed),
                  pl.BlockSpec((D_MODEL, D_FF), fixed, pipeline_mode=once),
                  pl.BlockSpec((D_FF, D_MODEL), fixed, pipeline_mode=once),
                  pl.BlockSpec((1, D_MODEL), fixed),
                  pl.BlockSpec((1, D_MODEL), fixed)],
        out_specs=pl.BlockSpec((tm, D_MODEL), row),
        out_shape=jax.ShapeDtypeStruct((m, D_MODEL), F32),
        compiler_params=pltpu.CompilerParams(
            dimension_semantics=("parallel",), vmem_limit_bytes=VMEM_LIMIT),
        name="ffn",
    )(x, a, b, c, wo, l1w, l1b, wu, wd, l2w, l2b)


def _hgrn_kernel(x_ref, lb_ref, nw_ref, ones256_ref, ones128_ref, bd_ref, o_ref, st_ref,
                 s_scr, qt_scr, kt_scr, v_scr, oi_scr):
    npos = BLOCK_LEN
    sl = SUBLANES
    t = pl.program_id(1)

    @pl.when(t == 0)
    def _():
        s_scr[...] = jnp.zeros_like(s_scr)

    lb = lb_ref[...]
    ones256 = ones256_ref[...]
    ones128 = ones128_ref[...]
    f, kk, q, v = [], [], [], []
    for p in range(npos):
        rows = slice(sl * p, sl * (p + 1))
        hq = x_ref[rows, 0:HG_WIDTH]
        z = x_ref[rows, HG_WIDTH:2 * HG_WIDTH]
        f.append(lb + (1.0 - lb) * _sigmoid(z))
        kk.append((1.0 - lb) * _sigmoid(-z))
        q.append(_silu(hq))
        v.append(x_ref[rows, 2 * HG_WIDTH:3 * HG_WIDTH])

    g = list(kk)
    e_rows = []
    for d in range(npos):
        if d > 0:
            g = [None] * d + [f[p] * g[p - 1] for p in range(d, npos)]
        e_rows.extend(q[p] * g[p] for p in range(d, npos))
    e_all = jnp.concatenate(e_rows, axis=0).astype(BF16)
    p_all = _head_sum_bf16(e_all, ones256, ones128)
    o = [None] * npos
    i = 0
    for d in range(npos):
        for p in range(d, npos):
            term = p_all[sl * i:sl * (i + 1)] * v[p - d]
            o[p] = term if o[p] is None else o[p] + term
            i += 1

    a = [f[0]]
    for p in range(1, npos):
        a.append(a[-1] * f[p])
    r = [None] * npos
    r[npos - 1] = jnp.ones_like(f[0])
    for p in range(npos - 2, -1, -1):
        r[p] = r[p + 1] * f[p + 1]
    npair = HG_HEADS // 2
    for p in range(npos):
        rows = slice(sl * p, sl * (p + 1))
        qt = q[p] * a[p]
        kt = kk[p] * r[p]
        for pr in range(npair):
            lanes = slice(128 * pr, 128 * (pr + 1))
            qt_scr[pr, rows, :] = qt[:, lanes]
            kt_scr[pr, rows, :] = kt[:, lanes]
            v_scr[pr, rows, :] = v[p][:, lanes]
    cd = a[npos - 1]
    for j in range(sl):
        blk = pl.ds(j, npos, stride=sl)
        for pr in range(npair):
            lanes = slice(128 * pr, 128 * (pr + 1))
            st = s_scr[pr]
            oi_scr[pr, blk, :] = _dot_nt(qt_scr[pr, blk, :].astype(BF16), st.astype(BF16))
            u = _dot_tn(v_scr[pr, blk, :].astype(BF16), kt_scr[pr, blk, :].astype(BF16))
            s_scr[pr] = st * cd[j:j + 1, lanes] + u * bd_ref[...]

    ot = (jnp.concatenate(o, axis=0)
          + jnp.concatenate([oi_scr[pr] for pr in range(npair)], axis=1))
    ms = _head_sum(ot * ot, ones256, ones128) * (1.0 / HG_DV)
    gate = x_ref[:, 3 * HG_WIDTH:4 * HG_WIDTH]
    o_ref[...] = ot * lax.rsqrt(ms + EPS) * nw_ref[...] * _silu(gate)

    @pl.when(t == pl.num_programs(1) - 1)
    def _():
        st_ref[...] = s_scr[...]


def _hgrn(hg_in, lb, nw, ones256, ones128, bd, bsz, tlen):
    tb = TIME_BLOCK
    nt = tlen // tb
    fixed2 = lambda b, t: (0, 0)
    return pl.pallas_call(
        _hgrn_kernel,
        grid=(bsz, nt),
        in_specs=[pl.BlockSpec((tb, HG_COLS), lambda b, t: (b * nt + t, 0)),
                  pl.BlockSpec((1, HG_WIDTH), fixed2),
                  pl.BlockSpec((1, HG_WIDTH), fixed2),
                  pl.BlockSpec((256, 256), fixed2),
                  pl.BlockSpec((128, 128), fixed2),
                  pl.BlockSpec((128, 128), fixed2)],
        out_specs=[pl.BlockSpec((tb, HG_WIDTH), lambda b, t: (b * nt + t, 0)),
                   pl.BlockSpec((None, HG_HEADS // 2, 128, 128), lambda b, t: (b, 0, 0, 0))],
        out_shape=[jax.ShapeDtypeStruct((bsz * tlen, HG_WIDTH), F32),
                   jax.ShapeDtypeStruct((bsz, HG_HEADS // 2, 128, 128), F32)],
        scratch_shapes=[pltpu.VMEM((HG_HEADS // 2, 128, 128), F32),
                        pltpu.VMEM((HG_HEADS // 2, tb, 128), F32),
                        pltpu.VMEM((HG_HEADS // 2, tb, 128), F32),
                        pltpu.VMEM((HG_HEADS // 2, tb, 128), F32),
                        pltpu.VMEM((HG_HEADS // 2, tb, 128), F32)],
        compiler_params=pltpu.CompilerParams(
            dimension_semantics=("parallel", "arbitrary"), vmem_limit_bytes=VMEM_LIMIT),
        name="hgrn",
    )(hg_in, lb, nw, ones256, ones128, bd)


def _ret_kernel(x_ref, cos_ref, sin_ref, dq_ref, dk_ref, dmat_ref, gam_ref, bd_ref, nw_ref,
                ones256_ref, ones128_ref, o_ref, st_ref, s_scr):
    tb = TIME_BLOCK
    t = pl.program_id(1)

    @pl.when(t == 0)
    def _():
        s_scr[...] = jnp.zeros_like(s_scr)

    w = RET_QK
    half = RET_DK // 2
    in_first_half = (lax.broadcasted_iota(jnp.int32, (tb, LANES), 1) & (RET_DK - 1)) < half
    rot = []
    for c in range(2 * w // LANES):
        lanes = slice(LANES * c, LANES * (c + 1))
        x = x_ref[:, lanes]
        partner = jnp.where(in_first_half, pltpu.roll(x, LANES - half, 1), pltpu.roll(x, half, 1))
        rot.append(x * cos_ref[:, lanes] + partner * sin_ref[:, lanes])
    rot = jnp.concatenate(rot, axis=1)
    q = rot[:, 0:w]
    k = rot[:, w:2 * w] * (RET_DK ** -0.5)
    v = x_ref[:, 2 * w:2 * w + RET_WIDTH]
    gate = x_ref[:, 2 * w + RET_WIDTH:2 * w + 2 * RET_WIDTH]
    vb = v.astype(BF16)

    lane = lax.broadcasted_iota(jnp.int32, (tb, w), 1)
    qs = jnp.concatenate(
        [jnp.where((lane >= RET_DK * h) & (lane < RET_DK * (h + 1)), q, 0.0)
         for h in range(RET_HEADS)], axis=0).astype(BF16)
    sc = _dot_nt(qs, k.astype(BF16))
    lane128 = lax.broadcasted_iota(jnp.int32, (tb, 128), 1)
    outs = []
    for p in range(RET_HEADS // 2):
        p0 = sc[(2 * p) * tb:(2 * p + 1) * tb] * dmat_ref[2 * p]
        p1 = sc[(2 * p + 1) * tb:(2 * p + 2) * tb] * dmat_ref[2 * p + 1]
        pc = jnp.concatenate([p0, p1], axis=1).astype(BF16)
        vp = v[:, 128 * p:128 * (p + 1)]
        vs = jnp.concatenate([jnp.where(lane128 < RET_DV, vp, 0.0),
                              jnp.where(lane128 >= RET_DV, vp, 0.0)], axis=0).astype(BF16)
        outs.append(_dot(pc, vs))
    o = jnp.concatenate(outs, axis=1)

    s = s_scr[...]
    o = o + _dot((q * dq_ref[...]).astype(BF16), s.astype(BF16))
    u = _dot_tn((k * dk_ref[...]).astype(BF16), vb)
    s_scr[...] = s * gam_ref[...] + u * bd_ref[...]

    ones256 = ones256_ref[...]
    ones128 = ones128_ref[...]
    mu = _head_sum(o, ones256, ones128) * (1.0 / RET_DV)
    oc = o - mu
    var = _head_sum(oc * oc, ones256, ones128) * (1.0 / RET_DV)
    o_ref[...] = oc * lax.rsqrt(var + EPS) * nw_ref[...] * _silu(gate)

    @pl.when(t == pl.num_programs(1) - 1)
    def _():
        st_ref[...] = s_scr[...]


def _ret(rt_in, cos, sin, dq, dk, dmat, gam, bd, nw, ones256, ones128, bsz, tlen):
    tb = TIME_BLOCK
    nt = tlen // tb
    fixed2 = lambda b, t: (0, 0)
    return pl.pallas_call(
        _ret_kernel,
        grid=(bsz, nt),
        in_specs=[pl.BlockSpec((tb, RT_COLS), lambda b, t: (b * nt + t, 0)),
                  pl.BlockSpec((tb, 2 * RET_QK), lambda b, t: (t, 0)),
                  pl.BlockSpec((tb, 2 * RET_QK), lambda b, t: (t, 0)),
                  pl.BlockSpec((tb, RET_QK), fixed2),
                  pl.BlockSpec((tb, RET_QK), fixed2),
                  pl.BlockSpec((RET_HEADS, tb, tb), lambda b, t: (0, 0, 0)),
                  pl.BlockSpec((RET_QK, RET_WIDTH), fixed2),
                  pl.BlockSpec((RET_QK, RET_WIDTH), fixed2),
                  pl.BlockSpec((1, RET_WIDTH), fixed2),
                  pl.BlockSpec((256, 256), fixed2),
                  pl.BlockSpec((128, 128), fixed2)],
        out_specs=[pl.BlockSpec((tb, RET_WIDTH), lambda b, t: (b * nt + t, 0)),
                   pl.BlockSpec((None, RET_QK, RET_WIDTH), lambda b, t: (b, 0, 0))],
        out_shape=[jax.ShapeDtypeStruct((bsz * tlen, RET_WIDTH), F32),
                   jax.ShapeDtypeStruct((bsz, RET_QK, RET_WIDTH), F32)],
        scratch_shapes=[pltpu.VMEM((RET_QK, RET_WIDTH), F32)],
        compiler_params=pltpu.CompilerParams(
            dimension_semantics=("parallel", "arbitrary"), vmem_limit_bytes=VMEM_LIMIT),
        name="ret",
    )(rt_in, cos, sin, dq, dk, dmat, gam, bd, nw, ones256, ones128)


def _cmul(ar, ai, xr, xi):
    return ar * xr - ai * xi, ar * xi + ai * xr


def _s5_kernel(u_ref, bblk_ref, cblk_ref, apow_ref, aend_ref, d_ref, gw_ref, gb_ref, y_ref, st_ref,
               x_scr):
    npos = BLOCK_LEN
    sl = SUBLANES
    n = S5_N
    t = pl.program_id(1)

    @pl.when(t == 0)
    def _():
        x_scr[...] = jnp.zeros_like(x_scr)

    def power(m):
        return apow_ref[m - 1:m, 0:n], apow_ref[m - 1:m, n:2 * n]

    u = u_ref[...]
    bu = _dot(u.astype(BF16), bblk_ref[...])
    a1r, a1i = power(1)
    xr = [bu[0:sl, 0:n]]
    xi = [bu[0:sl, n:2 * n]]
    for p in range(1, npos):
        rows = slice(sl * p, sl * (p + 1))
        mr, mi = _cmul(a1r, a1i, xr[-1], xi[-1])
        xr.append(bu[rows, 0:n] + mr)
        xi.append(bu[rows, n:2 * n] + mi)

    blk = lax.broadcasted_iota(jnp.int32, (sl, n), 0)
    cr, ci = xr[-1], xi[-1]
    s = 1
    while s < sl:
        pr_, pi_ = power(npos * s)
        sr = jnp.where(blk >= s, pltpu.roll(cr, s, 0), 0.0)
        si = jnp.where(blk >= s, pltpu.roll(ci, s, 0), 0.0)
        mr, mi = _cmul(pr_, pi_, sr, si)
        cr, ci = cr + mr, ci + mi
        s *= 2
    x0r = x_scr[:, 0:n]
    x0i = x_scr[:, n:2 * n]
    mr, mi = _cmul(aend_ref[:, 0:n], aend_ref[:, n:2 * n], x0r, x0i)
    cr, ci = cr + mr, ci + mi
    x_scr[:, 0:n] = cr[sl - 1:sl]
    x_scr[:, n:2 * n] = ci[sl - 1:sl]
    inr = jnp.where(blk >= 1, pltpu.roll(cr, 1, 0), x0r)
    ini = jnp.where(blk >= 1, pltpu.roll(ci, 1, 0), x0i)
    rows_out = []
    for p in range(npos):
        pr_, pi_ = power(p + 1)
        mr, mi = _cmul(pr_, pi_, inr, ini)
        rows_out.append(jnp.concatenate([xr[p] + mr, xi[p] + mi], axis=1))
    xc = jnp.concatenate(rows_out, axis=0).astype(BF16)

    y = _dot(xc, cblk_ref[...]) + d_ref[...] * u
    y = jax.nn.gelu(y, approximate=True)
    y_ref[...] = y * _sigmoid(_dot(y.astype(BF16), gw_ref[...]) + gb_ref[...])

    @pl.when(t == pl.num_programs(1) - 1)
    def _():
        st_ref[...] = x_scr[...]


def _s5(u, bblk, cblk, apow, aend, dvec, gw, gb, bsz, tlen):
    tb = TIME_BLOCK
    nt = tlen // tb
    fixed2 = lambda b, t: (0, 0)
    return pl.pallas_call(
        _s5_kernel,
        grid=(bsz, nt),
        in_specs=[pl.BlockSpec((tb, S5_WIDTH), lambda b, t: (b * nt + t, 0)),
                  pl.BlockSpec((S5_WIDTH, 2 * S5_N), fixed2),
                  pl.BlockSpec((2 * S5_N, S5_WIDTH), fixed2),
                  pl.BlockSpec((tb, 2 * S5_N), fixed2),
                  pl.BlockSpec((SUBLANES, 2 * S5_N), fixed2),
                  pl.BlockSpec((1, S5_WIDTH), fixed2),
                  pl.BlockSpec((S5_WIDTH, S5_WIDTH), fixed2),
                  pl.BlockSpec((1, S5_WIDTH), fixed2)],
        out_specs=[pl.BlockSpec((tb, S5_WIDTH), lambda b, t: (b * nt + t, 0)),
                   pl.BlockSpec((None, 1, 2 * S5_N), lambda b, t: (b, 0, 0))],
        out_shape=[jax.ShapeDtypeStruct((bsz * tlen, S5_WIDTH), F32),
                   jax.ShapeDtypeStruct((bsz, 1, 2 * S5_N), F32)],
        scratch_shapes=[pltpu.VMEM((1, 2 * S5_N), F32)],
        compiler_params=pltpu.CompilerParams(
            dimension_semantics=("parallel", "arbitrary"), vmem_limit_bytes=VMEM_LIMIT),
        name="s5",
    )(u, bblk, cblk, apow, aend, dvec, gw, gb)


STEP_ROWS = 8


def _hgrn_step_kernel(qt_ref, zt_ref, v_ref, g_ref, lb_ref, nw_ref, s_ref, acc_ref, o_ref, sn_ref,
                      o_scr):
    del acc_ref
    lb = lb_ref[...]
    z = zt_ref[...]
    f = lb + (1.0 - lb) * _sigmoid(z)
    kk = (1.0 - lb) * _sigmoid(-z)
    q = _silu(qt_ref[...])
    for b in range(STEP_ROWS):
        for h in range(HG_HEADS):
            rows = slice(HG_DK * h, HG_DK * (h + 1))
            sn = f[rows, b:b + 1] * s_ref[b, h] + kk[rows, b:b + 1] * v_ref[b, h:h + 1, :]
            sn_ref[b, h] = sn
            o_scr[b, h:h + 1, :] = jnp.sum(q[rows, b:b + 1] * sn, axis=0, keepdims=True)
    o = o_scr[...]
    ms = jnp.mean(o * o, -1, keepdims=True)
    o_ref[...] = o * lax.rsqrt(ms + EPS) * nw_ref[...] * _silu(g_ref[...])


def _step_specs(layer, heads, dk, dv):
    r = STEP_ROWS
    colt = pl.BlockSpec((None, heads * dk, r), lambda i: (i, 0, 0))
    rowv = pl.BlockSpec((r, heads, dv), lambda i: (i, 0, 0))
    st = pl.BlockSpec((None, r, heads, dk, dv), lambda i: (layer, i, 0, 0, 0))
    return colt, rowv, st


def _hgrn_step(layer, q_t, z_t, v3, g3, lb_c, nw_r, s_all, acc):
    n = s_all.shape[1]
    colt, rowv, st = _step_specs(layer, HG_HEADS, HG_DK, HG_DV)
    return pl.pallas_call(
        _hgrn_step_kernel,
        grid=(n // STEP_ROWS,),
        in_specs=[colt, colt, rowv, rowv,
                  pl.BlockSpec((HG_WIDTH, 1), lambda i: (0, 0)),
                  pl.BlockSpec((1, 1, HG_DV), lambda i: (0, 0, 0)),
                  st,
                  pl.BlockSpec(memory_space=pl.ANY)],
        out_specs=[rowv, st],
        out_shape=[jax.ShapeDtypeStruct((n, HG_HEADS, HG_DV), F32),
                   jax.ShapeDtypeStruct(s_all.shape, F32)],
        scratch_shapes=[pltpu.VMEM((STEP_ROWS, HG_HEADS, HG_DV), F32)],
        input_output_aliases={7: 1},
        compiler_params=pltpu.CompilerParams(
            dimension_semantics=("parallel",), vmem_limit_bytes=VMEM_LIMIT),
        name="hgrn_step",
    )(q_t, z_t, v3, g3, lb_c, nw_r, s_all, acc)


def _ret_step_kernel(qt_ref, kt_ref, v_ref, g_ref, cos_ref, sin_ref, gam_ref, nw_ref, s_ref,
                     acc_ref, o_ref, sn_ref, o_scr):
    del acc_ref
    half = RET_DK // 2

    def rotary(x):
        parts = []
        for h in range(RET_HEADS):
            parts.append(x[RET_DK * h + half:RET_DK * (h + 1)])
            parts.append(x[RET_DK * h:RET_DK * h + half])
        return x * cos_ref[...] + jnp.concatenate(parts, axis=0) * sin_ref[...]

    q = rotary(qt_ref[...])
    k = rotary(kt_ref[...]) * (RET_DK ** -0.5)
    for b in range(STEP_ROWS):
        for h in range(RET_HEADS):
            rows = slice(RET_DK * h, RET_DK * (h + 1))
            sn = gam_ref[h] * s_ref[b, h] + k[rows, b:b + 1] * v_ref[b, h:h + 1, :]
            sn_ref[b, h] = sn
            o_scr[b, h:h + 1, :] = jnp.sum(q[rows, b:b + 1] * sn, axis=0, keepdims=True)
    o = o_scr[...]
    mu = jnp.mean(o, -1, keepdims=True)
    oc = o - mu
    var = jnp.mean(oc * oc, -1, keepdims=True)
    o_ref[...] = oc * lax.rsqrt(var + EPS) * nw_ref[...] * _silu(g_ref[...])


def _ret_step(layer, q_t, k_t, v3, g3, cos_c, sin_c, gam, nw_r, s_all, acc):
    n = s_all.shape[1]
    colt, rowv, st = _step_specs(layer, RET_HEADS, RET_DK, RET_DV)
    par = pl.BlockSpec((RET_QK, 1), lambda i: (0, 0))
    return pl.pallas_call(
        _ret_step_kernel,
        grid=(n // STEP_ROWS,),
        in_specs=[colt, colt, rowv, rowv, par, par,
                  pl.BlockSpec((RET_HEADS, 1, 1), lambda i: (0, 0, 0)),
                  pl.BlockSpec((RET_HEADS, RET_DV), lambda i: (0, 0)),
                  st,
                  pl.BlockSpec(memory_space=pl.ANY)],
        out_specs=[rowv, st],
        out_shape=[jax.ShapeDtypeStruct((n, RET_HEADS, RET_DV), F32),
                   jax.ShapeDtypeStruct(s_all.shape, F32)],
        scratch_shapes=[pltpu.VMEM((STEP_ROWS, RET_HEADS, RET_DV), F32)],
        input_output_aliases={9: 1},
        compiler_params=pltpu.CompilerParams(
            dimension_semantics=("parallel",), vmem_limit_bytes=VMEM_LIMIT),
        name="ret_step",
    )(q_t, k_t, v3, g3, cos_c, sin_c, gam, nw_r, s_all, acc)


def _s5_step_kernel(u_ref, xr_ref, xi_ref, bblk_ref, cblk_ref, ab_ref, d_ref, gw_ref, gb_ref,
                    y_ref, xrn_ref, xin_ref):
    n = S5_N
    u = u_ref[...]
    bu = _dot_hi(u, bblk_ref[...])
    mr, mi = _cmul(ab_ref[:, 0:n], ab_ref[:, n:2 * n], xr_ref[...], xi_ref[...])
    xr = mr + bu[:, 0:n]
    xi = mi + bu[:, n:2 * n]
    xrn_ref[...] = xr
    xin_ref[...] = xi
    y = _dot_hi(jnp.concatenate([xr, xi], axis=1), cblk_ref[...]) + d_ref[...] * u
    y = jax.nn.gelu(y, approximate=True)
    y_ref[...] = y * _sigmoid(_dot(y.astype(BF16), gw_ref[...]) + gb_ref[...])


def _s5_step(u, x0r, x0i, bblk, cblk, ab, dvec, gw, gb):
    n = u.shape[0]
    return pl.pallas_call(
        _s5_step_kernel,
        out_shape=[jax.ShapeDtypeStruct((n, S5_WIDTH), F32),
                   jax.ShapeDtypeStruct((n, S5_N), F32),
                   jax.ShapeDtypeStruct((n, S5_N), F32)],
        compiler_params=pltpu.CompilerParams(vmem_limit_bytes=VMEM_LIMIT),
        name="s5_step",
    )(u, x0r, x0i, bblk, cblk, ab, dvec, gw, gb)


def _tile_order():
    rows = np.arange(TIME_BLOCK)
    return BLOCK_LEN * (rows % SUBLANES) + rows // SUBLANES


def _permute_tokens(x, inverse=False):
    bsz, tlen, dm = x.shape
    a, b = (BLOCK_LEN, SUBLANES) if inverse else (SUBLANES, BLOCK_LEN)
    x = x.reshape(bsz, tlen // TIME_BLOCK, a, b, dm)
    return jnp.swapaxes(x, 2, 3).reshape(bsz, tlen, dm)


def _rope_tables(pos):
    half = RET_DK // 2
    inv = ROPE_BASE ** (-jnp.arange(half, dtype=F32) / half)
    ang = pos[:, None] * inv[None, :]
    cos = jnp.cos(ang)
    sin = jnp.sin(ang)
    cos_h = jnp.concatenate([cos, cos], axis=1)
    sin_h = jnp.concatenate([-sin, sin], axis=1)
    return jnp.tile(cos_h, (1, RET_HEADS)), jnp.tile(sin_h, (1, RET_HEADS))


def _block_diag_mask(rows_per, cols_per, nblk):
    r = np.arange(rows_per * nblk)[:, None] // rows_per
    c = np.arange(cols_per * nblk)[None, :] // cols_per
    return (r == c).astype(np.float32)


def _s5_params(log_dt, a_re, a_im, b_re, b_im, c_re, c_im):
    dt = jnp.exp(log_dt)[:, None]
    mag = jnp.exp(dt * a_re)
    ab_re = mag * jnp.cos(dt * a_im)
    ab_im = mag * jnp.sin(dt * a_im)
    den = a_re * a_re + a_im * a_im
    nr = ab_re - 1.0
    g_re = (nr * a_re + ab_im * a_im) / den
    g_im = (ab_im * a_re - nr * a_im) / den
    bb_re = g_re[..., None] * b_re - g_im[..., None] * b_im
    bb_im = g_re[..., None] * b_im + g_im[..., None] * b_re
    eye = jnp.eye(S5_GROUPS, dtype=F32)

    def in_blk(bb):
        m = eye[:, None, :, None] * jnp.transpose(bb, (0, 2, 1))[:, :, None, :]
        return m.reshape(S5_WIDTH, S5_N)

    def out_blk(c):
        m = eye[:, None, :, None] * jnp.transpose(c, (0, 2, 1))[:, :, None, :]
        return m.reshape(S5_N, S5_WIDTH)

    bblk = jnp.concatenate([in_blk(bb_re), in_blk(bb_im)], axis=1)
    cblk = jnp.concatenate([out_blk(c_re), -out_blk(c_im)], axis=0)
    m = jnp.arange(1, TIME_BLOCK + 1, dtype=F32)[:, None, None]
    pmag = jnp.exp(m * (dt * a_re)[None])
    pang = m * (dt * a_im)[None]
    apow = jnp.concatenate([(pmag * jnp.cos(pang)).reshape(TIME_BLOCK, S5_N),
                            (pmag * jnp.sin(pang)).reshape(TIME_BLOCK, S5_N)], axis=1)
    ab = jnp.concatenate([ab_re.reshape(1, S5_N), ab_im.reshape(1, S5_N)], axis=1)
    return bblk, cblk, apow, ab


def kernel(x_prompt, x_sample, state_hgrn, state_ret, state_s5_re, state_s5_im, w_in, hgrn_lb_logits, hgrn_norm_w, ret_norm_w, s5_log_dt, s5_a_re, s5_a_im, s5_b_re, s5_b_im, s5_c_re, s5_c_im, s5_d, s5_glu_w, s5_glu_b, w_out, ln1_w, ln1_b, w_up, w_down, ln2_w, ln2_b):
    bp, tp, _ = x_prompt.shape
    bs = x_sample.shape[0]
    tb = TIME_BLOCK

    lb_prob = jax.nn.softmax(hgrn_lb_logits.astype(F32), axis=0)
    lower_bounds = jnp.cumsum(lb_prob, axis=0) - lb_prob[0:1]

    order = _tile_order()
    pos_p = (np.arange(tp) // tb * tb + np.tile(order, tp // tb)).astype(np.float32)
    cos_p, sin_p = _rope_tables(jnp.asarray(pos_p))
    cos_s, sin_s = _rope_tables(PAST_LEN + jnp.arange(1, dtype=F32))
    cos_qk = jnp.concatenate([cos_p, cos_p], axis=1)
    sin_qk = jnp.concatenate([sin_p, sin_p], axis=1)
    log_gamma = jnp.log1p(-jnp.exp2(-5.0 - jnp.arange(RET_HEADS, dtype=F32)))
    gamma = jnp.exp(log_gamma)
    tt = jnp.asarray(order.astype(np.float32))
    lg_lane = jnp.repeat(log_gamma, RET_DK)[None, :]
    dq = jnp.exp((tt[:, None] + 1.0) * lg_lane)
    dk = jnp.exp((tb - 1.0 - tt[:, None]) * lg_lane)
    diff = tt[:, None] - tt[None, :]
    dmat = jnp.where(diff >= 0, jnp.exp(jnp.maximum(diff, 0.0)[None] * log_gamma[:, None, None]), 0.0)
    gam_tbl = jnp.broadcast_to(jnp.exp(tb * jnp.repeat(log_gamma, RET_DK))[:, None],
                               (RET_QK, RET_WIDTH))
    ret_bd = jnp.asarray(_block_diag_mask(RET_DK, RET_DV, RET_HEADS))
    hg_bd = jnp.asarray(_block_diag_mask(HG_DV, HG_DK, 2))
    ones256 = jnp.asarray(_block_diag_mask(64, 64, 4)).astype(BF16)
    ones128 = jnp.asarray(_block_diag_mask(64, 64, 2)).astype(BF16)

    xp = _permute_tokens(x_prompt).reshape(bp * tp, D_MODEL)
    xs = x_sample.reshape(bs, D_MODEL)
    outs = {k: [] for k in ("hg_p", "rt_p", "re_p", "im_p", "re_s", "im_s")}
    new_hg_s = jnp.zeros(state_hgrn.shape, F32)
    new_rt_s = jnp.zeros(state_ret.shape, F32)

    def cols(a):
        return jnp.swapaxes(a.reshape(bs // STEP_ROWS, STEP_ROWS, a.shape[1]), 1, 2)

    for l in range(DEPTH):
        wp = w_in[l].astype(BF16)
        wo = w_out[l].astype(BF16)
        wu = w_up[l].astype(BF16)
        wd = w_down[l].astype(BF16)
        lb = lower_bounds[l]
        hnw = jnp.tile(hgrn_norm_w[l], HG_HEADS)[None, :]
        rnw = ret_norm_w[l][None, :]
        bblk, cblk, apow, ab = _s5_params(s5_log_dt[l], s5_a_re[l], s5_a_im[l], s5_b_re[l],
                                          s5_b_im[l], s5_c_re[l], s5_c_im[l])
        dvec = s5_d[l][None, :]
        gw = s5_glu_w[l].astype(BF16)
        gb = s5_glu_b[l][None, :]
        ln = (ln1_w[l][None, :], ln1_b[l][None, :], wu, wd, ln2_w[l][None, :], ln2_b[l][None, :])

        hg_in, rt_in, s5_in = _proj(xp, wp, 512)
        o_hg, st_hg = _hgrn(hg_in, lb[None, :], hnw, ones256, ones128, hg_bd, bp, tp)
        o_rt, st_rt = _ret(rt_in, cos_qk, sin_qk, dq, dk, dmat, gam_tbl, ret_bd, rnw,
                           ones256, ones128, bp, tp)
        y5, st_s5 = _s5(s5_in, bblk.astype(BF16), cblk.astype(BF16), apow,
                        apow[BLOCK_LEN - 1::BLOCK_LEN], dvec, gw, gb, bp, tp)
        xp = _ffn(xp, o_hg, o_rt, y5, wo, *ln, 512)
        hg_state = jnp.stack(
            [st_hg[:, h // 2, 64 * (h % 2):64 * (h % 2 + 1), 64 * (h % 2):64 * (h % 2 + 1)]
             for h in range(HG_HEADS)], axis=1)
        outs["hg_p"].append(jnp.swapaxes(hg_state, -1, -2))
        outs["rt_p"].append(jnp.stack(
            [st_rt[:, RET_DK * h:RET_DK * (h + 1), RET_DV * h:RET_DV * (h + 1)]
             for h in range(RET_HEADS)], axis=1))
        outs["re_p"].append(st_s5[:, 0, :S5_N].reshape(bp, S5_GROUPS, S5_STATE))
        outs["im_p"].append(st_s5[:, 0, S5_N:].reshape(bp, S5_GROUPS, S5_STATE))

        hg_in, rt_in, s5_in = _proj(xs, wp, bs)

        o_hg, new_hg_s = _hgrn_step(
            l, cols(hg_in[:, 0:HG_WIDTH]), cols(hg_in[:, HG_WIDTH:2 * HG_WIDTH]),
            hg_in[:, 2 * HG_WIDTH:3 * HG_WIDTH].reshape(bs, HG_HEADS, HG_DV),
            hg_in[:, 3 * HG_WIDTH:4 * HG_WIDTH].reshape(bs, HG_HEADS, HG_DV),
            lb[:, None], hgrn_norm_w[l].reshape(1, 1, HG_DV), state_hgrn, new_hg_s)
        w = RET_QK
        o_rt, new_rt_s = _ret_step(
            l, cols(rt_in[:, 0:w]), cols(rt_in[:, w:2 * w]),
            rt_in[:, 2 * w:2 * w + RET_WIDTH].reshape(bs, RET_HEADS, RET_DV),
            rt_in[:, 2 * w + RET_WIDTH:2 * w + 2 * RET_WIDTH].reshape(bs, RET_HEADS, RET_DV),
            cos_s.reshape(RET_QK, 1), sin_s.reshape(RET_QK, 1),
            gamma.reshape(RET_HEADS, 1, 1), ret_norm_w[l].reshape(RET_HEADS, RET_DV),
            state_ret, new_rt_s)
        y5, xr_n, xi_n = _s5_step(s5_in, state_s5_re[l].reshape(bs, S5_N),
                                  state_s5_im[l].reshape(bs, S5_N), bblk, cblk, ab, dvec, gw, gb)
        xs = _ffn(xs, o_hg.reshape(bs, HG_WIDTH), o_rt.reshape(bs, RET_WIDTH), y5, wo, *ln, bs)
        outs["re_s"].append(xr_n.reshape(bs, S5_GROUPS, S5_STATE))
        outs["im_s"].append(xi_n.reshape(bs, S5_GROUPS, S5_STATE))

    yp = _permute_tokens(xp.reshape(bp, tp, D_MODEL), inverse=True)
    return (yp, xs.reshape(bs, 1, D_MODEL),
            jnp.stack(outs["hg_p"]), jnp.stack(outs["rt_p"]), jnp.stack(outs["re_p"]),
            jnp.stack(outs["im_p"]), new_hg_s, new_rt_s,
            jnp.stack(outs["re_s"]), jnp.stack(outs["im_s"]))
```

```python
import jax
import jax.numpy as jnp
import numpy as np
from jax import lax
from jax.experimental import pallas as pl
from jax.experimental.pallas import tpu as pltpu

F32 = jnp.float32
BF16 = jnp.bfloat16
HIGHEST = lax.Precision.HIGHEST

D_MODEL = 1024
DEPTH = 4
PAST_LEN = 16384
HG_WIDTH = 384
HG_HEADS = 6
HG_DK = 64
HG_DV = 64
RET_WIDTH = 384
RET_HEADS = 6
RET_DK = 32
RET_DV = 64
RET_QK = RET_HEADS * RET_DK
S5_WIDTH = 256
S5_GROUP = 16
S5_GROUPS = 16
S5_STATE = 64
S5_N = S5_GROUPS * S5_STATE
D_FF = 4 * D_MODEL
ROPE_BASE = 10000.0
EPS = 1e-5
ALPHA = (2.0 * DEPTH) ** 0.25

HG_COLS = 4 * HG_WIDTH
RT_COLS = 2 * RET_QK + 2 * RET_WIDTH
PROJ_COLS = HG_COLS + RT_COLS + S5_WIDTH
LANES = 128

SUBLANES = 8
TIME_BLOCK = 128
BLOCK_LEN = TIME_BLOCK // SUBLANES
N_PAIRS = BLOCK_LEN * (BLOCK_LEN + 1) // 2
FF_CHUNK = 1024
VMEM_LIMIT = 56 * 1024 * 1024


def _dot(a, b):
    return jnp.dot(a, b, preferred_element_type=F32)


def _dot_nt(a, b):
    return lax.dot_general(a, b, (((1,), (1,)), ((), ())), preferred_element_type=F32)


def _dot_tn(a, b):
    return lax.dot_general(a, b, (((0,), (0,)), ((), ())), preferred_element_type=F32)


def _dot_hi(a, b):
    return jnp.dot(a, b, preferred_element_type=F32, precision=HIGHEST)


def _sigmoid(x):
    return jax.nn.sigmoid(x)


def _silu(x):
    return x * jax.nn.sigmoid(x)


def _layer_norm(y, w, b):
    mu = jnp.mean(y, -1, keepdims=True)
    yc = y - mu
    var = jnp.mean(yc * yc, -1, keepdims=True)
    return yc * lax.rsqrt(var + EPS) * w + b


def _head_sum_bf16(x, ones256, ones128):
    return jnp.concatenate([_dot(x[:, 0:256], ones256), _dot(x[:, 256:384], ones128)], axis=1)


def _head_sum(x, ones256, ones128):
    hi = x.astype(BF16)
    lo = (x - hi.astype(F32)).astype(BF16)
    return _head_sum_bf16(hi, ones256, ones128) + _head_sum_bf16(lo, ones256, ones128)


def _proj_kernel(x_ref, w_ref, hg_ref, rt_ref, s5_ref):
    x = x_ref[...].astype(BF16)
    hg_ref[...] = _dot(x, w_ref[:, 0:HG_COLS])
    rt_ref[...] = _dot(x, w_ref[:, HG_COLS:HG_COLS + RT_COLS])
    s5_ref[...] = _dot(x, w_ref[:, HG_COLS + RT_COLS:PROJ_COLS])


def _layer_spec(layer, shape, grid_rank, **kw):
    zeros = (0,) * len(shape)
    if grid_rank == 1:
        return pl.BlockSpec((None,) + shape, lambda i: (layer,) + zeros, **kw)
    return pl.BlockSpec((None,) + shape, lambda b, t: (layer,) + zeros, **kw)


def _proj(x, w, layer, tm):
    m = x.shape[0]
    return pl.pallas_call(
        _proj_kernel,
        grid=(m // tm,),
        in_specs=[pl.BlockSpec((tm, D_MODEL), lambda i: (i, 0)),
                  _layer_spec(layer, (D_MODEL, PROJ_COLS), 1, pipeline_mode=pl.Buffered(1))],
        out_specs=[pl.BlockSpec((tm, HG_COLS), lambda i: (i, 0)),
                   pl.BlockSpec((tm, RT_COLS), lambda i: (i, 0)),
                   pl.BlockSpec((tm, S5_WIDTH), lambda i: (i, 0))],
        out_shape=[jax.ShapeDtypeStruct((m, HG_COLS), F32),
                   jax.ShapeDtypeStruct((m, RT_COLS), F32),
                   jax.ShapeDtypeStruct((m, S5_WIDTH), F32)],
        compiler_params=pltpu.CompilerParams(
            dimension_semantics=("parallel",), vmem_limit_bytes=VMEM_LIMIT),
        name="proj",
    )(x, w)


def _ffn_kernel(x_ref, a_ref, b_ref, c_ref, wo_ref, l1w_ref, l1b_ref, wu_ref, wd_ref,
                l2w_ref, l2b_ref, o_ref):
    x = x_ref[...]
    mixed = (_dot(a_ref[...].astype(BF16), wo_ref[0:HG_WIDTH, :])
             + _dot(b_ref[...].astype(BF16), wo_ref[HG_WIDTH:HG_WIDTH + RET_WIDTH, :])
             + _dot(c_ref[...].astype(BF16), wo_ref[HG_WIDTH + RET_WIDTH:D_MODEL, :]))
    x1 = _layer_norm(ALPHA * x + mixed, l1w_ref[...], l1b_ref[...])
    xb = x1.astype(BF16)
    ff = jnp.zeros_like(x1)
    for c in range(D_FF // FF_CHUNK):
        h = _dot(xb, wu_ref[:, c * FF_CHUNK:(c + 1) * FF_CHUNK])
        h = jnp.square(jnp.maximum(h, 0.0))
        ff = ff + _dot(h.astype(BF16), wd_ref[c * FF_CHUNK:(c + 1) * FF_CHUNK, :])
    o_ref[...] = _layer_norm(ALPHA * x1 + ff, l2w_ref[...], l2b_ref[...])


def _ffn(x, a, b, c, wo, l1w, l1b, wu, wd, l2w, l2b, layer, tm):
    m = x.shape[0]
    row = lambda i: (i, 0)
    once = pl.Buffered(1)
    vec = _layer_spec(layer, (1, D_MODEL), 1)
    return pl.pallas_call(
        _ffn_kernel,
        grid=(m // tm,),
        in_specs=[pl.BlockSpec((tm, D_MODEL), row),
                  pl.BlockSpec((tm, HG_WIDTH), row),
                  pl.BlockSpec((tm, RET_WIDTH), row),
                  pl.BlockSpec((tm, S5_WIDTH), row),
                  _layer_spec(layer, (D_MODEL, D_MODEL), 1, pipeline_mode=once),
                  vec, vec,
                  _layer_spec(layer, (D_MODEL, D_FF), 1, pipeline_mode=once),
                  _layer_spec(layer, (D_FF, D_MODEL), 1, pipeline_mode=once),
                  vec, vec],
        out_specs=pl.BlockSpec((tm, D_MODEL), row),
        out_shape=jax.ShapeDtypeStruct((m, D_MODEL), F32),
        compiler_params=pltpu.CompilerParams(
            dimension_semantics=("parallel",), vmem_limit_bytes=VMEM_LIMIT),
        name="ffn",
    )(x, a, b, c, wo, l1w, l1b, wu, wd, l2w, l2b)


def _hgrn_kernel(x_ref, lb_ref, nw_ref, ones256_ref, ones128_ref, bd_ref, o_ref, st_ref,
                 s_scr, qt_scr, kt_scr, v_scr, oi_scr):
    npos = BLOCK_LEN
    sl = SUBLANES
    t = pl.program_id(1)

    @pl.when(t == 0)
    def _():
        s_scr[...] = jnp.zeros_like(s_scr)

    lb = lb_ref[...]
    ones256 = ones256_ref[...]
    ones128 = ones128_ref[...]
    f, kk, q, v = [], [], [], []
    for p in range(npos):
        rows = slice(sl * p, sl * (p + 1))
        hq = x_ref[rows, 0:HG_WIDTH]
        z = x_ref[rows, HG_WIDTH:2 * HG_WIDTH]
        f.append(lb + (1.0 - lb) * _sigmoid(z))
        kk.append((1.0 - lb) * _sigmoid(-z))
        q.append(_silu(hq))
        v.append(x_ref[rows, 2 * HG_WIDTH:3 * HG_WIDTH])

    g = list(kk)
    e_rows = []
    for d in range(npos):
        if d > 0:
            g = [None] * d + [f[p] * g[p - 1] for p in range(d, npos)]
        e_rows.extend(q[p] * g[p] for p in range(d, npos))
    e_all = jnp.concatenate(e_rows, axis=0).astype(BF16)
    p_all = _head_sum_bf16(e_all, ones256, ones128)
    o = [None] * npos
    i = 0
    for d in range(npos):
        for p in range(d, npos):
            term = p_all[sl * i:sl * (i + 1)] * v[p - d]
            o[p] = term if o[p] is None else o[p] + term
            i += 1

    a = [f[0]]
    for p in range(1, npos):
        a.append(a[-1] * f[p])
    r = [None] * npos
    r[npos - 1] = jnp.ones_like(f[0])
    for p in range(npos - 2, -1, -1):
        r[p] = r[p + 1] * f[p + 1]
    npair = HG_HEADS // 2
    for p in range(npos):
        rows = slice(sl * p, sl * (p + 1))
        qt = q[p] * a[p]
        kt = kk[p] * r[p]
        for pr in range(npair):
            lanes = slice(128 * pr, 128 * (pr + 1))
            qt_scr[pr, rows, :] = qt[:, lanes]
            kt_scr[pr, rows, :] = kt[:, lanes]
            v_scr[pr, rows, :] = v[p][:, lanes]
    cd = a[npos - 1]
    for j in range(sl):
        blk = pl.ds(j, npos, stride=sl)
        for pr in range(npair):
            lanes = slice(128 * pr, 128 * (pr + 1))
            st = s_scr[pr]
            oi_scr[pr, blk, :] = _dot_nt(qt_scr[pr, blk, :].astype(BF16), st.astype(BF16))
            u = _dot_tn(v_scr[pr, blk, :].astype(BF16), kt_scr[pr, blk, :].astype(BF16))
            s_scr[pr] = st * cd[j:j + 1, lanes] + u * bd_ref[...]

    ot = (jnp.concatenate(o, axis=0)
          + jnp.concatenate([oi_scr[pr] for pr in range(npair)], axis=1))
    ms = _head_sum(ot * ot, ones256, ones128) * (1.0 / HG_DV)
    gate = x_ref[:, 3 * HG_WIDTH:4 * HG_WIDTH]
    o_ref[...] = ot * lax.rsqrt(ms + EPS) * nw_ref[...] * _silu(gate)

    @pl.when(t == pl.num_programs(1) - 1)
    def _():
        st_ref[...] = s_scr[...]


def _hgrn(hg_in, lb, nw, ones256, ones128, bd, layer, bsz, tlen):
    tb = TIME_BLOCK
    nt = tlen // tb
    fixed2 = lambda b, t: (0, 0)
    return pl.pallas_call(
        _hgrn_kernel,
        grid=(bsz, nt),
        in_specs=[pl.BlockSpec((tb, HG_COLS), lambda b, t: (b * nt + t, 0)),
                  _layer_spec(layer, (1, HG_WIDTH), 2),
                  _layer_spec(layer, (1, HG_WIDTH), 2),
                  pl.BlockSpec((256, 256), fixed2),
                  pl.BlockSpec((128, 128), fixed2),
                  pl.BlockSpec((128, 128), fixed2)],
        out_specs=[pl.BlockSpec((tb, HG_WIDTH), lambda b, t: (b * nt + t, 0)),
                   pl.BlockSpec((None, HG_HEADS // 2, 128, 128), lambda b, t: (b, 0, 0, 0))],
        out_shape=[jax.ShapeDtypeStruct((bsz * tlen, HG_WIDTH), F32),
                   jax.ShapeDtypeStruct((bsz, HG_HEADS // 2, 128, 128), F32)],
        scratch_shapes=[pltpu.VMEM((HG_HEADS // 2, 128, 128), F32),
                        pltpu.VMEM((HG_HEADS // 2, tb, 128), F32),
                        pltpu.VMEM((HG_HEADS // 2, tb, 128), F32),
                        pltpu.VMEM((HG_HEADS // 2, tb, 128), F32),
                        pltpu.VMEM((HG_HEADS // 2, tb, 128), F32)],
        compiler_params=pltpu.CompilerParams(
            dimension_semantics=("parallel", "arbitrary"), vmem_limit_bytes=VMEM_LIMIT),
        name="hgrn",
    )(hg_in, lb, nw, ones256, ones128, bd)


def _ret_kernel(x_ref, cos_ref, sin_ref, dq_ref, dk_ref, dmat_ref, gam_ref, bd_ref, nw_ref,
                ones256_ref, ones128_ref, o_ref, st_ref, s_scr):
    tb = TIME_BLOCK
    t = pl.program_id(1)

    @pl.when(t == 0)
    def _():
        s_scr[...] = jnp.zeros_like(s_scr)

    w = RET_QK
    half = RET_DK // 2
    in_first_half = (lax.broadcasted_iota(jnp.int32, (tb, LANES), 1) & (RET_DK - 1)) < half
    rot = []
    for c in range(2 * w // LANES):
        lanes = slice(LANES * c, LANES * (c + 1))
        x = x_ref[:, lanes]
        partner = jnp.where(in_first_half, pltpu.roll(x, LANES - half, 1), pltpu.roll(x, half, 1))
        rot.append(x * cos_ref[:, lanes] + partner * sin_ref[:, lanes])
    rot = jnp.concatenate(rot, axis=1)
    q = rot[:, 0:w]
    k = rot[:, w:2 * w] * (RET_DK ** -0.5)
    v = x_ref[:, 2 * w:2 * w + RET_WIDTH]
    gate = x_ref[:, 2 * w + RET_WIDTH:2 * w + 2 * RET_WIDTH]
    vb = v.astype(BF16)

    lane = lax.broadcasted_iota(jnp.int32, (tb, w), 1)
    qs = jnp.concatenate(
        [jnp.where((lane >= RET_DK * h) & (lane < RET_DK * (h + 1)), q, 0.0)
         for h in range(RET_HEADS)], axis=0).astype(BF16)
    sc = _dot_nt(qs, k.astype(BF16))
    lane128 = lax.broadcasted_iota(jnp.int32, (tb, 128), 1)
    outs = []
    for p in range(RET_HEADS // 2):
        p0 = sc[(2 * p) * tb:(2 * p + 1) * tb] * dmat_ref[2 * p]
        p1 = sc[(2 * p + 1) * tb:(2 * p + 2) * tb] * dmat_ref[2 * p + 1]
        pc = jnp.concatenate([p0, p1], axis=1).astype(BF16)
        vp = v[:, 128 * p:128 * (p + 1)]
        vs = jnp.concatenate([jnp.where(lane128 < RET_DV, vp, 0.0),
                              jnp.where(lane128 >= RET_DV, vp, 0.0)], axis=0).astype(BF16)
        outs.append(_dot(pc, vs))
    o = jnp.concatenate(outs, axis=1)

    s = s_scr[...]
    o = o + _dot((q * dq_ref[...]).astype(BF16), s.astype(BF16))
    u = _dot_tn((k * dk_ref[...]).astype(BF16), vb)
    s_scr[...] = s * gam_ref[...] + u * bd_ref[...]

    ones256 = ones256_ref[...]
    ones128 = ones128_ref[...]
    mu = _head_sum(o, ones256, ones128) * (1.0 / RET_DV)
    oc = o - mu
    var = _head_sum(oc * oc, ones256, ones128) * (1.0 / RET_DV)
    o_ref[...] = oc * lax.rsqrt(var + EPS) * nw_ref[...] * _silu(gate)

    @pl.when(t == pl.num_programs(1) - 1)
    def _():
        st_ref[...] = s_scr[...]


def _ret(rt_in, cos, sin, dq, dk, dmat, gam, bd, nw, ones256, ones128, layer, bsz, tlen):
    tb = TIME_BLOCK
    nt = tlen // tb
    fixed2 = lambda b, t: (0, 0)
    return pl.pallas_call(
        _ret_kernel,
        grid=(bsz, nt),
        in_specs=[pl.BlockSpec((tb, RT_COLS), lambda b, t: (b * nt + t, 0)),
                  pl.BlockSpec((tb, 2 * RET_QK), lambda b, t: (t, 0)),
                  pl.BlockSpec((tb, 2 * RET_QK), lambda b, t: (t, 0)),
                  pl.BlockSpec((tb, RET_QK), fixed2),
                  pl.BlockSpec((tb, RET_QK), fixed2),
                  pl.BlockSpec((RET_HEADS, tb, tb), lambda b, t: (0, 0, 0)),
                  pl.BlockSpec((RET_QK, RET_WIDTH), fixed2),
                  pl.BlockSpec((RET_QK, RET_WIDTH), fixed2),
                  _layer_spec(layer, (1, RET_WIDTH), 2),
                  pl.BlockSpec((256, 256), fixed2),
                  pl.BlockSpec((128, 128), fixed2)],
        out_specs=[pl.BlockSpec((tb, RET_WIDTH), lambda b, t: (b * nt + t, 0)),
                   pl.BlockSpec((None, RET_QK, RET_WIDTH), lambda b, t: (b, 0, 0))],
        out_shape=[jax.ShapeDtypeStruct((bsz * tlen, RET_WIDTH), F32),
                   jax.ShapeDtypeStruct((bsz, RET_QK, RET_WIDTH), F32)],
        scratch_shapes=[pltpu.VMEM((RET_QK, RET_WIDTH), F32)],
        compiler_params=pltpu.CompilerParams(
            dimension_semantics=("parallel", "arbitrary"), vmem_limit_bytes=VMEM_LIMIT),
        name="ret",
    )(rt_in, cos, sin, dq, dk, dmat, gam, bd, nw, ones256, ones128)


def _cmul(ar, ai, xr, xi):
    return ar * xr - ai * xi, ar * xi + ai * xr


def _s5_kernel(u_ref, bblk_ref, cblk_ref, apow_ref, aend_ref, d_ref, gw_ref, gb_ref, y_ref, st_ref,
               x_scr):
    npos = BLOCK_LEN
    sl = SUBLANES
    n = S5_N
    t = pl.program_id(1)

    @pl.when(t == 0)
    def _():
        x_scr[...] = jnp.zeros_like(x_scr)

    def power(m):
        return apow_ref[m - 1:m, 0:n], apow_ref[m - 1:m, n:2 * n]

    u = u_ref[...]
    bu = _dot(u.astype(BF16), bblk_ref[...])
    a1r, a1i = power(1)
    xr = [bu[0:sl, 0:n]]
    xi = [bu[0:sl, n:2 * n]]
    for p in range(1, npos):
        rows = slice(sl * p, sl * (p + 1))
        mr, mi = _cmul(a1r, a1i, xr[-1], xi[-1])
        xr.append(bu[rows, 0:n] + mr)
        xi.append(bu[rows, n:2 * n] + mi)

    blk = lax.broadcasted_iota(jnp.int32, (sl, n), 0)
    cr, ci = xr[-1], xi[-1]
    s = 1
    while s < sl:
        pr_, pi_ = power(npos * s)
        sr = jnp.where(blk >= s, pltpu.roll(cr, s, 0), 0.0)
        si = jnp.where(blk >= s, pltpu.roll(ci, s, 0), 0.0)
        mr, mi = _cmul(pr_, pi_, sr, si)
        cr, ci = cr + mr, ci + mi
        s *= 2
    x0r = x_scr[:, 0:n]
    x0i = x_scr[:, n:2 * n]
    mr, mi = _cmul(aend_ref[:, 0:n], aend_ref[:, n:2 * n], x0r, x0i)
    cr, ci = cr + mr, ci + mi
    x_scr[:, 0:n] = cr[sl - 1:sl]
    x_scr[:, n:2 * n] = ci[sl - 1:sl]
    inr = jnp.where(blk >= 1, pltpu.roll(cr, 1, 0), x0r)
    ini = jnp.where(blk >= 1, pltpu.roll(ci, 1, 0), x0i)
    rows_out = []
    for p in range(npos):
        pr_, pi_ = power(p + 1)
        mr, mi = _cmul(pr_, pi_, inr, ini)
        rows_out.append(jnp.concatenate([xr[p] + mr, xi[p] + mi], axis=1))
    xc = jnp.concatenate(rows_out, axis=0).astype(BF16)

    y = _dot(xc, cblk_ref[...]) + d_ref[...] * u
    y = jax.nn.gelu(y, approximate=True)
    y_ref[...] = y * _sigmoid(_dot(y.astype(BF16), gw_ref[...]) + gb_ref[...])

    @pl.when(t == pl.num_programs(1) - 1)
    def _():
        st_ref[...] = x_scr[...]


def _s5(u, bblk, cblk, apow, aend, dvec, gw, gb, layer, bsz, tlen):
    tb = TIME_BLOCK
    nt = tlen // tb
    return pl.pallas_call(
        _s5_kernel,
        grid=(bsz, nt),
        in_specs=[pl.BlockSpec((tb, S5_WIDTH), lambda b, t: (b * nt + t, 0)),
                  _layer_spec(layer, (S5_WIDTH, 2 * S5_N), 2),
                  _layer_spec(layer, (2 * S5_N, S5_WIDTH), 2),
                  _layer_spec(layer, (tb, 2 * S5_N), 2),
                  _layer_spec(layer, (SUBLANES, 2 * S5_N), 2),
                  _layer_spec(layer, (1, S5_WIDTH), 2),
                  _layer_spec(layer, (S5_WIDTH, S5_WIDTH), 2),
                  _layer_spec(layer, (1, S5_WIDTH), 2)],
        out_specs=[pl.BlockSpec((tb, S5_WIDTH), lambda b, t: (b * nt + t, 0)),
                   pl.BlockSpec((None, 1, 2 * S5_N), lambda b, t: (b, 0, 0))],
        out_shape=[jax.ShapeDtypeStruct((bsz * tlen, S5_WIDTH), F32),
                   jax.ShapeDtypeStruct((bsz, 1, 2 * S5_N), F32)],
        scratch_shapes=[pltpu.VMEM((1, 2 * S5_N), F32)],
        compiler_params=pltpu.CompilerParams(
            dimension_semantics=("parallel", "arbitrary"), vmem_limit_bytes=VMEM_LIMIT),
        name="s5",
    )(u, bblk, cblk, apow, aend, dvec, gw, gb)


STEP_ROWS = 8


def _hgrn_step_kernel(x_ref, lb_ref, nw_ref, s_ref, acc_ref, o_ref, sn_ref,
                      f_scr, k_scr, q_scr, v_scr, g_scr):
    del acc_ref
    h = pl.program_id(0)

    @pl.when(h == 0)
    def _():
        lb = lb_ref[...]
        z = x_ref[:, HG_WIDTH:2 * HG_WIDTH].T
        f_scr[...] = lb + (1.0 - lb) * _sigmoid(z)
        k_scr[...] = (1.0 - lb) * _sigmoid(-z)
        q_scr[...] = _silu(x_ref[:, 0:HG_WIDTH].T)
        v_scr[...] = x_ref[:, 2 * HG_WIDTH:3 * HG_WIDTH].T
        g_scr[...] = x_ref[:, 3 * HG_WIDTH:4 * HG_WIDTH].T

    rows = pl.ds(pl.multiple_of(h * HG_DK, HG_DK), HG_DK)
    f = f_scr[rows, :]
    kk = k_scr[rows, :]
    q = q_scr[rows, :]
    v = v_scr[rows, :]
    o = jnp.zeros_like(v)
    for k in range(HG_DK):
        sn = f[k:k + 1] * s_ref[k] + kk[k:k + 1] * v
        sn_ref[k] = sn
        o = o + q[k:k + 1] * sn
    ms = jnp.mean(o * o, axis=0, keepdims=True)
    o_ref[...] = o * lax.rsqrt(ms + EPS) * nw_ref[...] * _silu(g_scr[rows, :])


def _hgrn_step(layer, hg_in, lb_c, nw_c, s_all, acc):
    n = hg_in.shape[0]
    st = pl.BlockSpec((None, None, HG_DK, HG_DV, n), lambda h: (layer, h, 0, 0, 0))
    chan = pltpu.VMEM((HG_WIDTH, n), F32)
    return pl.pallas_call(
        _hgrn_step_kernel,
        grid=(HG_HEADS,),
        in_specs=[pl.BlockSpec((n, HG_COLS), lambda h: (0, 0)),
                  _layer_spec(layer, (HG_WIDTH, 1), 1),
                  _layer_spec(layer, (HG_DV, 1), 1),
                  st,
                  pl.BlockSpec(memory_space=pl.ANY)],
        out_specs=[pl.BlockSpec((HG_DV, n), lambda h: (h, 0)), st],
        out_shape=[jax.ShapeDtypeStruct((HG_WIDTH, n), F32),
                   jax.ShapeDtypeStruct(s_all.shape, F32)],
        scratch_shapes=[chan, chan, chan, chan, chan],
        input_output_aliases={4: 1},
        compiler_params=pltpu.CompilerParams(
            dimension_semantics=("arbitrary",), vmem_limit_bytes=VMEM_LIMIT),
        name="hgrn_step",
    )(hg_in, lb_c, nw_c, s_all, acc)


def _ret_step_kernel(x_ref, cos_ref, sin_ref, gam_ref, nw_ref, s_ref, acc_ref, o_ref, sn_ref,
                     qk_scr, v_scr, g_scr):
    del acc_ref
    half = RET_DK // 2
    h = pl.program_id(0)

    @pl.when(h == 0)
    def _():
        x = x_ref[:, 0:2 * RET_QK].T
        parts = []
        for i in range(2 * RET_HEADS):
            parts.append(x[RET_DK * i + half:RET_DK * (i + 1)])
            parts.append(x[RET_DK * i:RET_DK * i + half])
        qk_scr[...] = x * cos_ref[...] + jnp.concatenate(parts, axis=0) * sin_ref[...]
        v_scr[...] = x_ref[:, 2 * RET_QK:2 * RET_QK + RET_WIDTH].T
        g_scr[...] = x_ref[:, 2 * RET_QK + RET_WIDTH:2 * RET_QK + 2 * RET_WIDTH].T

    q = qk_scr[pl.ds(pl.multiple_of(h * RET_DK, RET_DK), RET_DK), :]
    kk = qk_scr[pl.ds(pl.multiple_of(RET_QK + h * RET_DK, RET_DK), RET_DK), :] * (RET_DK ** -0.5)
    vrows = pl.ds(pl.multiple_of(h * RET_DV, RET_DV), RET_DV)
    v = v_scr[vrows, :]
    gam = gam_ref[h]
    o = jnp.zeros_like(v)
    for k in range(RET_DK):
        sn = gam * s_ref[k] + kk[k:k + 1] * v
        sn_ref[k] = sn
        o = o + q[k:k + 1] * sn
    mu = jnp.mean(o, axis=0, keepdims=True)
    oc = o - mu
    var = jnp.mean(oc * oc, axis=0, keepdims=True)
    o_ref[...] = oc * lax.rsqrt(var + EPS) * nw_ref[vrows, :] * _silu(g_scr[vrows, :])


def _ret_step(layer, rt_in, cos_c, sin_c, gam, nw_c, s_all, acc):
    n = rt_in.shape[0]
    st = pl.BlockSpec((None, None, RET_DK, RET_DV, n), lambda h: (layer, h, 0, 0, 0))
    par = pl.BlockSpec((2 * RET_QK, 1), lambda h: (0, 0))
    return pl.pallas_call(
        _ret_step_kernel,
        grid=(RET_HEADS,),
        in_specs=[pl.BlockSpec((n, RT_COLS), lambda h: (0, 0)),
                  par, par,
                  pl.BlockSpec((RET_HEADS, 1, n), lambda h: (0, 0, 0)),
                  _layer_spec(layer, (RET_WIDTH, 1), 1),
                  st,
                  pl.BlockSpec(memory_space=pl.ANY)],
        out_specs=[pl.BlockSpec((RET_DV, n), lambda h: (h, 0)), st],
        out_shape=[jax.ShapeDtypeStruct((RET_WIDTH, n), F32),
                   jax.ShapeDtypeStruct(s_all.shape, F32)],
        scratch_shapes=[pltpu.VMEM((2 * RET_QK, n), F32),
                        pltpu.VMEM((RET_WIDTH, n), F32),
                        pltpu.VMEM((RET_WIDTH, n), F32)],
        input_output_aliases={6: 1},
        compiler_params=pltpu.CompilerParams(
            dimension_semantics=("arbitrary",), vmem_limit_bytes=VMEM_LIMIT),
        name="ret_step",
    )(rt_in, cos_c, sin_c, gam, nw_c, s_all, acc)


def _s5_step_kernel(u_ref, xr_ref, xi_ref, bt_ref, ct_ref, ab_ref, d_ref, gwt_ref, gb_ref,
                    accr_ref, acci_ref, y_ref, xrn_ref, xin_ref):
    del accr_ref, acci_ref
    n = S5_N
    ut = u_ref[...].T
    bu = _dot_hi(bt_ref[...], ut)
    mr, mi = _cmul(ab_ref[0:n], ab_ref[n:2 * n], xr_ref[...], xi_ref[...])
    xr = mr + bu[0:n]
    xi = mi + bu[n:2 * n]
    xrn_ref[...] = xr
    xin_ref[...] = xi
    y = _dot_hi(ct_ref[...], jnp.concatenate([xr, xi], axis=0)) + d_ref[...] * ut
    y = jax.nn.gelu(y, approximate=True)
    y_ref[...] = y * _sigmoid(_dot(gwt_ref[...], y.astype(BF16)) + gb_ref[...])


def _s5_step(layer, u, xr_all, xi_all, bt, ct, ab, dcol, gwt, gbcol, accr, acci):
    n = u.shape[0]
    st = _layer_spec(layer, (S5_N, n), 1)
    anyspace = pl.BlockSpec(memory_space=pl.ANY)
    return pl.pallas_call(
        _s5_step_kernel,
        grid=(1,),
        in_specs=[pl.BlockSpec((n, S5_WIDTH), lambda i: (0, 0)), st, st,
                  _layer_spec(layer, (2 * S5_N, S5_WIDTH), 1),
                  _layer_spec(layer, (S5_WIDTH, 2 * S5_N), 1),
                  _layer_spec(layer, (2 * S5_N, 1), 1),
                  _layer_spec(layer, (S5_WIDTH, 1), 1),
                  _layer_spec(layer, (S5_WIDTH, S5_WIDTH), 1),
                  _layer_spec(layer, (S5_WIDTH, 1), 1),
                  anyspace, anyspace],
        out_specs=[pl.BlockSpec((S5_WIDTH, n), lambda i: (0, 0)), st, st],
        out_shape=[jax.ShapeDtypeStruct((S5_WIDTH, n), F32),
                   jax.ShapeDtypeStruct(xr_all.shape, F32),
                   jax.ShapeDtypeStruct(xi_all.shape, F32)],
        input_output_aliases={9: 1, 10: 2},
        compiler_params=pltpu.CompilerParams(
            dimension_semantics=("arbitrary",), vmem_limit_bytes=VMEM_LIMIT),
        name="s5_step",
    )(u, xr_all, xi_all, bt, ct, ab, dcol, gwt, gbcol, accr, acci)


def _tile_order():
    rows = np.arange(TIME_BLOCK)
    return BLOCK_LEN * (rows % SUBLANES) + rows // SUBLANES


def _permute_tokens(x, inverse=False):
    bsz, tlen, dm = x.shape
    a, b = (BLOCK_LEN, SUBLANES) if inverse else (SUBLANES, BLOCK_LEN)
    x = x.reshape(bsz, tlen // TIME_BLOCK, a, b, dm)
    return jnp.swapaxes(x, 2, 3).reshape(bsz, tlen, dm)


def _rope_tables(pos):
    half = RET_DK // 2
    inv = ROPE_BASE ** (-np.arange(half, dtype=np.float64) / half)
    ang = np.asarray(pos, np.float64)[:, None] * inv[None, :]
    cos_h = np.concatenate([np.cos(ang), np.cos(ang)], axis=1)
    sin_h = np.concatenate([-np.sin(ang), np.sin(ang)], axis=1)
    return (np.tile(cos_h, (1, RET_HEADS)).astype(np.float32),
            np.tile(sin_h, (1, RET_HEADS)).astype(np.float32))


def _position_tables(tlen):
    tb = TIME_BLOCK
    order = _tile_order()
    pos_p = np.arange(tlen) // tb * tb + np.tile(order, tlen // tb)
    cos_p, sin_p = _rope_tables(pos_p)
    cos_s, sin_s = _rope_tables(np.array([PAST_LEN]))
    log_gamma = np.log1p(-np.exp2(-5.0 - np.arange(RET_HEADS, dtype=np.float64)))
    lg_lane = np.repeat(log_gamma, RET_DK)[None, :]
    tt = order.astype(np.float64)
    diff = tt[:, None] - tt[None, :]
    f32 = lambda a: jnp.asarray(np.asarray(a, np.float32))
    return dict(
        cos_qk=f32(np.concatenate([cos_p, cos_p], axis=1)),
        sin_qk=f32(np.concatenate([sin_p, sin_p], axis=1)),
        cos_col=f32(np.concatenate([cos_s, cos_s], axis=1).reshape(2 * RET_QK, 1)),
        sin_col=f32(np.concatenate([sin_s, sin_s], axis=1).reshape(2 * RET_QK, 1)),
        dq=f32(np.exp((tt[:, None] + 1.0) * lg_lane)),
        dk=f32(np.exp((tb - 1.0 - tt[:, None]) * lg_lane)),
        dmat=f32(np.where(diff >= 0, np.exp(np.maximum(diff, 0.0)[None] * log_gamma[:, None, None]), 0.0)),
        gam_tbl=f32(np.broadcast_to(np.exp(tb * np.repeat(log_gamma, RET_DK))[:, None],
                                    (RET_QK, RET_WIDTH))),
        gamma=np.exp(log_gamma),
    )


def _block_diag_mask(rows_per, cols_per, nblk):
    r = np.arange(rows_per * nblk)[:, None] // rows_per
    c = np.arange(cols_per * nblk)[None, :] // cols_per
    return (r == c).astype(np.float32)


def _s5_params(log_dt, a_re, a_im, b_re, b_im, c_re, c_im):
    nl = log_dt.shape[0]
    dt = jnp.exp(log_dt)[..., None]
    mag = jnp.exp(dt * a_re)
    ab_re = mag * jnp.cos(dt * a_im)
    ab_im = mag * jnp.sin(dt * a_im)
    den = a_re * a_re + a_im * a_im
    nr = ab_re - 1.0
    g_re = (nr * a_re + ab_im * a_im) / den
    g_im = (ab_im * a_re - nr * a_im) / den
    bb_re = g_re[..., None] * b_re - g_im[..., None] * b_im
    bb_im = g_re[..., None] * b_im + g_im[..., None] * b_re
    eye = jnp.eye(S5_GROUPS, dtype=F32)[None, :, None, :, None]

    def in_blk(bb):
        m = eye * jnp.swapaxes(bb, 2, 3)[:, :, :, None, :]
        return m.reshape(nl, S5_WIDTH, S5_N)

    def out_blk(c):
        m = eye * jnp.swapaxes(c, 2, 3)[:, :, :, None, :]
        return m.reshape(nl, S5_N, S5_WIDTH)

    bblk = jnp.concatenate([in_blk(bb_re), in_blk(bb_im)], axis=2)
    cblk = jnp.concatenate([out_blk(c_re), -out_blk(c_im)], axis=1)
    m = jnp.arange(1, TIME_BLOCK + 1, dtype=F32)[None, :, None, None]
    pmag = jnp.exp(m * (dt * a_re)[:, None])
    pang = m * (dt * a_im)[:, None]
    apow = jnp.concatenate([(pmag * jnp.cos(pang)).reshape(nl, TIME_BLOCK, S5_N),
                            (pmag * jnp.sin(pang)).reshape(nl, TIME_BLOCK, S5_N)], axis=2)
    ab = jnp.concatenate([ab_re.reshape(nl, S5_N, 1), ab_im.reshape(nl, S5_N, 1)], axis=1)
    return bblk, cblk, apow, ab


def kernel(x_prompt, x_sample, state_hgrn, state_ret, state_s5_re, state_s5_im, w_in, hgrn_lb_logits, hgrn_norm_w, ret_norm_w, s5_log_dt, s5_a_re, s5_a_im, s5_b_re, s5_b_im, s5_c_re, s5_c_im, s5_d, s5_glu_w, s5_glu_b, w_out, ln1_w, ln1_b, w_up, w_down, ln2_w, ln2_b):
    bp, tp, _ = x_prompt.shape
    bs = x_sample.shape[0]
    tb = TIME_BLOCK

    lb_prob = jax.nn.softmax(hgrn_lb_logits.astype(F32), axis=0)
    lower_bounds = jnp.cumsum(lb_prob, axis=0) - lb_prob[0:1]

    tbl = _position_tables(tp)
    ret_bd = jnp.asarray(_block_diag_mask(RET_DK, RET_DV, RET_HEADS))
    hg_bd = jnp.asarray(_block_diag_mask(HG_DV, HG_DK, 2))
    ones256 = jnp.asarray(_block_diag_mask(64, 64, 4)).astype(BF16)
    ones128 = jnp.asarray(_block_diag_mask(64, 64, 2)).astype(BF16)
    gam_lanes = jnp.asarray(np.broadcast_to(tbl["gamma"][:, None, None],
                                            (RET_HEADS, 1, bs)).astype(np.float32))

    wp = w_in.astype(BF16)
    wo = w_out.astype(BF16)
    ffn_par = (wo, ln1_w[:, None, :], ln1_b[:, None, :], w_up.astype(BF16), w_down.astype(BF16),
               ln2_w[:, None, :], ln2_b[:, None, :])
    lb_row = lower_bounds[:, None, :]
    hnw_row = jnp.tile(hgrn_norm_w, (1, HG_HEADS))[:, None, :]
    rnw_row = ret_norm_w[:, None, :]
    bblk, cblk, apow, ab_col = _s5_params(s5_log_dt, s5_a_re, s5_a_im, s5_b_re, s5_b_im,
                                          s5_c_re, s5_c_im)
    s5_par = (bblk.astype(BF16), cblk.astype(BF16), apow, apow[:, BLOCK_LEN - 1::BLOCK_LEN],
              s5_d[:, None, :], s5_glu_w.astype(BF16), s5_glu_b[:, None, :])
    s5_step_par = (jnp.swapaxes(bblk, 1, 2), jnp.swapaxes(cblk, 1, 2), ab_col, s5_d[:, :, None],
                   jnp.swapaxes(s5_glu_w, 1, 2).astype(BF16), s5_glu_b[:, :, None])

    hg_s = jnp.transpose(state_hgrn, (0, 2, 3, 4, 1))
    rt_s = jnp.transpose(state_ret, (0, 2, 3, 4, 1))
    re_s = jnp.transpose(state_s5_re, (0, 2, 3, 1)).reshape(DEPTH, S5_N, bs)
    im_s = jnp.transpose(state_s5_im, (0, 2, 3, 1)).reshape(DEPTH, S5_N, bs)
    new_hg_s = jnp.zeros(hg_s.shape, F32)
    new_rt_s = jnp.zeros(rt_s.shape, F32)
    new_re_s = jnp.zeros(re_s.shape, F32)
    new_im_s = jnp.zeros(im_s.shape, F32)

    xp = _permute_tokens(x_prompt).reshape(bp * tp, D_MODEL)
    xs = x_sample.reshape(bs, D_MODEL)
    outs = {k: [] for k in ("hg_p", "rt_p", "re_p", "im_p")}
    for l in range(DEPTH):
        hg_in, rt_in, s5_in = _proj(xp, wp, l, 512)
        o_hg, st_hg = _hgrn(hg_in, lb_row, hnw_row, ones256, ones128, hg_bd, l, bp, tp)
        o_rt, st_rt = _ret(rt_in, tbl["cos_qk"], tbl["sin_qk"], tbl["dq"], tbl["dk"], tbl["dmat"],
                           tbl["gam_tbl"], ret_bd, rnw_row, ones256, ones128, l, bp, tp)
        y5, st_s5 = _s5(s5_in, *s5_par, l, bp, tp)
        xp = _ffn(xp, o_hg, o_rt, y5, *ffn_par, l, 512)
        hg_state = jnp.stack(
            [st_hg[:, h // 2, 64 * (h % 2):64 * (h % 2 + 1), 64 * (h % 2):64 * (h % 2 + 1)]
             for h in range(HG_HEADS)], axis=1)
        outs["hg_p"].append(jnp.swapaxes(hg_state, -1, -2))
        outs["rt_p"].append(jnp.stack(
            [st_rt[:, RET_DK * h:RET_DK * (h + 1), RET_DV * h:RET_DV * (h + 1)]
             for h in range(RET_HEADS)], axis=1))
        outs["re_p"].append(st_s5[:, 0, :S5_N].reshape(bp, S5_GROUPS, S5_STATE))
        outs["im_p"].append(st_s5[:, 0, S5_N:].reshape(bp, S5_GROUPS, S5_STATE))

        hg_in, rt_in, s5_in = _proj(xs, wp, l, bs)
        o_hg_t, new_hg_s = _hgrn_step(l, hg_in, lower_bounds[:, :, None],
                                      hgrn_norm_w[:, :, None], hg_s, new_hg_s)
        o_rt_t, new_rt_s = _ret_step(l, rt_in, tbl["cos_col"], tbl["sin_col"], gam_lanes,
                                     ret_norm_w[:, :, None], rt_s, new_rt_s)
        y5_t, new_re_s, new_im_s = _s5_step(l, s5_in, re_s, im_s, *s5_step_par, new_re_s, new_im_s)
        xs = _ffn(xs, o_hg_t.T, o_rt_t.T, y5_t.T, *ffn_par, l, bs)

    yp = _permute_tokens(xp.reshape(bp, tp, D_MODEL), inverse=True)

    def s5_out(a):
        return jnp.transpose(a.reshape(DEPTH, S5_GROUPS, S5_STATE, bs), (0, 3, 1, 2))

    return (yp, xs.reshape(bs, 1, D_MODEL),
            jnp.stack(outs["hg_p"]), jnp.stack(outs["rt_p"]), jnp.stack(outs["re_p"]),
            jnp.stack(outs["im_p"]),
            jnp.transpose(new_hg_s, (0, 4, 1, 2, 3)), jnp.transpose(new_rt_s, (0, 4, 1, 2, 3)),
            s5_out(new_re_s), s5_out(new_im_s))
```

```python
import jax
import jax.numpy as jnp
import numpy as np
from jax import lax
from jax.experimental import pallas as pl
from jax.experimental.pallas import tpu as pltpu

F32 = jnp.float32
BF16 = jnp.bfloat16
HIGHEST = lax.Precision.HIGHEST

D_MODEL = 1024
DEPTH = 4
PAST_LEN = 16384
HG_WIDTH = 384
HG_HEADS = 6
HG_DK = 64
HG_DV = 64
RET_WIDTH = 384
RET_HEADS = 6
RET_DK = 32
RET_DV = 64
RET_QK = RET_HEADS * RET_DK
S5_WIDTH = 256
S5_GROUP = 16
S5_GROUPS = 16
S5_STATE = 64
S5_N = S5_GROUPS * S5_STATE
D_FF = 4 * D_MODEL
ROPE_BASE = 10000.0
EPS = 1e-5
ALPHA = (2.0 * DEPTH) ** 0.25

HG_COLS = 4 * HG_WIDTH
RT_COLS = 2 * RET_QK + 2 * RET_WIDTH
PROJ_COLS = HG_COLS + RT_COLS + S5_WIDTH
LANES = 128

SUBLANES = 8
TIME_BLOCK = 128
BLOCK_LEN = TIME_BLOCK // SUBLANES
SEQS_PER_STEP = 8
FF_CHUNK = 1024
VMEM_LIMIT = 56 * 1024 * 1024


def _dot(a, b):
    return jnp.dot(a, b, preferred_element_type=F32)


def _dot_nt(a, b):
    return lax.dot_general(a, b, (((1,), (1,)), ((), ())), preferred_element_type=F32)


def _dot_tn(a, b):
    return lax.dot_general(a, b, (((0,), (0,)), ((), ())), preferred_element_type=F32)


def _dot_hi(a, b):
    return jnp.dot(a, b, preferred_element_type=F32, precision=HIGHEST)


def _sigmoid(x):
    return jax.nn.sigmoid(x)


def _silu(x):
    return x * jax.nn.sigmoid(x)


def _layer_norm(y, w, b):
    mu = jnp.mean(y, -1, keepdims=True)
    yc = y - mu
    var = jnp.mean(yc * yc, -1, keepdims=True)
    return yc * lax.rsqrt(var + EPS) * w + b


def _head_sum_bf16(x, ones256, ones128):
    return jnp.concatenate([_dot(x[:, 0:256], ones256), _dot(x[:, 256:384], ones128)], axis=1)


def _head_sum(x, ones256, ones128):
    hi = x.astype(BF16)
    lo = (x - hi.astype(F32)).astype(BF16)
    return _head_sum_bf16(hi, ones256, ones128) + _head_sum_bf16(lo, ones256, ones128)


def _proj_kernel(x_ref, w_ref, hg_ref, rt_ref, s5_ref):
    x = x_ref[...].astype(BF16)
    hg_ref[...] = _dot(x, w_ref[:, 0:HG_COLS])
    rt_ref[...] = _dot(x, w_ref[:, HG_COLS:HG_COLS + RT_COLS])
    s5_ref[...] = _dot(x, w_ref[:, HG_COLS + RT_COLS:PROJ_COLS])


def _layer_spec(layer, shape, grid_rank, **kw):
    zeros = (0,) * len(shape)
    if grid_rank == 1:
        return pl.BlockSpec((None,) + shape, lambda i: (layer,) + zeros, **kw)
    return pl.BlockSpec((None,) + shape, lambda b, t: (layer,) + zeros, **kw)


def _proj(x, w, layer, tm):
    m = x.shape[0]
    return pl.pallas_call(
        _proj_kernel,
        grid=(m // tm,),
        in_specs=[pl.BlockSpec((tm, D_MODEL), lambda i: (i, 0)),
                  _layer_spec(layer, (D_MODEL, PROJ_COLS), 1, pipeline_mode=pl.Buffered(1))],
        out_specs=[pl.BlockSpec((tm, HG_COLS), lambda i: (i, 0)),
                   pl.BlockSpec((tm, RT_COLS), lambda i: (i, 0)),
                   pl.BlockSpec((tm, S5_WIDTH), lambda i: (i, 0))],
        out_shape=[jax.ShapeDtypeStruct((m, HG_COLS), F32),
                   jax.ShapeDtypeStruct((m, RT_COLS), F32),
                   jax.ShapeDtypeStruct((m, S5_WIDTH), F32)],
        compiler_params=pltpu.CompilerParams(
            dimension_semantics=("parallel",), vmem_limit_bytes=VMEM_LIMIT),
        name="proj",
    )(x, w)


def _ffn_kernel(x_ref, a_ref, b_ref, c_ref, wo_ref, l1w_ref, l1b_ref, wu_ref, wd_ref,
                l2w_ref, l2b_ref, o_ref):
    x = x_ref[...]
    mixed = (_dot(a_ref[...].astype(BF16), wo_ref[0:HG_WIDTH, :])
             + _dot(b_ref[...].astype(BF16), wo_ref[HG_WIDTH:HG_WIDTH + RET_WIDTH, :])
             + _dot(c_ref[...].astype(BF16), wo_ref[HG_WIDTH + RET_WIDTH:D_MODEL, :]))
    x1 = _layer_norm(ALPHA * x + mixed, l1w_ref[...], l1b_ref[...])
    xb = x1.astype(BF16)
    ff = jnp.zeros_like(x1)
    for c in range(D_FF // FF_CHUNK):
        h = _dot(xb, wu_ref[:, c * FF_CHUNK:(c + 1) * FF_CHUNK])
        h = jnp.square(jnp.maximum(h, 0.0))
        ff = ff + _dot(h.astype(BF16), wd_ref[c * FF_CHUNK:(c + 1) * FF_CHUNK, :])
    o_ref[...] = _layer_norm(ALPHA * x1 + ff, l2w_ref[...], l2b_ref[...])


def _ffn(x, a, b, c, wo, l1w, l1b, wu, wd, l2w, l2b, layer, tm):
    m = x.shape[0]
    row = lambda i: (i, 0)
    once = pl.Buffered(1)
    vec = _layer_spec(layer, (1, D_MODEL), 1)
    return pl.pallas_call(
        _ffn_kernel,
        grid=(m // tm,),
        in_specs=[pl.BlockSpec((tm, D_MODEL), row),
                  pl.BlockSpec((tm, HG_WIDTH), row),
                  pl.BlockSpec((tm, RET_WIDTH), row),
                  pl.BlockSpec((tm, S5_WIDTH), row),
                  _layer_spec(layer, (D_MODEL, D_MODEL), 1, pipeline_mode=once),
                  vec, vec,
                  _layer_spec(layer, (D_MODEL, D_FF), 1, pipeline_mode=once),
                  _layer_spec(layer, (D_FF, D_MODEL), 1, pipeline_mode=once),
                  vec, vec],
        out_specs=pl.BlockSpec((tm, D_MODEL), row),
        out_shape=jax.ShapeDtypeStruct((m, D_MODEL), F32),
        compiler_params=pltpu.CompilerParams(
            dimension_semantics=("parallel",), vmem_limit_bytes=VMEM_LIMIT),
        name="ffn",
    )(x, a, b, c, wo, l1w, l1b, wu, wd, l2w, l2b)


_DONE = object()


def _per_sequence(body, n_shared_in):
    def kern(x_ref, *refs):
        shared = refs[:n_shared_in]
        per_seq = refs[n_shared_in:]
        state_out, state_scr = per_seq[1], per_seq[2]
        t = pl.program_id(1)

        @pl.when(t == 0)
        def _():
            state_scr[...] = jnp.zeros(state_scr.shape, F32)

        live = [body(x_ref.at[i], *shared, *(r.at[i] for r in per_seq))
                for i in range(SEQS_PER_STEP)]
        while live:
            live = [g for g in live if next(g, _DONE) is not _DONE]

        @pl.when(t == pl.num_programs(1) - 1)
        def _():
            state_out[...] = state_scr[...]
    return kern


def _seq_major(a):
    return a.reshape(SEQS_PER_STEP, a.shape[0] // SEQS_PER_STEP, a.shape[1])


def _hgrn_body(x_ref, lb_ref, nw_ref, ones256_ref, ones128_ref, bd_ref, o_ref, st_ref,
               s_scr, qt_scr, kt_scr, v_scr, oi_scr):
    del st_ref
    npos = BLOCK_LEN
    sl = SUBLANES
    lb = lb_ref[...]
    ones256 = ones256_ref[...]
    ones128 = ones128_ref[...]
    f, kk, q, v = [], [], [], []
    for p in range(npos):
        rows = slice(sl * p, sl * (p + 1))
        hq = x_ref[rows, 0:HG_WIDTH]
        z = x_ref[rows, HG_WIDTH:2 * HG_WIDTH]
        f.append(lb + (1.0 - lb) * _sigmoid(z))
        kk.append((1.0 - lb) * _sigmoid(-z))
        q.append(_silu(hq))
        v.append(x_ref[rows, 2 * HG_WIDTH:3 * HG_WIDTH])
    yield

    g = list(kk)
    e_rows = []
    for d in range(npos):
        if d > 0:
            g = [None] * d + [f[p] * g[p - 1] for p in range(d, npos)]
        e_rows.extend(q[p] * g[p] for p in range(d, npos))
    yield
    e_all = jnp.concatenate(e_rows, axis=0).astype(BF16)
    p_all = _head_sum_bf16(e_all, ones256, ones128)
    yield
    o = [None] * npos
    i = 0
    for d in range(npos):
        for p in range(d, npos):
            term = p_all[sl * i:sl * (i + 1)] * v[p - d]
            o[p] = term if o[p] is None else o[p] + term
            i += 1

    yield
    a = [f[0]]
    for p in range(1, npos):
        a.append(a[-1] * f[p])
    r = [None] * npos
    r[npos - 1] = jnp.ones_like(f[0])
    for p in range(npos - 2, -1, -1):
        r[p] = r[p + 1] * f[p + 1]
    npair = HG_HEADS // 2
    for p in range(npos):
        rows = slice(sl * p, sl * (p + 1))
        qt = q[p] * a[p]
        kt = kk[p] * r[p]
        for pr in range(npair):
            lanes = slice(128 * pr, 128 * (pr + 1))
            qt_scr[pr, rows, :] = qt[:, lanes]
            kt_scr[pr, rows, :] = kt[:, lanes]
            v_scr[pr, rows, :] = v[p][:, lanes]
    cd = a[npos - 1]
    for j in range(sl):
        blk = pl.ds(j, npos, stride=sl)
        for pr in range(npair):
            lanes = slice(128 * pr, 128 * (pr + 1))
            st = s_scr[pr]
            oi_scr[pr, blk, :] = _dot_nt(qt_scr[pr, blk, :].astype(BF16), st.astype(BF16))
            u = _dot_tn(v_scr[pr, blk, :].astype(BF16), kt_scr[pr, blk, :].astype(BF16))
            s_scr[pr] = st * cd[j:j + 1, lanes] + u * bd_ref[...]
        yield

    ot = (jnp.concatenate(o, axis=0)
          + jnp.concatenate([oi_scr[pr] for pr in range(npair)], axis=1))
    ms = _head_sum(ot * ot, ones256, ones128) * (1.0 / HG_DV)
    gate = x_ref[:, 3 * HG_WIDTH:4 * HG_WIDTH]
    o_ref[...] = ot * lax.rsqrt(ms + EPS) * nw_ref[...] * _silu(gate)


def _hgrn(hg_in, lb, nw, ones256, ones128, bd, layer, bsz, tlen):
    tb = TIME_BLOCK
    nt = tlen // tb
    ns = SEQS_PER_STEP
    npair = HG_HEADS // 2
    fixed2 = lambda b, t: (0, 0)
    tile = lambda b, t: (0, b * nt + t, 0)
    slab = pltpu.VMEM((ns, npair, tb, 128), F32)
    o, st = pl.pallas_call(
        _per_sequence(_hgrn_body, 5),
        grid=(bsz // ns, nt),
        in_specs=[pl.BlockSpec((ns, tb, HG_COLS), tile),
                  _layer_spec(layer, (1, HG_WIDTH), 2),
                  _layer_spec(layer, (1, HG_WIDTH), 2),
                  pl.BlockSpec((256, 256), fixed2),
                  pl.BlockSpec((128, 128), fixed2),
                  pl.BlockSpec((128, 128), fixed2)],
        out_specs=[pl.BlockSpec((ns, tb, HG_WIDTH), tile),
                   pl.BlockSpec((ns, None, npair, 128, 128), lambda b, t: (0, b, 0, 0, 0))],
        out_shape=[jax.ShapeDtypeStruct((ns, bsz // ns * tlen, HG_WIDTH), F32),
                   jax.ShapeDtypeStruct((ns, bsz // ns, npair, 128, 128), F32)],
        scratch_shapes=[pltpu.VMEM((ns, npair, 128, 128), F32), slab, slab, slab, slab],
        compiler_params=pltpu.CompilerParams(
            dimension_semantics=("parallel", "arbitrary"), vmem_limit_bytes=VMEM_LIMIT),
        name="hgrn",
    )(_seq_major(hg_in), lb, nw, ones256, ones128, bd)
    return o.reshape(bsz * tlen, HG_WIDTH), st.reshape(bsz, npair, 128, 128)


def _ret_body(x_ref, cos_ref, sin_ref, dq_ref, dk_ref, dmat_ref, gam_ref, bd_ref, nw_ref,
              ones256_ref, ones128_ref, o_ref, st_ref, s_scr):
    del st_ref
    tb = TIME_BLOCK
    w = RET_QK
    half = RET_DK // 2
    in_first_half = (lax.broadcasted_iota(jnp.int32, (tb, LANES), 1) & (RET_DK - 1)) < half
    rot = []
    for c in range(2 * w // LANES):
        lanes = slice(LANES * c, LANES * (c + 1))
        x = x_ref[:, lanes]
        partner = jnp.where(in_first_half, pltpu.roll(x, LANES - half, 1), pltpu.roll(x, half, 1))
        rot.append(x * cos_ref[:, lanes] + partner * sin_ref[:, lanes])
    rot = jnp.concatenate(rot, axis=1)
    q = rot[:, 0:w]
    k = rot[:, w:2 * w] * (RET_DK ** -0.5)
    v = x_ref[:, 2 * w:2 * w + RET_WIDTH]
    gate = x_ref[:, 2 * w + RET_WIDTH:2 * w + 2 * RET_WIDTH]
    vb = v.astype(BF16)

    lane = lax.broadcasted_iota(jnp.int32, (tb, w), 1)
    qs = jnp.concatenate(
        [jnp.where((lane >= RET_DK * h) & (lane < RET_DK * (h + 1)), q, 0.0)
         for h in range(RET_HEADS)], axis=0).astype(BF16)
    yield
    sc = _dot_nt(qs, k.astype(BF16))
    yield
    lane128 = lax.broadcasted_iota(jnp.int32, (tb, 128), 1)
    outs = []
    for p in range(RET_HEADS // 2):
        p0 = sc[(2 * p) * tb:(2 * p + 1) * tb] * dmat_ref[2 * p]
        p1 = sc[(2 * p + 1) * tb:(2 * p + 2) * tb] * dmat_ref[2 * p + 1]
        pc = jnp.concatenate([p0, p1], axis=1).astype(BF16)
        vp = v[:, 128 * p:128 * (p + 1)]
        vs = jnp.concatenate([jnp.where(lane128 < RET_DV, vp, 0.0),
                              jnp.where(lane128 >= RET_DV, vp, 0.0)], axis=0).astype(BF16)
        outs.append(_dot(pc, vs))
    o = jnp.concatenate(outs, axis=1)
    yield

    s = s_scr[...]
    o = o + _dot((q * dq_ref[...]).astype(BF16), s.astype(BF16))
    u = _dot_tn((k * dk_ref[...]).astype(BF16), vb)
    s_scr[...] = s * gam_ref[...] + u * bd_ref[...]

    ones256 = ones256_ref[...]
    ones128 = ones128_ref[...]
    mu = _head_sum(o, ones256, ones128) * (1.0 / RET_DV)
    oc = o - mu
    var = _head_sum(oc * oc, ones256, ones128) * (1.0 / RET_DV)
    o_ref[...] = oc * lax.rsqrt(var + EPS) * nw_ref[...] * _silu(gate)


def _ret(rt_in, cos, sin, dq, dk, dmat, gam, bd, nw, ones256, ones128, layer, bsz, tlen):
    tb = TIME_BLOCK
    nt = tlen // tb
    ns = SEQS_PER_STEP
    fixed2 = lambda b, t: (0, 0)
    tile = lambda b, t: (0, b * nt + t, 0)
    o, st = pl.pallas_call(
        _per_sequence(_ret_body, 10),
        grid=(bsz // ns, nt),
        in_specs=[pl.BlockSpec((ns, tb, RT_COLS), tile),
                  pl.BlockSpec((tb, 2 * RET_QK), lambda b, t: (t, 0)),
                  pl.BlockSpec((tb, 2 * RET_QK), lambda b, t: (t, 0)),
                  pl.BlockSpec((tb, RET_QK), fixed2),
                  pl.BlockSpec((tb, RET_QK), fixed2),
                  pl.BlockSpec((RET_HEADS, tb, tb), lambda b, t: (0, 0, 0)),
                  pl.BlockSpec((RET_QK, RET_WIDTH), fixed2),
                  pl.BlockSpec((RET_QK, RET_WIDTH), fixed2),
                  _layer_spec(layer, (1, RET_WIDTH), 2),
                  pl.BlockSpec((256, 256), fixed2),
                  pl.BlockSpec((128, 128), fixed2)],
        out_specs=[pl.BlockSpec((ns, tb, RET_WIDTH), tile),
                   pl.BlockSpec((ns, None, RET_QK, RET_WIDTH), lambda b, t: (0, b, 0, 0))],
        out_shape=[jax.ShapeDtypeStruct((ns, bsz // ns * tlen, RET_WIDTH), F32),
                   jax.ShapeDtypeStruct((ns, bsz // ns, RET_QK, RET_WIDTH), F32)],
        scratch_shapes=[pltpu.VMEM((ns, RET_QK, RET_WIDTH), F32)],
        compiler_params=pltpu.CompilerParams(
            dimension_semantics=("parallel", "arbitrary"), vmem_limit_bytes=VMEM_LIMIT),
        name="ret",
    )(_seq_major(rt_in), cos, sin, dq, dk, dmat, gam, bd, nw, ones256, ones128)
    return o.reshape(bsz * tlen, RET_WIDTH), st.reshape(bsz, RET_QK, RET_WIDTH)


def _cmul(ar, ai, xr, xi):
    return ar * xr - ai * xi, ar * xi + ai * xr


def _s5_body(u_ref, bblk_ref, cblk_ref, apow_ref, aend_ref, d_ref, gw_ref, gb_ref, y_ref, st_ref,
             x_scr):
    del st_ref
    npos = BLOCK_LEN
    sl = SUBLANES
    n = S5_N

    def power(m):
        return apow_ref[m - 1:m, 0:n], apow_ref[m - 1:m, n:2 * n]

    u = u_ref[...]
    bu = _dot(u.astype(BF16), bblk_ref[...])
    a1r, a1i = power(1)
    xr = [bu[0:sl, 0:n]]
    xi = [bu[0:sl, n:2 * n]]
    for p in range(1, npos):
        rows = slice(sl * p, sl * (p + 1))
        mr, mi = _cmul(a1r, a1i, xr[-1], xi[-1])
        xr.append(bu[rows, 0:n] + mr)
        xi.append(bu[rows, n:2 * n] + mi)

    yield
    blk = lax.broadcasted_iota(jnp.int32, (sl, n), 0)
    cr, ci = xr[-1], xi[-1]
    s = 1
    while s < sl:
        pr_, pi_ = power(npos * s)
        sr = jnp.where(blk >= s, pltpu.roll(cr, s, 0), 0.0)
        si = jnp.where(blk >= s, pltpu.roll(ci, s, 0), 0.0)
        mr, mi = _cmul(pr_, pi_, sr, si)
        cr, ci = cr + mr, ci + mi
        s *= 2
    x0r = x_scr[:, 0:n]
    x0i = x_scr[:, n:2 * n]
    mr, mi = _cmul(aend_ref[:, 0:n], aend_ref[:, n:2 * n], x0r, x0i)
    cr, ci = cr + mr, ci + mi
    x_scr[:, 0:n] = cr[sl - 1:sl]
    x_scr[:, n:2 * n] = ci[sl - 1:sl]
    inr = jnp.where(blk >= 1, pltpu.roll(cr, 1, 0), x0r)
    ini = jnp.where(blk >= 1, pltpu.roll(ci, 1, 0), x0i)
    rows_out = []
    for p in range(npos):
        pr_, pi_ = power(p + 1)
        mr, mi = _cmul(pr_, pi_, inr, ini)
        rows_out.append(jnp.concatenate([xr[p] + mr, xi[p] + mi], axis=1))
    xc = jnp.concatenate(rows_out, axis=0).astype(BF16)
    yield

    y = _dot(xc, cblk_ref[...]) + d_ref[...] * u
    y = jax.nn.gelu(y, approximate=True)
    y_ref[...] = y * _sigmoid(_dot(y.astype(BF16), gw_ref[...]) + gb_ref[...])


def _s5(u, bblk, cblk, apow, aend, dvec, gw, gb, layer, bsz, tlen):
    tb = TIME_BLOCK
    nt = tlen // tb
    ns = SEQS_PER_STEP
    tile = lambda b, t: (0, b * nt + t, 0)
    y, st = pl.pallas_call(
        _per_sequence(_s5_body, 7),
        grid=(bsz // ns, nt),
        in_specs=[pl.BlockSpec((ns, tb, S5_WIDTH), tile),
                  _layer_spec(layer, (S5_WIDTH, 2 * S5_N), 2),
                  _layer_spec(layer, (2 * S5_N, S5_WIDTH), 2),
                  _layer_spec(layer, (tb, 2 * S5_N), 2),
                  _layer_spec(layer, (SUBLANES, 2 * S5_N), 2),
                  _layer_spec(layer, (1, S5_WIDTH), 2),
                  _layer_spec(layer, (S5_WIDTH, S5_WIDTH), 2),
                  _layer_spec(layer, (1, S5_WIDTH), 2)],
        out_specs=[pl.BlockSpec((ns, tb, S5_WIDTH), tile),
                   pl.BlockSpec((ns, None, 1, 2 * S5_N), lambda b, t: (0, b, 0, 0))],
        out_shape=[jax.ShapeDtypeStruct((ns, bsz // ns * tlen, S5_WIDTH), F32),
                   jax.ShapeDtypeStruct((ns, bsz // ns, 1, 2 * S5_N), F32)],
        scratch_shapes=[pltpu.VMEM((ns, 1, 2 * S5_N), F32)],
        compiler_params=pltpu.CompilerParams(
            dimension_semantics=("parallel", "arbitrary"), vmem_limit_bytes=VMEM_LIMIT),
        name="s5",
    )(_seq_major(u), bblk, cblk, apow, aend, dvec, gw, gb)
    return y.reshape(bsz * tlen, S5_WIDTH), st.reshape(bsz, 1, 2 * S5_N)


def _hgrn_step_kernel(x_ref, lb_ref, nw_ref, s_ref, acc_ref, o_ref, sn_ref,
                      f_scr, k_scr, q_scr, v_scr, g_scr):
    del acc_ref
    h = pl.program_id(0)

    @pl.when(h == 0)
    def _():
        lb = lb_ref[...]
        z = x_ref[:, HG_WIDTH:2 * HG_WIDTH].T
        f_scr[...] = lb + (1.0 - lb) * _sigmoid(z)
        k_scr[...] = (1.0 - lb) * _sigmoid(-z)
        q_scr[...] = _silu(x_ref[:, 0:HG_WIDTH].T)
        v_scr[...] = x_ref[:, 2 * HG_WIDTH:3 * HG_WIDTH].T
        g_scr[...] = x_ref[:, 3 * HG_WIDTH:4 * HG_WIDTH].T

    rows = pl.ds(pl.multiple_of(h * HG_DK, HG_DK), HG_DK)
    f = f_scr[rows, :]
    kk = k_scr[rows, :]
    q = q_scr[rows, :]
    v = v_scr[rows, :]
    o = jnp.zeros_like(v)
    for k in range(HG_DK):
        sn = f[k:k + 1] * s_ref[k] + kk[k:k + 1] * v
        sn_ref[k] = sn
        o = o + q[k:k + 1] * sn
    ms = jnp.mean(o * o, axis=0, keepdims=True)
    o_ref[...] = o * lax.rsqrt(ms + EPS) * nw_ref[...] * _silu(g_scr[rows, :])


def _hgrn_step(layer, hg_in, lb_c, nw_c, s_all, acc):
    n = hg_in.shape[0]
    st = pl.BlockSpec((None, None, HG_DK, HG_DV, n), lambda h: (layer, h, 0, 0, 0))
    chan = pltpu.VMEM((HG_WIDTH, n), F32)
    return pl.pallas_call(
        _hgrn_step_kernel,
        grid=(HG_HEADS,),
        in_specs=[pl.BlockSpec((n, HG_COLS), lambda h: (0, 0)),
                  _layer_spec(layer, (HG_WIDTH, 1), 1),
                  _layer_spec(layer, (HG_DV, 1), 1),
                  st,
                  pl.BlockSpec(memory_space=pl.ANY)],
        out_specs=[pl.BlockSpec((HG_DV, n), lambda h: (h, 0)), st],
        out_shape=[jax.ShapeDtypeStruct((HG_WIDTH, n), F32),
                   jax.ShapeDtypeStruct(s_all.shape, F32)],
        scratch_shapes=[chan, chan, chan, chan, chan],
        input_output_aliases={4: 1},
        compiler_params=pltpu.CompilerParams(
            dimension_semantics=("arbitrary",), vmem_limit_bytes=VMEM_LIMIT),
        name="hgrn_step",
    )(hg_in, lb_c, nw_c, s_all, acc)


def _ret_step_kernel(x_ref, cos_ref, sin_ref, gam_ref, nw_ref, s_ref, acc_ref, o_ref, sn_ref,
                     qk_scr, v_scr, g_scr):
    del acc_ref
    half = RET_DK // 2
    h = pl.program_id(0)

    @pl.when(h == 0)
    def _():
        x = x_ref[:, 0:2 * RET_QK].T
        parts = []
        for i in range(2 * RET_HEADS):
            parts.append(x[RET_DK * i + half:RET_DK * (i + 1)])
            parts.append(x[RET_DK * i:RET_DK * i + half])
        qk_scr[...] = x * cos_ref[...] + jnp.concatenate(parts, axis=0) * sin_ref[...]
        v_scr[...] = x_ref[:, 2 * RET_QK:2 * RET_QK + RET_WIDTH].T
        g_scr[...] = x_ref[:, 2 * RET_QK + RET_WIDTH:2 * RET_QK + 2 * RET_WIDTH].T

    q = qk_scr[pl.ds(pl.multiple_of(h * RET_DK, RET_DK), RET_DK), :]
    kk = qk_scr[pl.ds(pl.multiple_of(RET_QK + h * RET_DK, RET_DK), RET_DK), :] * (RET_DK ** -0.5)
    vrows = pl.ds(pl.multiple_of(h * RET_DV, RET_DV), RET_DV)
    v = v_scr[vrows, :]
    gam = gam_ref[h]
    o = jnp.zeros_like(v)
    for k in range(RET_DK):
        sn = gam * s_ref[k] + kk[k:k + 1] * v
        sn_ref[k] = sn
        o = o + q[k:k + 1] * sn
    mu = jnp.mean(o, axis=0, keepdims=True)
    oc = o - mu
    var = jnp.mean(oc * oc, axis=0, keepdims=True)
    o_ref[...] = oc * lax.rsqrt(var + EPS) * nw_ref[vrows, :] * _silu(g_scr[vrows, :])


def _ret_step(layer, rt_in, cos_c, sin_c, gam, nw_c, s_all, acc):
    n = rt_in.shape[0]
    st = pl.BlockSpec((None, None, RET_DK, RET_DV, n), lambda h: (layer, h, 0, 0, 0))
    par = pl.BlockSpec((2 * RET_QK, 1), lambda h: (0, 0))
    return pl.pallas_call(
        _ret_step_kernel,
        grid=(RET_HEADS,),
        in_specs=[pl.BlockSpec((n, RT_COLS), lambda h: (0, 0)),
                  par, par,
                  pl.BlockSpec((RET_HEADS, 1, n), lambda h: (0, 0, 0)),
                  _layer_spec(layer, (RET_WIDTH, 1), 1),
                  st,
                  pl.BlockSpec(memory_space=pl.ANY)],
        out_specs=[pl.BlockSpec((RET_DV, n), lambda h: (h, 0)), st],
        out_shape=[jax.ShapeDtypeStruct((RET_WIDTH, n), F32),
                   jax.ShapeDtypeStruct(s_all.shape, F32)],
        scratch_shapes=[pltpu.VMEM((2 * RET_QK, n), F32),
                        pltpu.VMEM((RET_WIDTH, n), F32),
                        pltpu.VMEM((RET_WIDTH, n), F32)],
        input_output_aliases={6: 1},
        compiler_params=pltpu.CompilerParams(
            dimension_semantics=("arbitrary",), vmem_limit_bytes=VMEM_LIMIT),
        name="ret_step",
    )(rt_in, cos_c, sin_c, gam, nw_c, s_all, acc)


def _s5_step_kernel(u_ref, xr_ref, xi_ref, bt_ref, ct_ref, ab_ref, d_ref, gwt_ref, gb_ref,
                    accr_ref, acci_ref, y_ref, xrn_ref, xin_ref):
    del accr_ref, acci_ref
    n = S5_N
    ut = u_ref[...].T
    bu = _dot_hi(bt_ref[...], ut)
    mr, mi = _cmul(ab_ref[0:n], ab_ref[n:2 * n], xr_ref[...], xi_ref[...])
    xr = mr + bu[0:n]
    xi = mi + bu[n:2 * n]
    xrn_ref[...] = xr
    xin_ref[...] = xi
    y = _dot_hi(ct_ref[...], jnp.concatenate([xr, xi], axis=0)) + d_ref[...] * ut
    y = jax.nn.gelu(y, approximate=True)
    y_ref[...] = y * _sigmoid(_dot(gwt_ref[...], y.astype(BF16)) + gb_ref[...])


def _s5_step(layer, u, xr_all, xi_all, bt, ct, ab, dcol, gwt, gbcol, accr, acci):
    n = u.shape[0]
    st = _layer_spec(layer, (S5_N, n), 1)
    anyspace = pl.BlockSpec(memory_space=pl.ANY)
    return pl.pallas_call(
        _s5_step_kernel,
        grid=(1,),
        in_specs=[pl.BlockSpec((n, S5_WIDTH), lambda i: (0, 0)), st, st,
                  _layer_spec(layer, (2 * S5_N, S5_WIDTH), 1),
                  _layer_spec(layer, (S5_WIDTH, 2 * S5_N), 1),
                  _layer_spec(layer, (2 * S5_N, 1), 1),
                  _layer_spec(layer, (S5_WIDTH, 1), 1),
                  _layer_spec(layer, (S5_WIDTH, S5_WIDTH), 1),
                  _layer_spec(layer, (S5_WIDTH, 1), 1),
                  anyspace, anyspace],
        out_specs=[pl.BlockSpec((S5_WIDTH, n), lambda i: (0, 0)), st, st],
        out_shape=[jax.ShapeDtypeStruct((S5_WIDTH, n), F32),
                   jax.ShapeDtypeStruct(xr_all.shape, F32),
                   jax.ShapeDtypeStruct(xi_all.shape, F32)],
        input_output_aliases={9: 1, 10: 2},
        compiler_params=pltpu.CompilerParams(
            dimension_semantics=("arbitrary",), vmem_limit_bytes=VMEM_LIMIT),
        name="s5_step",
    )(u, xr_all, xi_all, bt, ct, ab, dcol, gwt, gbcol, accr, acci)


def _tile_order():
    rows = np.arange(TIME_BLOCK)
    return BLOCK_LEN * (rows % SUBLANES) + rows // SUBLANES


def _permute_tokens(x, inverse=False):
    bsz, tlen, dm = x.shape
    a, b = (BLOCK_LEN, SUBLANES) if inverse else (SUBLANES, BLOCK_LEN)
    x = x.reshape(bsz, tlen // TIME_BLOCK, a, b, dm)
    return jnp.swapaxes(x, 2, 3).reshape(bsz, tlen, dm)


def _rope_tables(pos):
    half = RET_DK // 2
    inv = ROPE_BASE ** (-np.arange(half, dtype=np.float64) / half)
    ang = np.asarray(pos, np.float64)[:, None] * inv[None, :]
    cos_h = np.concatenate([np.cos(ang), np.cos(ang)], axis=1)
    sin_h = np.concatenate([-np.sin(ang), np.sin(ang)], axis=1)
    return (np.tile(cos_h, (1, RET_HEADS)).astype(np.float32),
            np.tile(sin_h, (1, RET_HEADS)).astype(np.float32))


def _position_tables(tlen):
    tb = TIME_BLOCK
    order = _tile_order()
    pos_p = np.arange(tlen) // tb * tb + np.tile(order, tlen // tb)
    cos_p, sin_p = _rope_tables(pos_p)
    cos_s, sin_s = _rope_tables(np.array([PAST_LEN]))
    log_gamma = np.log1p(-np.exp2(-5.0 - np.arange(RET_HEADS, dtype=np.float64)))
    lg_lane = np.repeat(log_gamma, RET_DK)[None, :]
    tt = order.astype(np.float64)
    diff = tt[:, None] - tt[None, :]
    f32 = lambda a: jnp.asarray(np.asarray(a, np.float32))
    return dict(
        cos_qk=f32(np.concatenate([cos_p, cos_p], axis=1)),
        sin_qk=f32(np.concatenate([sin_p, sin_p], axis=1)),
        cos_col=f32(np.concatenate([cos_s, cos_s], axis=1).reshape(2 * RET_QK, 1)),
        sin_col=f32(np.concatenate([sin_s, sin_s], axis=1).reshape(2 * RET_QK, 1)),
        dq=f32(np.exp((tt[:, None] + 1.0) * lg_lane)),
        dk=f32(np.exp((tb - 1.0 - tt[:, None]) * lg_lane)),
        dmat=f32(np.where(diff >= 0, np.exp(np.maximum(diff, 0.0)[None] * log_gamma[:, None, None]), 0.0)),
        gam_tbl=f32(np.broadcast_to(np.exp(tb * np.repeat(log_gamma, RET_DK))[:, None],
                                    (RET_QK, RET_WIDTH))),
        gamma=np.exp(log_gamma),
    )


def _block_diag_mask(rows_per, cols_per, nblk):
    r = np.arange(rows_per * nblk)[:, None] // rows_per
    c = np.arange(cols_per * nblk)[None, :] // cols_per
    return (r == c).astype(np.float32)


def _s5_params(log_dt, a_re, a_im, b_re, b_im, c_re, c_im):
    nl = log_dt.shape[0]
    dt = jnp.exp(log_dt)[..., None]
    mag = jnp.exp(dt * a_re)
    ab_re = mag * jnp.cos(dt * a_im)
    ab_im = mag * jnp.sin(dt * a_im)
    den = a_re * a_re + a_im * a_im
    nr = ab_re - 1.0
    g_re = (nr * a_re + ab_im * a_im) / den
    g_im = (ab_im * a_re - nr * a_im) / den
    bb_re = g_re[..., None] * b_re - g_im[..., None] * b_im
    bb_im = g_re[..., None] * b_im + g_im[..., None] * b_re
    eye = jnp.eye(S5_GROUPS, dtype=F32)[None, :, None, :, None]

    def in_blk(bb):
        m = eye * jnp.swapaxes(bb, 2, 3)[:, :, :, None, :]
        return m.reshape(nl, S5_WIDTH, S5_N)

    def out_blk(c):
        m = eye * jnp.swapaxes(c, 2, 3)[:, :, :, None, :]
        return m.reshape(nl, S5_N, S5_WIDTH)

    bblk = jnp.concatenate([in_blk(bb_re), in_blk(bb_im)], axis=2)
    cblk = jnp.concatenate([out_blk(c_re), -out_blk(c_im)], axis=1)
    m = jnp.arange(1, TIME_BLOCK + 1, dtype=F32)[None, :, None, None]
    pmag = jnp.exp(m * (dt * a_re)[:, None])
    pang = m * (dt * a_im)[:, None]
    apow = jnp.concatenate([(pmag * jnp.cos(pang)).reshape(nl, TIME_BLOCK, S5_N),
                            (pmag * jnp.sin(pang)).reshape(nl, TIME_BLOCK, S5_N)], axis=2)
    ab = jnp.concatenate([ab_re.reshape(nl, S5_N, 1), ab_im.reshape(nl, S5_N, 1)], axis=1)
    return bblk, cblk, apow, ab


def kernel(x_prompt, x_sample, state_hgrn, state_ret, state_s5_re, state_s5_im, w_in, hgrn_lb_logits, hgrn_norm_w, ret_norm_w, s5_log_dt, s5_a_re, s5_a_im, s5_b_re, s5_b_im, s5_c_re, s5_c_im, s5_d, s5_glu_w, s5_glu_b, w_out, ln1_w, ln1_b, w_up, w_down, ln2_w, ln2_b):
    bp, tp, _ = x_prompt.shape
    bs = x_sample.shape[0]
    tb = TIME_BLOCK

    lb_prob = jax.nn.softmax(hgrn_lb_logits.astype(F32), axis=0)
    lower_bounds = jnp.cumsum(lb_prob, axis=0) - lb_prob[0:1]

    tbl = _position_tables(tp)
    ret_bd = jnp.asarray(_block_diag_mask(RET_DK, RET_DV, RET_HEADS))
    hg_bd = jnp.asarray(_block_diag_mask(HG_DV, HG_DK, 2))
    ones256 = jnp.asarray(_block_diag_mask(64, 64, 4)).astype(BF16)
    ones128 = jnp.asarray(_block_diag_mask(64, 64, 2)).astype(BF16)
    gam_lanes = jnp.asarray(np.broadcast_to(tbl["gamma"][:, None, None],
                                            (RET_HEADS, 1, bs)).astype(np.float32))

    wp = w_in.astype(BF16)
    wo = w_out.astype(BF16)
    ffn_par = (wo, ln1_w[:, None, :], ln1_b[:, None, :], w_up.astype(BF16), w_down.astype(BF16),
               ln2_w[:, None, :], ln2_b[:, None, :])
    lb_row = lower_bounds[:, None, :]
    hnw_row = jnp.tile(hgrn_norm_w, (1, HG_HEADS))[:, None, :]
    rnw_row = ret_norm_w[:, None, :]
    bblk, cblk, apow, ab_col = _s5_params(s5_log_dt, s5_a_re, s5_a_im, s5_b_re, s5_b_im,
                                          s5_c_re, s5_c_im)
    s5_par = (bblk.astype(BF16), cblk.astype(BF16), apow, apow[:, BLOCK_LEN - 1::BLOCK_LEN],
              s5_d[:, None, :], s5_glu_w.astype(BF16), s5_glu_b[:, None, :])
    s5_step_par = (jnp.swapaxes(bblk, 1, 2), jnp.swapaxes(cblk, 1, 2), ab_col, s5_d[:, :, None],
                   jnp.swapaxes(s5_glu_w, 1, 2).astype(BF16), s5_glu_b[:, :, None])

    hg_s = jnp.transpose(state_hgrn, (0, 2, 3, 4, 1))
    rt_s = jnp.transpose(state_ret, (0, 2, 3, 4, 1))
    re_s = jnp.transpose(state_s5_re, (0, 2, 3, 1)).reshape(DEPTH, S5_N, bs)
    im_s = jnp.transpose(state_s5_im, (0, 2, 3, 1)).reshape(DEPTH, S5_N, bs)
    new_hg_s = jnp.zeros(hg_s.shape, F32)
    new_rt_s = jnp.zeros(rt_s.shape, F32)
    new_re_s = jnp.zeros(re_s.shape, F32)
    new_im_s = jnp.zeros(im_s.shape, F32)

    xp = _permute_tokens(x_prompt).reshape(bp * tp, D_MODEL)
    xs = x_sample.reshape(bs, D_MODEL)
    outs = {k: [] for k in ("hg_p", "rt_p", "re_p", "im_p")}
    for l in range(DEPTH):
        hg_in, rt_in, s5_in = _proj(xp, wp, l, 512)
        o_hg, st_hg = _hgrn(hg_in, lb_row, hnw_row, ones256, ones128, hg_bd, l, bp, tp)
        o_rt, st_rt = _ret(rt_in, tbl["cos_qk"], tbl["sin_qk"], tbl["dq"], tbl["dk"], tbl["dmat"],
                           tbl["gam_tbl"], ret_bd, rnw_row, ones256, ones128, l, bp, tp)
        y5, st_s5 = _s5(s5_in, *s5_par, l, bp, tp)
        xp = _ffn(xp, o_hg, o_rt, y5, *ffn_par, l, 512)
        hg_state = jnp.stack(
            [st_hg[:, h // 2, 64 * (h % 2):64 * (h % 2 + 1), 64 * (h % 2):64 * (h % 2 + 1)]
             for h in range(HG_HEADS)], axis=1)
        outs["hg_p"].append(jnp.swapaxes(hg_state, -1, -2))
        outs["rt_p"].append(jnp.stack(
            [st_rt[:, RET_DK * h:RET_DK * (h + 1), RET_DV * h:RET_DV * (h + 1)]
             for h in range(RET_HEADS)], axis=1))
        outs["re_p"].append(st_s5[:, 0, :S5_N].reshape(bp, S5_GROUPS, S5_STATE))
        outs["im_p"].append(st_s5[:, 0, S5_N:].reshape(bp, S5_GROUPS, S5_STATE))

        hg_in, rt_in, s5_in = _proj(xs, wp, l, bs)
        o_hg_t, new_hg_s = _hgrn_step(l, hg_in, lower_bounds[:, :, None],
                                      hgrn_norm_w[:, :, None], hg_s, new_hg_s)
        o_rt_t, new_rt_s = _ret_step(l, rt_in, tbl["cos_col"], tbl["sin_col"], gam_lanes,
                                     ret_norm_w[:, :, None], rt_s, new_rt_s)
        y5_t, new_re_s, new_im_s = _s5_step(l, s5_in, re_s, im_s, *s5_step_par, new_re_s, new_im_s)
        xs = _ffn(xs, o_hg_t.T, o_rt_t.T, y5_t.T, *ffn_par, l, bs)

    yp = _permute_tokens(xp.reshape(bp, tp, D_MODEL), inverse=True)

    def s5_out(a):
        return jnp.transpose(a.reshape(DEPTH, S5_GROUPS, S5_STATE, bs), (0, 3, 1, 2))

    return (yp, xs.reshape(bs, 1, D_MODEL),
            jnp.stack(outs["hg_p"]), jnp.stack(outs["rt_p"]), jnp.stack(outs["re_p"]),
            jnp.stack(outs["im_p"]),
            jnp.transpose(new_hg_s, (0, 4, 1, 2, 3)), jnp.transpose(new_rt_s, (0, 4, 1, 2, 3)),
            s5_out(new_re_s), s5_out(new_im_s))
```

```python
import jax
import jax.numpy as jnp
import numpy as np
from jax import lax
from jax.experimental import pallas as pl
from jax.experimental.pallas import tpu as pltpu

F32 = jnp.float32
BF16 = jnp.bfloat16
HIGHEST = lax.Precision.HIGHEST

D_MODEL = 1024
DEPTH = 4
PAST_LEN = 16384
HG_WIDTH = 384
HG_HEADS = 6
HG_DK = 64
HG_DV = 64
RET_WIDTH = 384
RET_HEADS = 6
RET_DK = 32
RET_DV = 64
RET_QK = RET_HEADS * RET_DK
S5_WIDTH = 256
S5_GROUP = 16
S5_GROUPS = 16
S5_STATE = 64
S5_N = S5_GROUPS * S5_STATE
D_FF = 4 * D_MODEL
ROPE_BASE = 10000.0
EPS = 1e-5
ALPHA = (2.0 * DEPTH) ** 0.25

HG_COLS = 4 * HG_WIDTH
RT_COLS = 2 * RET_QK + 2 * RET_WIDTH
PROJ_COLS = HG_COLS + RT_COLS + S5_WIDTH
LANES = 128

SUBLANES = 8
TIME_BLOCK = 128
BLOCK_LEN = TIME_BLOCK // SUBLANES
SEQS_PER_STEP = 8
FF_CHUNK = 1024
FFN_ROW_GROUPS = 2
VMEM_LIMIT = 56 * 1024 * 1024


def _dot(a, b):
    return jnp.dot(a, b, preferred_element_type=F32)


def _dot_nt(a, b):
    return lax.dot_general(a, b, (((1,), (1,)), ((), ())), preferred_element_type=F32)


def _dot_tn(a, b):
    return lax.dot_general(a, b, (((0,), (0,)), ((), ())), preferred_element_type=F32)


def _dot_hi(a, b):
    return jnp.dot(a, b, preferred_element_type=F32, precision=HIGHEST)


def _sigmoid(x):
    return jax.nn.sigmoid(x)


def _silu(x):
    return x * jax.nn.sigmoid(x)


def _layer_norm(y, w, b):
    mu = jnp.mean(y, -1, keepdims=True)
    yc = y - mu
    var = jnp.mean(yc * yc, -1, keepdims=True)
    return yc * lax.rsqrt(var + EPS) * w + b


def _head_sum_bf16(x, ones256, ones128):
    return jnp.concatenate([_dot(x[:, 0:256], ones256), _dot(x[:, 256:384], ones128)], axis=1)


def _head_sum(x, ones256, ones128):
    hi = x.astype(BF16)
    lo = (x - hi.astype(F32)).astype(BF16)
    return _head_sum_bf16(hi, ones256, ones128) + _head_sum_bf16(lo, ones256, ones128)


def _proj_kernel(x_ref, w_ref, hg_ref, rt_ref, s5_ref):
    x = x_ref[...].astype(BF16)
    hg_ref[...] = _dot(x, w_ref[:, 0:HG_COLS])
    rt_ref[...] = _dot(x, w_ref[:, HG_COLS:HG_COLS + RT_COLS])
    s5_ref[...] = _dot(x, w_ref[:, HG_COLS + RT_COLS:PROJ_COLS])


def _layer_spec(layer, shape, grid_rank, **kw):
    zeros = (0,) * len(shape)
    if grid_rank == 1:
        return pl.BlockSpec((None,) + shape, lambda i: (layer,) + zeros, **kw)
    return pl.BlockSpec((None,) + shape, lambda b, t: (layer,) + zeros, **kw)


def _proj(x, w, layer, tm):
    m = x.shape[0]
    return pl.pallas_call(
        _proj_kernel,
        grid=(m // tm,),
        in_specs=[pl.BlockSpec((tm, D_MODEL), lambda i: (i, 0)),
                  _layer_spec(layer, (D_MODEL, PROJ_COLS), 1, pipeline_mode=pl.Buffered(1))],
        out_specs=[pl.BlockSpec((tm, HG_COLS), lambda i: (i, 0)),
                   pl.BlockSpec((tm, RT_COLS), lambda i: (i, 0)),
                   pl.BlockSpec((tm, S5_WIDTH), lambda i: (i, 0))],
        out_shape=[jax.ShapeDtypeStruct((m, HG_COLS), F32),
                   jax.ShapeDtypeStruct((m, RT_COLS), F32),
                   jax.ShapeDtypeStruct((m, S5_WIDTH), F32)],
        compiler_params=pltpu.CompilerParams(
            dimension_semantics=("parallel",), vmem_limit_bytes=VMEM_LIMIT),
        name="proj",
    )(x, w)


def _ffn_rows(rows, x_ref, a_ref, b_ref, c_ref, wo_ref, l1w_ref, l1b_ref, wu_ref, wd_ref,
              l2w_ref, l2b_ref, o_ref):
    x = x_ref[rows, :]
    mixed = (_dot(a_ref[rows, :].astype(BF16), wo_ref[0:HG_WIDTH, :])
             + _dot(b_ref[rows, :].astype(BF16), wo_ref[HG_WIDTH:HG_WIDTH + RET_WIDTH, :])
             + _dot(c_ref[rows, :].astype(BF16), wo_ref[HG_WIDTH + RET_WIDTH:D_MODEL, :]))
    yield
    x1 = _layer_norm(ALPHA * x + mixed, l1w_ref[...], l1b_ref[...])
    xb = x1.astype(BF16)
    yield
    ff = None
    for c in range(D_FF // FF_CHUNK):
        h = _dot(xb, wu_ref[:, c * FF_CHUNK:(c + 1) * FF_CHUNK])
        yield
        h = jnp.square(jnp.maximum(h, 0.0)).astype(BF16)
        yield
        d = _dot(h, wd_ref[c * FF_CHUNK:(c + 1) * FF_CHUNK, :])
        ff = d if ff is None else ff + d
        yield
    o_ref[rows, :] = _layer_norm(ALPHA * x1 + ff, l2w_ref[...], l2b_ref[...])


def _ffn_kernel(x_ref, *refs):
    tm = x_ref.shape[0]
    groups = FFN_ROW_GROUPS if tm % (FFN_ROW_GROUPS * SUBLANES) == 0 and tm >= 256 else 1
    step = tm // groups
    live = [_ffn_rows(slice(g * step, (g + 1) * step), x_ref, *refs) for g in range(groups)]
    for g in range(groups):
        for gen in live[:g]:
            next(gen, _DONE)
    while live:
        live = [gen for gen in live if next(gen, _DONE) is not _DONE]


def _ffn(x, a, b, c, wo, l1w, l1b, wu, wd, l2w, l2b, layer, tm):
    m = x.shape[0]
    row = lambda i: (i, 0)
    once = pl.Buffered(1)
    vec = _layer_spec(layer, (1, D_MODEL), 1)
    return pl.pallas_call(
        _ffn_kernel,
        grid=(m // tm,),
        in_specs=[pl.BlockSpec((tm, D_MODEL), row),
                  pl.BlockSpec((tm, HG_WIDTH), row),
                  pl.BlockSpec((tm, RET_WIDTH), row),
                  pl.BlockSpec((tm, S5_WIDTH), row),
                  _layer_spec(layer, (D_MODEL, D_MODEL), 1, pipeline_mode=once),
                  vec, vec,
                  _layer_spec(layer, (D_MODEL, D_FF), 1, pipeline_mode=once),
                  _layer_spec(layer, (D_FF, D_MODEL), 1, pipeline_mode=once),
                  vec, vec],
        out_specs=pl.BlockSpec((tm, D_MODEL), row),
        out_shape=jax.ShapeDtypeStruct((m, D_MODEL), F32),
        compiler_params=pltpu.CompilerParams(
            dimension_semantics=("parallel",), vmem_limit_bytes=VMEM_LIMIT),
        name="ffn",
    )(x, a, b, c, wo, l1w, l1b, wu, wd, l2w, l2b)


_DONE = object()


def _per_sequence(body, n_shared_in):
    def kern(x_ref, *refs):
        shared = refs[:n_shared_in]
        per_seq = refs[n_shared_in:]
        state_out, state_scr = per_seq[1], per_seq[2]
        t = pl.program_id(1)

        @pl.when(t == 0)
        def _():
            state_scr[...] = jnp.zeros(state_scr.shape, F32)

        live = [body(x_ref.at[i], *shared, *(r.at[i] for r in per_seq))
                for i in range(SEQS_PER_STEP)]
        while live:
            live = [g for g in live if next(g, _DONE) is not _DONE]

        @pl.when(t == pl.num_programs(1) - 1)
        def _():
            state_out[...] = state_scr[...]
    return kern


def _seq_major(a):
    return a.reshape(SEQS_PER_STEP, a.shape[0] // SEQS_PER_STEP, a.shape[1])


def _hgrn_body(x_ref, lb_ref, nw_ref, ones256_ref, ones128_ref, bd_ref, o_ref, st_ref,
               s_scr, qt_scr, kt_scr, v_scr, oi_scr):
    del st_ref
    npos = BLOCK_LEN
    sl = SUBLANES
    lb = lb_ref[...]
    ones256 = ones256_ref[...]
    ones128 = ones128_ref[...]
    f, kk, q, v = [], [], [], []
    for p in range(npos):
        rows = slice(sl * p, sl * (p + 1))
        hq = x_ref[rows, 0:HG_WIDTH]
        z = x_ref[rows, HG_WIDTH:2 * HG_WIDTH]
        f.append(lb + (1.0 - lb) * _sigmoid(z))
        kk.append(1.0 - f[-1])
        q.append(_silu(hq))
        v.append(x_ref[rows, 2 * HG_WIDTH:3 * HG_WIDTH])
        if p % 4 == 3:
            yield

    g = list(kk)
    e_rows = []
    for d in range(npos):
        if d > 0:
            g = [None] * d + [f[p] * g[p - 1] for p in range(d, npos)]
        e_rows.extend(q[p] * g[p] for p in range(d, npos))
        if d % 4 == 3:
            yield
    e_all = jnp.concatenate(e_rows, axis=0).astype(BF16)
    p_all = _head_sum_bf16(e_all, ones256, ones128)
    yield
    o = [None] * npos
    i = 0
    for d in range(npos):
        for p in range(d, npos):
            term = p_all[sl * i:sl * (i + 1)] * v[p - d]
            o[p] = term if o[p] is None else o[p] + term
            i += 1
        if d % 4 == 3:
            yield

    a = [f[0]]
    for p in range(1, npos):
        a.append(a[-1] * f[p])
    r = [None] * npos
    r[npos - 1] = jnp.ones_like(f[0])
    for p in range(npos - 2, -1, -1):
        r[p] = r[p + 1] * f[p + 1]
    npair = HG_HEADS // 2
    for p in range(npos):
        rows = slice(sl * p, sl * (p + 1))
        qt = q[p] * a[p]
        kt = kk[p] * r[p]
        for pr in range(npair):
            lanes = slice(128 * pr, 128 * (pr + 1))
            qt_scr[pr, rows, :] = qt[:, lanes]
            kt_scr[pr, rows, :] = kt[:, lanes]
            v_scr[pr, rows, :] = v[p][:, lanes]
    cd = a[npos - 1]
    for j in range(sl):
        blk = pl.ds(j, npos, stride=sl)
        for pr in range(npair):
            lanes = slice(128 * pr, 128 * (pr + 1))
            st = s_scr[pr]
            oi_scr[pr, blk, :] = _dot_nt(qt_scr[pr, blk, :].astype(BF16), st.astype(BF16))
            u = _dot_tn(v_scr[pr, blk, :].astype(BF16), kt_scr[pr, blk, :].astype(BF16))
            s_scr[pr] = st * cd[j:j + 1, lanes] + u * bd_ref[...]
        yield

    ot = (jnp.concatenate(o, axis=0)
          + jnp.concatenate([oi_scr[pr] for pr in range(npair)], axis=1))
    ms = _head_sum(ot * ot, ones256, ones128) * (1.0 / HG_DV)
    gate = x_ref[:, 3 * HG_WIDTH:4 * HG_WIDTH]
    o_ref[...] = ot * lax.rsqrt(ms + EPS) * nw_ref[...] * _silu(gate)


def _hgrn(hg_in, lb, nw, ones256, ones128, bd, layer, bsz, tlen):
    tb = TIME_BLOCK
    nt = tlen // tb
    ns = SEQS_PER_STEP
    npair = HG_HEADS // 2
    fixed2 = lambda b, t: (0, 0)
    tile = lambda b, t: (0, b * nt + t, 0)
    slab = pltpu.VMEM((ns, npair, tb, 128), F32)
    o, st = pl.pallas_call(
        _per_sequence(_hgrn_body, 5),
        grid=(bsz // ns, nt),
        in_specs=[pl.BlockSpec((ns, tb, HG_COLS), tile),
                  _layer_spec(layer, (1, HG_WIDTH), 2),
                  _layer_spec(layer, (1, HG_WIDTH), 2),
                  pl.BlockSpec((256, 256), fixed2),
                  pl.BlockSpec((128, 128), fixed2),
                  pl.BlockSpec((128, 128), fixed2)],
        out_specs=[pl.BlockSpec((ns, tb, HG_WIDTH), tile),
                   pl.BlockSpec((ns, None, npair, 128, 128), lambda b, t: (0, b, 0, 0, 0))],
        out_shape=[jax.ShapeDtypeStruct((ns, bsz // ns * tlen, HG_WIDTH), F32),
                   jax.ShapeDtypeStruct((ns, bsz // ns, npair, 128, 128), F32)],
        scratch_shapes=[pltpu.VMEM((ns, npair, 128, 128), F32), slab, slab, slab, slab],
        compiler_params=pltpu.CompilerParams(
            dimension_semantics=("parallel", "arbitrary"), vmem_limit_bytes=VMEM_LIMIT),
        name="hgrn",
    )(_seq_major(hg_in), lb, nw, ones256, ones128, bd)
    return o.reshape(bsz * tlen, HG_WIDTH), st.reshape(bsz, npair, 128, 128)


def _ret_body(x_ref, cos_ref, sin_ref, dq_ref, dk_ref, dmat_ref, gam_ref, bd_ref, nw_ref,
              ones256_ref, ones128_ref, o_ref, st_ref, s_scr):
    del st_ref
    tb = TIME_BLOCK
    w = RET_QK
    half = RET_DK // 2
    in_first_half = (lax.broadcasted_iota(jnp.int32, (tb, LANES), 1) & (RET_DK - 1)) < half
    rot = []
    for c in range(2 * w // LANES):
        lanes = slice(LANES * c, LANES * (c + 1))
        x = x_ref[:, lanes]
        partner = jnp.where(in_first_half, pltpu.roll(x, LANES - half, 1), pltpu.roll(x, half, 1))
        rot.append(x * cos_ref[:, lanes] + partner * sin_ref[:, lanes])
    rot = jnp.concatenate(rot, axis=1)
    q = rot[:, 0:w]
    k = rot[:, w:2 * w] * (RET_DK ** -0.5)
    v = x_ref[:, 2 * w:2 * w + RET_WIDTH]
    gate = x_ref[:, 2 * w + RET_WIDTH:2 * w + 2 * RET_WIDTH]
    vb = v.astype(BF16)

    lane = lax.broadcasted_iota(jnp.int32, (tb, w), 1)
    qs = jnp.concatenate(
        [jnp.where((lane >= RET_DK * h) & (lane < RET_DK * (h + 1)), q, 0.0)
         for h in range(RET_HEADS)], axis=0).astype(BF16)
    yield
    sc = _dot_nt(qs, k.astype(BF16))
    yield
    lane128 = lax.broadcasted_iota(jnp.int32, (tb, 128), 1)
    outs = []
    for p in range(RET_HEADS // 2):
        p0 = sc[(2 * p) * tb:(2 * p + 1) * tb] * dmat_ref[2 * p]
        p1 = sc[(2 * p + 1) * tb:(2 * p + 2) * tb] * dmat_ref[2 * p + 1]
        pc = jnp.concatenate([p0, p1], axis=1).astype(BF16)
        vp = v[:, 128 * p:128 * (p + 1)]
        vs = jnp.concatenate([jnp.where(lane128 < RET_DV, vp, 0.0),
                              jnp.where(lane128 >= RET_DV, vp, 0.0)], axis=0).astype(BF16)
        outs.append(_dot(pc, vs))
    o = jnp.concatenate(outs, axis=1)
    yield

    s = s_scr[...]
    o = o + _dot((q * dq_ref[...]).astype(BF16), s.astype(BF16))
    u = _dot_tn((k * dk_ref[...]).astype(BF16), vb)
    s_scr[...] = s * gam_ref[...] + u * bd_ref[...]

    ones256 = ones256_ref[...]
    ones128 = ones128_ref[...]
    mu = _head_sum(o, ones256, ones128) * (1.0 / RET_DV)
    oc = o - mu
    var = _head_sum(oc * oc, ones256, ones128) * (1.0 / RET_DV)
    o_ref[...] = oc * lax.rsqrt(var + EPS) * nw_ref[...] * _silu(gate)


def _ret(rt_in, cos, sin, dq, dk, dmat, gam, bd, nw, ones256, ones128, layer, bsz, tlen):
    tb = TIME_BLOCK
    nt = tlen // tb
    ns = SEQS_PER_STEP
    fixed2 = lambda b, t: (0, 0)
    tile = lambda b, t: (0, b * nt + t, 0)
    o, st = pl.pallas_call(
        _per_sequence(_ret_body, 10),
        grid=(bsz // ns, nt),
        in_specs=[pl.BlockSpec((ns, tb, RT_COLS), tile),
                  pl.BlockSpec((tb, 2 * RET_QK), lambda b, t: (t, 0)),
                  pl.BlockSpec((tb, 2 * RET_QK), lambda b, t: (t, 0)),
                  pl.BlockSpec((tb, RET_QK), fixed2),
                  pl.BlockSpec((tb, RET_QK), fixed2),
                  pl.BlockSpec((RET_HEADS, tb, tb), lambda b, t: (0, 0, 0)),
                  pl.BlockSpec((RET_QK, RET_WIDTH), fixed2),
                  pl.BlockSpec((RET_QK, RET_WIDTH), fixed2),
                  _layer_spec(layer, (1, RET_WIDTH), 2),
                  pl.BlockSpec((256, 256), fixed2),
                  pl.BlockSpec((128, 128), fixed2)],
        out_specs=[pl.BlockSpec((ns, tb, RET_WIDTH), tile),
                   pl.BlockSpec((ns, None, RET_QK, RET_WIDTH), lambda b, t: (0, b, 0, 0))],
        out_shape=[jax.ShapeDtypeStruct((ns, bsz // ns * tlen, RET_WIDTH), F32),
                   jax.ShapeDtypeStruct((ns, bsz // ns, RET_QK, RET_WIDTH), F32)],
        scratch_shapes=[pltpu.VMEM((ns, RET_QK, RET_WIDTH), F32)],
        compiler_params=pltpu.CompilerParams(
            dimension_semantics=("parallel", "arbitrary"), vmem_limit_bytes=VMEM_LIMIT),
        name="ret",
    )(_seq_major(rt_in), cos, sin, dq, dk, dmat, gam, bd, nw, ones256, ones128)
    return o.reshape(bsz * tlen, RET_WIDTH), st.reshape(bsz, RET_QK, RET_WIDTH)


def _cmul(ar, ai, xr, xi):
    return ar * xr - ai * xi, ar * xi + ai * xr


def _s5_body(u_ref, bblk_ref, cblk_ref, apow_ref, aend_ref, d_ref, gw_ref, gb_ref, y_ref, st_ref,
             x_scr):
    del st_ref
    npos = BLOCK_LEN
    sl = SUBLANES
    n = S5_N

    def power(m):
        return apow_ref[m - 1:m, 0:n], apow_ref[m - 1:m, n:2 * n]

    u = u_ref[...]
    bu = _dot(u.astype(BF16), bblk_ref[...])
    a1r, a1i = power(1)
    xr = [bu[0:sl, 0:n]]
    xi = [bu[0:sl, n:2 * n]]
    for p in range(1, npos):
        rows = slice(sl * p, sl * (p + 1))
        mr, mi = _cmul(a1r, a1i, xr[-1], xi[-1])
        xr.append(bu[rows, 0:n] + mr)
        xi.append(bu[rows, n:2 * n] + mi)

    yield
    blk = lax.broadcasted_iota(jnp.int32, (sl, n), 0)
    cr, ci = xr[-1], xi[-1]
    s = 1
    while s < sl:
        pr_, pi_ = power(npos * s)
        sr = jnp.where(blk >= s, pltpu.roll(cr, s, 0), 0.0)
        si = jnp.where(blk >= s, pltpu.roll(ci, s, 0), 0.0)
        mr, mi = _cmul(pr_, pi_, sr, si)
        cr, ci = cr + mr, ci + mi
        s *= 2
    x0r = x_scr[:, 0:n]
    x0i = x_scr[:, n:2 * n]
    mr, mi = _cmul(aend_ref[:, 0:n], aend_ref[:, n:2 * n], x0r, x0i)
    cr, ci = cr + mr, ci + mi
    x_scr[:, 0:n] = cr[sl - 1:sl]
    x_scr[:, n:2 * n] = ci[sl - 1:sl]
    inr = jnp.where(blk >= 1, pltpu.roll(cr, 1, 0), x0r)
    ini = jnp.where(blk >= 1, pltpu.roll(ci, 1, 0), x0i)
    rows_out = []
    for p in range(npos):
        pr_, pi_ = power(p + 1)
        mr, mi = _cmul(pr_, pi_, inr, ini)
        rows_out.append(jnp.concatenate([xr[p] + mr, xi[p] + mi], axis=1))
    xc = jnp.concatenate(rows_out, axis=0).astype(BF16)
    yield

    y = _dot(xc, cblk_ref[...]) + d_ref[...] * u
    y = jax.nn.gelu(y, approximate=True)
    y_ref[...] = y * _sigmoid(_dot(y.astype(BF16), gw_ref[...]) + gb_ref[...])


def _s5(u, bblk, cblk, apow, aend, dvec, gw, gb, layer, bsz, tlen):
    tb = TIME_BLOCK
    nt = tlen // tb
    ns = SEQS_PER_STEP
    tile = lambda b, t: (0, b * nt + t, 0)
    y, st = pl.pallas_call(
        _per_sequence(_s5_body, 7),
        grid=(bsz // ns, nt),
        in_specs=[pl.BlockSpec((ns, tb, S5_WIDTH), tile),
                  _layer_spec(layer, (S5_WIDTH, 2 * S5_N), 2),
                  _layer_spec(layer, (2 * S5_N, S5_WIDTH), 2),
                  _layer_spec(layer, (tb, 2 * S5_N), 2),
                  _layer_spec(layer, (SUBLANES, 2 * S5_N), 2),
                  _layer_spec(layer, (1, S5_WIDTH), 2),
                  _layer_spec(layer, (S5_WIDTH, S5_WIDTH), 2),
                  _layer_spec(layer, (1, S5_WIDTH), 2)],
        out_specs=[pl.BlockSpec((ns, tb, S5_WIDTH), tile),
                   pl.BlockSpec((ns, None, 1, 2 * S5_N), lambda b, t: (0, b, 0, 0))],
        out_shape=[jax.ShapeDtypeStruct((ns, bsz // ns * tlen, S5_WIDTH), F32),
                   jax.ShapeDtypeStruct((ns, bsz // ns, 1, 2 * S5_N), F32)],
        scratch_shapes=[pltpu.VMEM((ns, 1, 2 * S5_N), F32)],
        compiler_params=pltpu.CompilerParams(
            dimension_semantics=("parallel", "arbitrary"), vmem_limit_bytes=VMEM_LIMIT),
        name="s5",
    )(_seq_major(u), bblk, cblk, apow, aend, dvec, gw, gb)
    return y.reshape(bsz * tlen, S5_WIDTH), st.reshape(bsz, 1, 2 * S5_N)


def _hgrn_step_kernel(x_ref, lb_ref, nw_ref, s_ref, acc_ref, o_ref, sn_ref,
                      f_scr, k_scr, q_scr, v_scr, g_scr):
    del acc_ref
    h = pl.program_id(0)

    @pl.when(h == 0)
    def _():
        lb = lb_ref[...]
        z = x_ref[:, HG_WIDTH:2 * HG_WIDTH].T
        f_scr[...] = lb + (1.0 - lb) * _sigmoid(z)
        k_scr[...] = (1.0 - lb) * _sigmoid(-z)
        q_scr[...] = _silu(x_ref[:, 0:HG_WIDTH].T)
        v_scr[...] = x_ref[:, 2 * HG_WIDTH:3 * HG_WIDTH].T
        g_scr[...] = x_ref[:, 3 * HG_WIDTH:4 * HG_WIDTH].T

    rows = pl.ds(pl.multiple_of(h * HG_DK, HG_DK), HG_DK)
    f = f_scr[rows, :]
    kk = k_scr[rows, :]
    q = q_scr[rows, :]
    v = v_scr[rows, :]
    o = jnp.zeros_like(v)
    for k in range(HG_DK):
        sn = f[k:k + 1] * s_ref[k] + kk[k:k + 1] * v
        sn_ref[k] = sn
        o = o + q[k:k + 1] * sn
    ms = jnp.mean(o * o, axis=0, keepdims=True)
    o_ref[...] = o * lax.rsqrt(ms + EPS) * nw_ref[...] * _silu(g_scr[rows, :])


def _hgrn_step(layer, hg_in, lb_c, nw_c, s_all, acc):
    n = hg_in.shape[0]
    st = pl.BlockSpec((None, None, HG_DK, HG_DV, n), lambda h: (layer, h, 0, 0, 0))
    chan = pltpu.VMEM((HG_WIDTH, n), F32)
    return pl.pallas_call(
        _hgrn_step_kernel,
        grid=(HG_HEADS,),
        in_specs=[pl.BlockSpec((n, HG_COLS), lambda h: (0, 0)),
                  _layer_spec(layer, (HG_WIDTH, 1), 1),
                  _layer_spec(layer, (HG_DV, 1), 1),
                  st,
                  pl.BlockSpec(memory_space=pl.ANY)],
        out_specs=[pl.BlockSpec((HG_DV, n), lambda h: (h, 0)), st],
        out_shape=[jax.ShapeDtypeStruct((HG_WIDTH, n), F32),
                   jax.ShapeDtypeStruct(s_all.shape, F32)],
        scratch_shapes=[chan, chan, chan, chan, chan],
        input_output_aliases={4: 1},
        compiler_params=pltpu.CompilerParams(
            dimension_semantics=("arbitrary",), vmem_limit_bytes=VMEM_LIMIT),
        name="hgrn_step",
    )(hg_in, lb_c, nw_c, s_all, acc)


def _ret_step_kernel(x_ref, cos_ref, sin_ref, gam_ref, nw_ref, s_ref, acc_ref, o_ref, sn_ref,
                     qk_scr, v_scr, g_scr):
    del acc_ref
    half = RET_DK // 2
    h = pl.program_id(0)

    @pl.when(h == 0)
    def _():
        x = x_ref[:, 0:2 * RET_QK].T
        parts = []
        for i in range(2 * RET_HEADS):
            parts.append(x[RET_DK * i + half:RET_DK * (i + 1)])
            parts.append(x[RET_DK * i:RET_DK * i + half])
        qk_scr[...] = x * cos_ref[...] + jnp.concatenate(parts, axis=0) * sin_ref[...]
        v_scr[...] = x_ref[:, 2 * RET_QK:2 * RET_QK + RET_WIDTH].T
        g_scr[...] = x_ref[:, 2 * RET_QK + RET_WIDTH:2 * RET_QK + 2 * RET_WIDTH].T

    q = qk_scr[pl.ds(pl.multiple_of(h * RET_DK, RET_DK), RET_DK), :]
    kk = qk_scr[pl.ds(pl.multiple_of(RET_QK + h * RET_DK, RET_DK), RET_DK), :] * (RET_DK ** -0.5)
    vrows = pl.ds(pl.multiple_of(h * RET_DV, RET_DV), RET_DV)
    v = v_scr[vrows, :]
    gam = gam_ref[h]
    o = jnp.zeros_like(v)
    for k in range(RET_DK):
        sn = gam * s_ref[k] + kk[k:k + 1] * v
        sn_ref[k] = sn
        o = o + q[k:k + 1] * sn
    mu = jnp.mean(o, axis=0, keepdims=True)
    oc = o - mu
    var = jnp.mean(oc * oc, axis=0, keepdims=True)
    o_ref[...] = oc * lax.rsqrt(var + EPS) * nw_ref[vrows, :] * _silu(g_scr[vrows, :])


def _ret_step(layer, rt_in, cos_c, sin_c, gam, nw_c, s_all, acc):
    n = rt_in.shape[0]
    st = pl.BlockSpec((None, None, RET_DK, RET_DV, n), lambda h: (layer, h, 0, 0, 0))
    par = pl.BlockSpec((2 * RET_QK, 1), lambda h: (0, 0))
    return pl.pallas_call(
        _ret_step_kernel,
        grid=(RET_HEADS,),
        in_specs=[pl.BlockSpec((n, RT_COLS), lambda h: (0, 0)),
                  par, par,
                  pl.BlockSpec((RET_HEADS, 1, n), lambda h: (0, 0, 0)),
                  _layer_spec(layer, (RET_WIDTH, 1), 1),
                  st,
                  pl.BlockSpec(memory_space=pl.ANY)],
        out_specs=[pl.BlockSpec((RET_DV, n), lambda h: (h, 0)), st],
        out_shape=[jax.ShapeDtypeStruct((RET_WIDTH, n), F32),
                   jax.ShapeDtypeStruct(s_all.shape, F32)],
        scratch_shapes=[pltpu.VMEM((2 * RET_QK, n), F32),
                        pltpu.VMEM((RET_WIDTH, n), F32),
                        pltpu.VMEM((RET_WIDTH, n), F32)],
        input_output_aliases={6: 1},
        compiler_params=pltpu.CompilerParams(
            dimension_semantics=("arbitrary",), vmem_limit_bytes=VMEM_LIMIT),
        name="ret_step",
    )(rt_in, cos_c, sin_c, gam, nw_c, s_all, acc)


def _s5_step_kernel(u_ref, xr_ref, xi_ref, bt_ref, ct_ref, ab_ref, d_ref, gwt_ref, gb_ref,
                    accr_ref, acci_ref, y_ref, xrn_ref, xin_ref):
    del accr_ref, acci_ref
    n = S5_N
    ut = u_ref[...].T
    bu = _dot_hi(bt_ref[...], ut)
    mr, mi = _cmul(ab_ref[0:n], ab_ref[n:2 * n], xr_ref[...], xi_ref[...])
    xr = mr + bu[0:n]
    xi = mi + bu[n:2 * n]
    xrn_ref[...] = xr
    xin_ref[...] = xi
    y = _dot_hi(ct_ref[...], jnp.concatenate([xr, xi], axis=0)) + d_ref[...] * ut
    y = jax.nn.gelu(y, approximate=True)
    y_ref[...] = y * _sigmoid(_dot(gwt_ref[...], y.astype(BF16)) + gb_ref[...])


def _s5_step(layer, u, xr_all, xi_all, bt, ct, ab, dcol, gwt, gbcol, accr, acci):
    n = u.shape[0]
    st = _layer_spec(layer, (S5_N, n), 1)
    anyspace = pl.BlockSpec(memory_space=pl.ANY)
    return pl.pallas_call(
        _s5_step_kernel,
        grid=(1,),
        in_specs=[pl.BlockSpec((n, S5_WIDTH), lambda i: (0, 0)), st, st,
                  _layer_spec(layer, (2 * S5_N, S5_WIDTH), 1),
                  _layer_spec(layer, (S5_WIDTH, 2 * S5_N), 1),
                  _layer_spec(layer, (2 * S5_N, 1), 1),
                  _layer_spec(layer, (S5_WIDTH, 1), 1),
                  _layer_spec(layer, (S5_WIDTH, S5_WIDTH), 1),
                  _layer_spec(layer, (S5_WIDTH, 1), 1),
                  anyspace, anyspace],
        out_specs=[pl.BlockSpec((S5_WIDTH, n), lambda i: (0, 0)), st, st],
        out_shape=[jax.ShapeDtypeStruct((S5_WIDTH, n), F32),
                   jax.ShapeDtypeStruct(xr_all.shape, F32),
                   jax.ShapeDtypeStruct(xi_all.shape, F32)],
        input_output_aliases={9: 1, 10: 2},
        compiler_params=pltpu.CompilerParams(
            dimension_semantics=("arbitrary",), vmem_limit_bytes=VMEM_LIMIT),
        name="s5_step",
    )(u, xr_all, xi_all, bt, ct, ab, dcol, gwt, gbcol, accr, acci)


def _tile_order():
    rows = np.arange(TIME_BLOCK)
    return BLOCK_LEN * (rows % SUBLANES) + rows // SUBLANES


def _permute_tokens(x, inverse=False):
    bsz, tlen, dm = x.shape
    a, b = (BLOCK_LEN, SUBLANES) if inverse else (SUBLANES, BLOCK_LEN)
    x = x.reshape(bsz, tlen // TIME_BLOCK, a, b, dm)
    return jnp.swapaxes(x, 2, 3).reshape(bsz, tlen, dm)


def _rope_tables(pos):
    half = RET_DK // 2
    inv = ROPE_BASE ** (-np.arange(half, dtype=np.float64) / half)
    ang = np.asarray(pos, np.float64)[:, None] * inv[None, :]
    cos_h = np.concatenate([np.cos(ang), np.cos(ang)], axis=1)
    sin_h = np.concatenate([-np.sin(ang), np.sin(ang)], axis=1)
    return (np.tile(cos_h, (1, RET_HEADS)).astype(np.float32),
            np.tile(sin_h, (1, RET_HEADS)).astype(np.float32))


def _position_tables(tlen):
    tb = TIME_BLOCK
    order = _tile_order()
    pos_p = np.arange(tlen) // tb * tb + np.tile(order, tlen // tb)
    cos_p, sin_p = _rope_tables(pos_p)
    cos_s, sin_s = _rope_tables(np.array([PAST_LEN]))
    log_gamma = np.log1p(-np.exp2(-5.0 - np.arange(RET_HEADS, dtype=np.float64)))
    lg_lane = np.repeat(log_gamma, RET_DK)[None, :]
    tt = order.astype(np.float64)
    diff = tt[:, None] - tt[None, :]
    f32 = lambda a: jnp.asarray(np.asarray(a, np.float32))
    return dict(
        cos_qk=f32(np.concatenate([cos_p, cos_p], axis=1)),
        sin_qk=f32(np.concatenate([sin_p, sin_p], axis=1)),
        cos_col=f32(np.concatenate([cos_s, cos_s], axis=1).reshape(2 * RET_QK, 1)),
        sin_col=f32(np.concatenate([sin_s, sin_s], axis=1).reshape(2 * RET_QK, 1)),
        dq=f32(np.exp((tt[:, None] + 1.0) * lg_lane)),
        dk=f32(np.exp((tb - 1.0 - tt[:, None]) * lg_lane)),
        dmat=f32(np.where(diff >= 0, np.exp(np.maximum(diff, 0.0)[None] * log_gamma[:, None, None]), 0.0)),
        gam_tbl=f32(np.broadcast_to(np.exp(tb * np.repeat(log_gamma, RET_DK))[:, None],
                                    (RET_QK, RET_WIDTH))),
        gamma=np.exp(log_gamma),
    )


def _block_diag_mask(rows_per, cols_per, nblk):
    r = np.arange(rows_per * nblk)[:, None] // rows_per
    c = np.arange(cols_per * nblk)[None, :] // cols_per
    return (r == c).astype(np.float32)


def _s5_params(log_dt, a_re, a_im, b_re, b_im, c_re, c_im):
    nl = log_dt.shape[0]
    dt = jnp.exp(log_dt)[..., None]
    mag = jnp.exp(dt * a_re)
    ab_re = mag * jnp.cos(dt * a_im)
    ab_im = mag * jnp.sin(dt * a_im)
    den = a_re * a_re + a_im * a_im
    nr = ab_re - 1.0
    g_re = (nr * a_re + ab_im * a_im) / den
    g_im = (ab_im * a_re - nr * a_im) / den
    bb_re = g_re[..., None] * b_re - g_im[..., None] * b_im
    bb_im = g_re[..., None] * b_im + g_im[..., None] * b_re
    eye = jnp.eye(S5_GROUPS, dtype=F32)[None, :, None, :, None]

    def in_blk(bb):
        m = eye * jnp.swapaxes(bb, 2, 3)[:, :, :, None, :]
        return m.reshape(nl, S5_WIDTH, S5_N)

    def out_blk(c):
        m = eye * jnp.swapaxes(c, 2, 3)[:, :, :, None, :]
        return m.reshape(nl, S5_N, S5_WIDTH)

    bblk = jnp.concatenate([in_blk(bb_re), in_blk(bb_im)], axis=2)
    cblk = jnp.concatenate([out_blk(c_re), -out_blk(c_im)], axis=1)
    m = jnp.arange(1, TIME_BLOCK + 1, dtype=F32)[None, :, None, None]
    pmag = jnp.exp(m * (dt * a_re)[:, None])
    pang = m * (dt * a_im)[:, None]
    apow = jnp.concatenate([(pmag * jnp.cos(pang)).reshape(nl, TIME_BLOCK, S5_N),
                            (pmag * jnp.sin(pang)).reshape(nl, TIME_BLOCK, S5_N)], axis=2)
    ab = jnp.concatenate([ab_re.reshape(nl, S5_N, 1), ab_im.reshape(nl, S5_N, 1)], axis=1)
    return bblk, cblk, apow, ab


def kernel(x_prompt, x_sample, state_hgrn, state_ret, state_s5_re, state_s5_im, w_in, hgrn_lb_logits, hgrn_norm_w, ret_norm_w, s5_log_dt, s5_a_re, s5_a_im, s5_b_re, s5_b_im, s5_c_re, s5_c_im, s5_d, s5_glu_w, s5_glu_b, w_out, ln1_w, ln1_b, w_up, w_down, ln2_w, ln2_b):
    bp, tp, _ = x_prompt.shape
    bs = x_sample.shape[0]
    tb = TIME_BLOCK

    lb_prob = jax.nn.softmax(hgrn_lb_logits.astype(F32), axis=0)
    lower_bounds = jnp.cumsum(lb_prob, axis=0) - lb_prob[0:1]

    tbl = _position_tables(tp)
    ret_bd = jnp.asarray(_block_diag_mask(RET_DK, RET_DV, RET_HEADS))
    hg_bd = jnp.asarray(_block_diag_mask(HG_DV, HG_DK, 2))
    ones256 = jnp.asarray(_block_diag_mask(64, 64, 4)).astype(BF16)
    ones128 = jnp.asarray(_block_diag_mask(64, 64, 2)).astype(BF16)
    gam_lanes = jnp.asarray(np.broadcast_to(tbl["gamma"][:, None, None],
                                            (RET_HEADS, 1, bs)).astype(np.float32))

    wp = w_in.astype(BF16)
    wo = w_out.astype(BF16)
    ffn_par = (wo, ln1_w[:, None, :], ln1_b[:, None, :], w_up.astype(BF16), w_down.astype(BF16),
               ln2_w[:, None, :], ln2_b[:, None, :])
    lb_row = lower_bounds[:, None, :]
    hnw_row = jnp.tile(hgrn_norm_w, (1, HG_HEADS))[:, None, :]
    rnw_row = ret_norm_w[:, None, :]
    bblk, cblk, apow, ab_col = _s5_params(s5_log_dt, s5_a_re, s5_a_im, s5_b_re, s5_b_im,
                                          s5_c_re, s5_c_im)
    s5_par = (bblk.astype(BF16), cblk.astype(BF16), apow, apow[:, BLOCK_LEN - 1::BLOCK_LEN],
              s5_d[:, None, :], s5_glu_w.astype(BF16), s5_glu_b[:, None, :])
    s5_step_par = (jnp.swapaxes(bblk, 1, 2), jnp.swapaxes(cblk, 1, 2), ab_col, s5_d[:, :, None],
                   jnp.swapaxes(s5_glu_w, 1, 2).astype(BF16), s5_glu_b[:, :, None])

    hg_s = jnp.transpose(state_hgrn, (0, 2, 3, 4, 1))
    rt_s = jnp.transpose(state_ret, (0, 2, 3, 4, 1))
    re_s = jnp.transpose(state_s5_re, (0, 2, 3, 1)).reshape(DEPTH, S5_N, bs)
    im_s = jnp.transpose(state_s5_im, (0, 2, 3, 1)).reshape(DEPTH, S5_N, bs)
    new_hg_s = jnp.zeros(hg_s.shape, F32)
    new_rt_s = jnp.zeros(rt_s.shape, F32)
    new_re_s = jnp.zeros(re_s.shape, F32)
    new_im_s = jnp.zeros(im_s.shape, F32)

    xp = _permute_tokens(x_prompt).reshape(bp * tp, D_MODEL)
    xs = x_sample.reshape(bs, D_MODEL)
    outs = {k: [] for k in ("hg_p", "rt_p", "re_p", "im_p")}
    for l in range(DEPTH):
        hg_in, rt_in, s5_in = _proj(xp, wp, l, 512)
        o_hg, st_hg = _hgrn(hg_in, lb_row, hnw_row, ones256, ones128, hg_bd, l, bp, tp)
        o_rt, st_rt = _ret(rt_in, tbl["cos_qk"], tbl["sin_qk"], tbl["dq"], tbl["dk"], tbl["dmat"],
                           tbl["gam_tbl"], ret_bd, rnw_row, ones256, ones128, l, bp, tp)
        y5, st_s5 = _s5(s5_in, *s5_par, l, bp, tp)
        xp = _ffn(xp, o_hg, o_rt, y5, *ffn_par, l, 512)
        hg_state = jnp.stack(
            [st_hg[:, h // 2, 64 * (h % 2):64 * (h % 2 + 1), 64 * (h % 2):64 * (h % 2 + 1)]
             for h in range(HG_HEADS)], axis=1)
        outs["hg_p"].append(jnp.swapaxes(hg_state, -1, -2))
        outs["rt_p"].append(jnp.stack(
            [st_rt[:, RET_DK * h:RET_DK * (h + 1), RET_DV * h:RET_DV * (h + 1)]
             for h in range(RET_HEADS)], axis=1))
        outs["re_p"].append(st_s5[:, 0, :S5_N].reshape(bp, S5_GROUPS, S5_STATE))
        outs["im_p"].append(st_s5[:, 0, S5_N:].reshape(bp, S5_GROUPS, S5_STATE))

        hg_in, rt_in, s5_in = _proj(xs, wp, l, bs)
        o_hg_t, new_hg_s = _hgrn_step(l, hg_in, lower_bounds[:, :, None],
                                      hgrn_norm_w[:, :, None], hg_s, new_hg_s)
        o_rt_t, new_rt_s = _ret_step(l, rt_in, tbl["cos_col"], tbl["sin_col"], gam_lanes,
                                     ret_norm_w[:, :, None], rt_s, new_rt_s)
        y5_t, new_re_s, new_im_s = _s5_step(l, s5_in, re_s, im_s, *s5_step_par, new_re_s, new_im_s)
        xs = _ffn(xs, o_hg_t.T, o_rt_t.T, y5_t.T, *ffn_par, l, bs)

    yp = _permute_tokens(xp.reshape(bp, tp, D_MODEL), inverse=True)

    def s5_out(a):
        return jnp.transpose(a.reshape(DEPTH, S5_GROUPS, S5_STATE, bs), (0, 3, 1, 2))

    return (yp, xs.reshape(bs, 1, D_MODEL),
            jnp.stack(outs["hg_p"]), jnp.stack(outs["rt_p"]), jnp.stack(outs["re_p"]),
            jnp.stack(outs["im_p"]),
            jnp.transpose(new_hg_s, (0, 4, 1, 2, 3)), jnp.transpose(new_rt_s, (0, 4, 1, 2, 3)),
            s5_out(new_re_s), s5_out(new_im_s))
```

```python
import jax
import jax.numpy as jnp
import numpy as np
from jax import lax
from jax.experimental import pallas as pl
from jax.experimental.pallas import tpu as pltpu

F32 = jnp.float32
BF16 = jnp.bfloat16

D_MODEL = 1024
DEPTH = 4
PAST_LEN = 16384
HG_WIDTH = 384
HG_HEADS = 6
HG_DK = 64
HG_DV = 64
RET_WIDTH = 384
RET_HEADS = 6
RET_DK = 32
RET_DV = 64
RET_QK = RET_HEADS * RET_DK
S5_WIDTH = 256
S5_GROUP = 16
S5_GROUPS = 16
S5_STATE = 64
S5_N = S5_GROUPS * S5_STATE
D_FF = 4 * D_MODEL
ROPE_BASE = 10000.0
EPS = 1e-5
ALPHA = (2.0 * DEPTH) ** 0.25

HG_COLS = 4 * HG_WIDTH
RT_COLS = 2 * RET_QK + 2 * RET_WIDTH
PROJ_COLS = HG_COLS + RT_COLS + S5_WIDTH
LANES = 128

SUBLANES = 8
TIME_BLOCK = 128
BLOCK_LEN = TIME_BLOCK // SUBLANES
SEQS_PER_STEP = 8
FF_CHUNK = 1024
FFN_ROW_GROUPS = 2
VMEM_LIMIT = 56 * 1024 * 1024


def _dot(a, b):
    return jnp.dot(a, b, preferred_element_type=F32)


def _dot_nt(a, b):
    return lax.dot_general(a, b, (((1,), (1,)), ((), ())), preferred_element_type=F32)


def _dot_tn(a, b):
    return lax.dot_general(a, b, (((0,), (0,)), ((), ())), preferred_element_type=F32)


def _sigmoid(x):
    return jax.nn.sigmoid(x)


def _silu(x):
    return x * jax.nn.sigmoid(x)


def _layer_norm(y, w, b):
    mu = jnp.mean(y, -1, keepdims=True)
    yc = y - mu
    var = jnp.mean(yc * yc, -1, keepdims=True)
    return yc * lax.rsqrt(var + EPS) * w + b


def _head_sum_bf16(x, ones256, ones128):
    return jnp.concatenate([_dot(x[:, 0:256], ones256), _dot(x[:, 256:384], ones128)], axis=1)


def _head_sum(x, ones256, ones128):
    hi = x.astype(BF16)
    lo = (x - hi.astype(F32)).astype(BF16)
    return _head_sum_bf16(hi, ones256, ones128) + _head_sum_bf16(lo, ones256, ones128)


def _proj_kernel(x_ref, w_ref, hg_ref, rt_ref, s5_ref):
    x = x_ref[...].astype(BF16)
    hg_ref[...] = _dot(x, w_ref[:, 0:HG_COLS])
    rt_ref[...] = _dot(x, w_ref[:, HG_COLS:HG_COLS + RT_COLS])
    s5_ref[...] = _dot(x, w_ref[:, HG_COLS + RT_COLS:PROJ_COLS])


def _layer_spec(layer, shape, grid_rank, **kw):
    zeros = (0,) * len(shape)
    if grid_rank == 1:
        return pl.BlockSpec((None,) + shape, lambda i: (layer,) + zeros, **kw)
    return pl.BlockSpec((None,) + shape, lambda b, t: (layer,) + zeros, **kw)


def _proj(x, w, layer, tm):
    m = x.shape[0]
    return pl.pallas_call(
        _proj_kernel,
        grid=(m // tm,),
        in_specs=[pl.BlockSpec((tm, D_MODEL), lambda i: (i, 0)),
                  _layer_spec(layer, (D_MODEL, PROJ_COLS), 1, pipeline_mode=pl.Buffered(1))],
        out_specs=[pl.BlockSpec((tm, HG_COLS), lambda i: (i, 0)),
                   pl.BlockSpec((tm, RT_COLS), lambda i: (i, 0)),
                   pl.BlockSpec((tm, S5_WIDTH), lambda i: (i, 0))],
        out_shape=[jax.ShapeDtypeStruct((m, HG_COLS), F32),
                   jax.ShapeDtypeStruct((m, RT_COLS), F32),
                   jax.ShapeDtypeStruct((m, S5_WIDTH), F32)],
        compiler_params=pltpu.CompilerParams(
            dimension_semantics=("parallel",), vmem_limit_bytes=VMEM_LIMIT),
        name="proj",
    )(x, w)


def _ffn_rows(rows, x_ref, a_ref, b_ref, c_ref, wo_ref, l1w_ref, l1b_ref, wu_ref, wd_ref,
              l2w_ref, l2b_ref, o_ref):
    x = x_ref[rows, :]
    mixed = (_dot(a_ref[rows, :].astype(BF16), wo_ref[0:HG_WIDTH, :])
             + _dot(b_ref[rows, :].astype(BF16), wo_ref[HG_WIDTH:HG_WIDTH + RET_WIDTH, :])
             + _dot(c_ref[rows, :].astype(BF16), wo_ref[HG_WIDTH + RET_WIDTH:D_MODEL, :]))
    yield
    x1 = _layer_norm(ALPHA * x + mixed, l1w_ref[...], l1b_ref[...])
    xb = x1.astype(BF16)
    yield
    ff = None
    for c in range(D_FF // FF_CHUNK):
        h = _dot(xb, wu_ref[:, c * FF_CHUNK:(c + 1) * FF_CHUNK])
        yield
        h = jnp.square(jnp.maximum(h, 0.0)).astype(BF16)
        yield
        d = _dot(h, wd_ref[c * FF_CHUNK:(c + 1) * FF_CHUNK, :])
        ff = d if ff is None else ff + d
        yield
    o_ref[rows, :] = _layer_norm(ALPHA * x1 + ff, l2w_ref[...], l2b_ref[...])


def _ffn_kernel(x_ref, *refs):
    tm = x_ref.shape[0]
    groups = FFN_ROW_GROUPS if tm % (FFN_ROW_GROUPS * SUBLANES) == 0 and tm >= 256 else 1
    step = tm // groups
    live = [_ffn_rows(slice(g * step, (g + 1) * step), x_ref, *refs) for g in range(groups)]
    for g in range(groups):
        for gen in live[:g]:
            next(gen, _DONE)
    while live:
        live = [gen for gen in live if next(gen, _DONE) is not _DONE]


def _ffn(x, a, b, c, wo, l1w, l1b, wu, wd, l2w, l2b, layer, tm):
    m = x.shape[0]
    row = lambda i: (i, 0)
    once = pl.Buffered(1)
    vec = _layer_spec(layer, (1, D_MODEL), 1)
    return pl.pallas_call(
        _ffn_kernel,
        grid=(m // tm,),
        in_specs=[pl.BlockSpec((tm, D_MODEL), row),
                  pl.BlockSpec((tm, HG_WIDTH), row),
                  pl.BlockSpec((tm, RET_WIDTH), row),
                  pl.BlockSpec((tm, S5_WIDTH), row),
                  _layer_spec(layer, (D_MODEL, D_MODEL), 1, pipeline_mode=once),
                  vec, vec,
                  _layer_spec(layer, (D_MODEL, D_FF), 1, pipeline_mode=once),
                  _layer_spec(layer, (D_FF, D_MODEL), 1, pipeline_mode=once),
                  vec, vec],
        out_specs=pl.BlockSpec((tm, D_MODEL), row),
        out_shape=jax.ShapeDtypeStruct((m, D_MODEL), F32),
        compiler_params=pltpu.CompilerParams(
            dimension_semantics=("parallel",), vmem_limit_bytes=VMEM_LIMIT),
        name="ffn",
    )(x, a, b, c, wo, l1w, l1b, wu, wd, l2w, l2b)


_DONE = object()


def _per_sequence(body, n_shared_in):
    def kern(x_ref, *refs):
        shared = refs[:n_shared_in]
        per_seq = refs[n_shared_in:]
        state_out, state_scr = per_seq[1], per_seq[2]
        t = pl.program_id(1)

        @pl.when(t == 0)
        def _():
            state_scr[...] = jnp.zeros(state_scr.shape, F32)

        live = [body(x_ref.at[i], *shared, *(r.at[i] for r in per_seq))
                for i in range(SEQS_PER_STEP)]
        while live:
            live = [g for g in live if next(g, _DONE) is not _DONE]

        @pl.when(t == pl.num_programs(1) - 1)
        def _():
            state_out[...] = state_scr[...]
    return kern


def _seq_major(a):
    return a.reshape(SEQS_PER_STEP, a.shape[0] // SEQS_PER_STEP, a.shape[1])


def _hgrn_body(x_ref, lb_ref, nw_ref, ones256_ref, ones128_ref, bd_ref, o_ref, st_ref,
               s_scr, qt_scr, kt_scr, v_scr, oi_scr):
    del st_ref
    npos = BLOCK_LEN
    sl = SUBLANES
    lb = lb_ref[...]
    ones256 = ones256_ref[...]
    ones128 = ones128_ref[...]
    f, kk, q, v = [], [], [], []
    for p in range(npos):
        rows = slice(sl * p, sl * (p + 1))
        hq = x_ref[rows, 0:HG_WIDTH]
        z = x_ref[rows, HG_WIDTH:2 * HG_WIDTH]
        f.append(lb + (1.0 - lb) * _sigmoid(z))
        kk.append(1.0 - f[-1])
        q.append(_silu(hq))
        v.append(x_ref[rows, 2 * HG_WIDTH:3 * HG_WIDTH])
        if p % 4 == 3:
            yield

    g = list(kk)
    e_rows = []
    for d in range(npos):
        if d > 0:
            g = [None] * d + [f[p] * g[p - 1] for p in range(d, npos)]
        e_rows.extend(q[p] * g[p] for p in range(d, npos))
        if d % 4 == 3:
            yield
    e_all = jnp.concatenate(e_rows, axis=0).astype(BF16)
    p_all = _head_sum_bf16(e_all, ones256, ones128)
    yield
    o = [None] * npos
    i = 0
    for d in range(npos):
        for p in range(d, npos):
            term = p_all[sl * i:sl * (i + 1)] * v[p - d]
            o[p] = term if o[p] is None else o[p] + term
            i += 1
        if d % 4 == 3:
            yield

    a = [f[0]]
    for p in range(1, npos):
        a.append(a[-1] * f[p])
    r = [None] * npos
    r[npos - 1] = jnp.ones_like(f[0])
    for p in range(npos - 2, -1, -1):
        r[p] = r[p + 1] * f[p + 1]
    npair = HG_HEADS // 2
    for p in range(npos):
        rows = slice(sl * p, sl * (p + 1))
        qt = q[p] * a[p]
        kt = kk[p] * r[p]
        for pr in range(npair):
            lanes = slice(128 * pr, 128 * (pr + 1))
            qt_scr[pr, rows, :] = qt[:, lanes]
            kt_scr[pr, rows, :] = kt[:, lanes]
            v_scr[pr, rows, :] = v[p][:, lanes]
    cd = a[npos - 1]
    for j in range(sl):
        blk = pl.ds(j, npos, stride=sl)
        for pr in range(npair):
            lanes = slice(128 * pr, 128 * (pr + 1))
            st = s_scr[pr]
            oi_scr[pr, blk, :] = _dot_nt(qt_scr[pr, blk, :].astype(BF16), st.astype(BF16))
            u = _dot_tn(v_scr[pr, blk, :].astype(BF16), kt_scr[pr, blk, :].astype(BF16))
            s_scr[pr] = st * cd[j:j + 1, lanes] + u * bd_ref[...]
        yield

    ot = (jnp.concatenate(o, axis=0)
          + jnp.concatenate([oi_scr[pr] for pr in range(npair)], axis=1))
    ms = _head_sum(ot * ot, ones256, ones128) * (1.0 / HG_DV)
    gate = x_ref[:, 3 * HG_WIDTH:4 * HG_WIDTH]
    o_ref[...] = ot * lax.rsqrt(ms + EPS) * nw_ref[...] * _silu(gate)


def _hgrn(hg_in, lb, nw, ones256, ones128, bd, layer, bsz, tlen):
    tb = TIME_BLOCK
    nt = tlen // tb
    ns = SEQS_PER_STEP
    npair = HG_HEADS // 2
    fixed2 = lambda b, t: (0, 0)
    tile = lambda b, t: (0, b * nt + t, 0)
    slab = pltpu.VMEM((ns, npair, tb, 128), F32)
    o, st = pl.pallas_call(
        _per_sequence(_hgrn_body, 5),
        grid=(bsz // ns, nt),
        in_specs=[pl.BlockSpec((ns, tb, HG_COLS), tile),
                  _layer_spec(layer, (1, HG_WIDTH), 2),
                  _layer_spec(layer, (1, HG_WIDTH), 2),
                  pl.BlockSpec((256, 256), fixed2),
                  pl.BlockSpec((128, 128), fixed2),
                  pl.BlockSpec((128, 128), fixed2)],
        out_specs=[pl.BlockSpec((ns, tb, HG_WIDTH), tile),
                   pl.BlockSpec((ns, None, npair, 128, 128), lambda b, t: (0, b, 0, 0, 0))],
        out_shape=[jax.ShapeDtypeStruct((ns, bsz // ns * tlen, HG_WIDTH), F32),
                   jax.ShapeDtypeStruct((ns, bsz // ns, npair, 128, 128), F32)],
        scratch_shapes=[pltpu.VMEM((ns, npair, 128, 128), F32), slab, slab, slab, slab],
        compiler_params=pltpu.CompilerParams(
            dimension_semantics=("parallel", "arbitrary"), vmem_limit_bytes=VMEM_LIMIT),
        name="hgrn",
    )(_seq_major(hg_in), lb, nw, ones256, ones128, bd)
    return o.reshape(bsz * tlen, HG_WIDTH), st.reshape(bsz, npair, 128, 128)


def _ret_body(x_ref, cos_ref, sin_ref, dq_ref, dk_ref, dmat_ref, gam_ref, bd_ref, nw_ref,
              ones256_ref, ones128_ref, o_ref, st_ref, s_scr):
    del st_ref
    tb = TIME_BLOCK
    w = RET_QK
    half = RET_DK // 2
    in_first_half = (lax.broadcasted_iota(jnp.int32, (tb, LANES), 1) & (RET_DK - 1)) < half
    rot = []
    for c in range(2 * w // LANES):
        lanes = slice(LANES * c, LANES * (c + 1))
        x = x_ref[:, lanes]
        partner = jnp.where(in_first_half, pltpu.roll(x, LANES - half, 1), pltpu.roll(x, half, 1))
        rot.append(x * cos_ref[:, lanes] + partner * sin_ref[:, lanes])
    rot = jnp.concatenate(rot, axis=1)
    q = rot[:, 0:w]
    k = rot[:, w:2 * w] * (RET_DK ** -0.5)
    v = x_ref[:, 2 * w:2 * w + RET_WIDTH]
    gate = x_ref[:, 2 * w + RET_WIDTH:2 * w + 2 * RET_WIDTH]
    vb = v.astype(BF16)

    lane = lax.broadcasted_iota(jnp.int32, (tb, w), 1)
    qs = jnp.concatenate(
        [jnp.where((lane >= RET_DK * h) & (lane < RET_DK * (h + 1)), q, 0.0)
         for h in range(RET_HEADS)], axis=0).astype(BF16)
    yield
    sc = _dot_nt(qs, k.astype(BF16))
    yield
    lane128 = lax.broadcasted_iota(jnp.int32, (tb, 128), 1)
    outs = []
    for p in range(RET_HEADS // 2):
        p0 = sc[(2 * p) * tb:(2 * p + 1) * tb] * dmat_ref[2 * p]
        p1 = sc[(2 * p + 1) * tb:(2 * p + 2) * tb] * dmat_ref[2 * p + 1]
        pc = jnp.concatenate([p0, p1], axis=1).astype(BF16)
        vp = v[:, 128 * p:128 * (p + 1)]
        vs = jnp.concatenate([jnp.where(lane128 < RET_DV, vp, 0.0),
                              jnp.where(lane128 >= RET_DV, vp, 0.0)], axis=0).astype(BF16)
        outs.append(_dot(pc, vs))
    o = jnp.concatenate(outs, axis=1)
    yield

    s = s_scr[...]
    o = o + _dot((q * dq_ref[...]).astype(BF16), s.astype(BF16))
    u = _dot_tn((k * dk_ref[...]).astype(BF16), vb)
    s_scr[...] = s * gam_ref[...] + u * bd_ref[...]

    ones256 = ones256_ref[...]
    ones128 = ones128_ref[...]
    mu = _head_sum(o, ones256, ones128) * (1.0 / RET_DV)
    oc = o - mu
    var = _head_sum(oc * oc, ones256, ones128) * (1.0 / RET_DV)
    o_ref[...] = oc * lax.rsqrt(var + EPS) * nw_ref[...] * _silu(gate)


def _ret(rt_in, cos, sin, dq, dk, dmat, gam, bd, nw, ones256, ones128, layer, bsz, tlen):
    tb = TIME_BLOCK
    nt = tlen // tb
    ns = SEQS_PER_STEP
    fixed2 = lambda b, t: (0, 0)
    tile = lambda b, t: (0, b * nt + t, 0)
    o, st = pl.pallas_call(
        _per_sequence(_ret_body, 10),
        grid=(bsz // ns, nt),
        in_specs=[pl.BlockSpec((ns, tb, RT_COLS), tile),
                  pl.BlockSpec((tb, 2 * RET_QK), lambda b, t: (t, 0)),
                  pl.BlockSpec((tb, 2 * RET_QK), lambda b, t: (t, 0)),
                  pl.BlockSpec((tb, RET_QK), fixed2),
                  pl.BlockSpec((tb, RET_QK), fixed2),
                  pl.BlockSpec((RET_HEADS, tb, tb), lambda b, t: (0, 0, 0)),
                  pl.BlockSpec((RET_QK, RET_WIDTH), fixed2),
                  pl.BlockSpec((RET_QK, RET_WIDTH), fixed2),
                  _layer_spec(layer, (1, RET_WIDTH), 2),
                  pl.BlockSpec((256, 256), fixed2),
                  pl.BlockSpec((128, 128), fixed2)],
        out_specs=[pl.BlockSpec((ns, tb, RET_WIDTH), tile),
                   pl.BlockSpec((ns, None, RET_QK, RET_WIDTH), lambda b, t: (0, b, 0, 0))],
        out_shape=[jax.ShapeDtypeStruct((ns, bsz // ns * tlen, RET_WIDTH), F32),
                   jax.ShapeDtypeStruct((ns, bsz // ns, RET_QK, RET_WIDTH), F32)],
        scratch_shapes=[pltpu.VMEM((ns, RET_QK, RET_WIDTH), F32)],
        compiler_params=pltpu.CompilerParams(
            dimension_semantics=("parallel", "arbitrary"), vmem_limit_bytes=VMEM_LIMIT),
        name="ret",
    )(_seq_major(rt_in), cos, sin, dq, dk, dmat, gam, bd, nw, ones256, ones128)
    return o.reshape(bsz * tlen, RET_WIDTH), st.reshape(bsz, RET_QK, RET_WIDTH)


def _cmul(ar, ai, xr, xi):
    return ar * xr - ai * xi, ar * xi + ai * xr


def _s5_body(u_ref, bblk_ref, cblk_ref, apow_ref, aend_ref, d_ref, gw_ref, gb_ref, y_ref, st_ref,
             x_scr):
    del st_ref
    npos = BLOCK_LEN
    sl = SUBLANES
    n = S5_N

    def power(m):
        return apow_ref[m - 1:m, 0:n], apow_ref[m - 1:m, n:2 * n]

    u = u_ref[...]
    bu = _dot(u.astype(BF16), bblk_ref[...])
    a1r, a1i = power(1)
    xr = [bu[0:sl, 0:n]]
    xi = [bu[0:sl, n:2 * n]]
    for p in range(1, npos):
        rows = slice(sl * p, sl * (p + 1))
        mr, mi = _cmul(a1r, a1i, xr[-1], xi[-1])
        xr.append(bu[rows, 0:n] + mr)
        xi.append(bu[rows, n:2 * n] + mi)

    yield
    blk = lax.broadcasted_iota(jnp.int32, (sl, n), 0)
    cr, ci = xr[-1], xi[-1]
    s = 1
    while s < sl:
        pr_, pi_ = power(npos * s)
        sr = jnp.where(blk >= s, pltpu.roll(cr, s, 0), 0.0)
        si = jnp.where(blk >= s, pltpu.roll(ci, s, 0), 0.0)
        mr, mi = _cmul(pr_, pi_, sr, si)
        cr, ci = cr + mr, ci + mi
        s *= 2
    x0r = x_scr[:, 0:n]
    x0i = x_scr[:, n:2 * n]
    mr, mi = _cmul(aend_ref[:, 0:n], aend_ref[:, n:2 * n], x0r, x0i)
    cr, ci = cr + mr, ci + mi
    x_scr[:, 0:n] = cr[sl - 1:sl]
    x_scr[:, n:2 * n] = ci[sl - 1:sl]
    inr = jnp.where(blk >= 1, pltpu.roll(cr, 1, 0), x0r)
    ini = jnp.where(blk >= 1, pltpu.roll(ci, 1, 0), x0i)
    rows_out = []
    for p in range(npos):
        pr_, pi_ = power(p + 1)
        mr, mi = _cmul(pr_, pi_, inr, ini)
        rows_out.append(jnp.concatenate([xr[p] + mr, xi[p] + mi], axis=1))
    xc = jnp.concatenate(rows_out, axis=0).astype(BF16)
    yield

    y = _dot(xc, cblk_ref[...]) + d_ref[...] * u
    y = jax.nn.gelu(y, approximate=True)
    y_ref[...] = y * _sigmoid(_dot(y.astype(BF16), gw_ref[...]) + gb_ref[...])


def _s5(u, bblk, cblk, apow, aend, dvec, gw, gb, layer, bsz, tlen):
    tb = TIME_BLOCK
    nt = tlen // tb
    ns = SEQS_PER_STEP
    tile = lambda b, t: (0, b * nt + t, 0)
    y, st = pl.pallas_call(
        _per_sequence(_s5_body, 7),
        grid=(bsz // ns, nt),
        in_specs=[pl.BlockSpec((ns, tb, S5_WIDTH), tile),
                  _layer_spec(layer, (S5_WIDTH, 2 * S5_N), 2),
                  _layer_spec(layer, (2 * S5_N, S5_WIDTH), 2),
                  _layer_spec(layer, (tb, 2 * S5_N), 2),
                  _layer_spec(layer, (SUBLANES, 2 * S5_N), 2),
                  _layer_spec(layer, (1, S5_WIDTH), 2),
                  _layer_spec(layer, (S5_WIDTH, S5_WIDTH), 2),
                  _layer_spec(layer, (1, S5_WIDTH), 2)],
        out_specs=[pl.BlockSpec((ns, tb, S5_WIDTH), tile),
                   pl.BlockSpec((ns, None, 1, 2 * S5_N), lambda b, t: (0, b, 0, 0))],
        out_shape=[jax.ShapeDtypeStruct((ns, bsz // ns * tlen, S5_WIDTH), F32),
                   jax.ShapeDtypeStruct((ns, bsz // ns, 1, 2 * S5_N), F32)],
        scratch_shapes=[pltpu.VMEM((ns, 1, 2 * S5_N), F32)],
        compiler_params=pltpu.CompilerParams(
            dimension_semantics=("parallel", "arbitrary"), vmem_limit_bytes=VMEM_LIMIT),
        name="s5",
    )(_seq_major(u), bblk, cblk, apow, aend, dvec, gw, gb)
    return y.reshape(bsz * tlen, S5_WIDTH), st.reshape(bsz, 1, 2 * S5_N)


def _hgrn_step_kernel(x_ref, lb_ref, nw_ref, s_ref, acc_ref, o_ref, sn_ref,
                      f_scr, k_scr, q_scr, v_scr, g_scr):
    del acc_ref
    h = pl.program_id(0)

    @pl.when(h == 0)
    def _():
        lb = lb_ref[...]
        z = x_ref[:, HG_WIDTH:2 * HG_WIDTH].T
        f_scr[...] = lb + (1.0 - lb) * _sigmoid(z)
        k_scr[...] = (1.0 - lb) * _sigmoid(-z)
        q_scr[...] = _silu(x_ref[:, 0:HG_WIDTH].T)
        v_scr[...] = x_ref[:, 2 * HG_WIDTH:3 * HG_WIDTH].T
        g_scr[...] = x_ref[:, 3 * HG_WIDTH:4 * HG_WIDTH].T

    rows = pl.ds(pl.multiple_of(h * HG_DK, HG_DK), HG_DK)
    f = f_scr[rows, :]
    kk = k_scr[rows, :]
    q = q_scr[rows, :]
    v = v_scr[rows, :]
    o = jnp.zeros_like(v)
    for k in range(HG_DK):
        sn = f[k:k + 1] * s_ref[k] + kk[k:k + 1] * v
        sn_ref[k] = sn
        o = o + q[k:k + 1] * sn
    ms = jnp.mean(o * o, axis=0, keepdims=True)
    o_ref[...] = o * lax.rsqrt(ms + EPS) * nw_ref[...] * _silu(g_scr[rows, :])


def _hgrn_step(layer, hg_in, lb_c, nw_c, s_all, acc):
    n = hg_in.shape[0]
    st = pl.BlockSpec((None, None, HG_DK, HG_DV, n), lambda h: (layer, h, 0, 0, 0))
    chan = pltpu.VMEM((HG_WIDTH, n), F32)
    return pl.pallas_call(
        _hgrn_step_kernel,
        grid=(HG_HEADS,),
        in_specs=[pl.BlockSpec((n, HG_COLS), lambda h: (0, 0)),
                  _layer_spec(layer, (HG_WIDTH, 1), 1),
                  _layer_spec(layer, (HG_DV, 1), 1),
                  st,
                  pl.BlockSpec(memory_space=pl.ANY)],
        out_specs=[pl.BlockSpec((HG_DV, n), lambda h: (h, 0)), st],
        out_shape=[jax.ShapeDtypeStruct((HG_WIDTH, n), F32),
                   jax.ShapeDtypeStruct(s_all.shape, F32)],
        scratch_shapes=[chan, chan, chan, chan, chan],
        input_output_aliases={4: 1},
        compiler_params=pltpu.CompilerParams(
            dimension_semantics=("arbitrary",), vmem_limit_bytes=VMEM_LIMIT),
        name="hgrn_step",
    )(hg_in, lb_c, nw_c, s_all, acc)


def _ret_step_kernel(x_ref, cos_ref, sin_ref, gam_ref, nw_ref, s_ref, acc_ref, o_ref, sn_ref,
                     qk_scr, v_scr, g_scr):
    del acc_ref
    half = RET_DK // 2
    h = pl.program_id(0)

    @pl.when(h == 0)
    def _():
        x = x_ref[:, 0:2 * RET_QK].T
        parts = []
        for i in range(2 * RET_HEADS):
            parts.append(x[RET_DK * i + half:RET_DK * (i + 1)])
            parts.append(x[RET_DK * i:RET_DK * i + half])
        qk_scr[...] = x * cos_ref[...] + jnp.concatenate(parts, axis=0) * sin_ref[...]
        v_scr[...] = x_ref[:, 2 * RET_QK:2 * RET_QK + RET_WIDTH].T
        g_scr[...] = x_ref[:, 2 * RET_QK + RET_WIDTH:2 * RET_QK + 2 * RET_WIDTH].T

    q = qk_scr[pl.ds(pl.multiple_of(h * RET_DK, RET_DK), RET_DK), :]
    kk = qk_scr[pl.ds(pl.multiple_of(RET_QK + h * RET_DK, RET_DK), RET_DK), :] * (RET_DK ** -0.5)
    vrows = pl.ds(pl.multiple_of(h * RET_DV, RET_DV), RET_DV)
    v = v_scr[vrows, :]
    gam = gam_ref[h]
    o = jnp.zeros_like(v)
    for k in range(RET_DK):
        sn = gam * s_ref[k] + kk[k:k + 1] * v
        sn_ref[k] = sn
        o = o + q[k:k + 1] * sn
    mu = jnp.mean(o, axis=0, keepdims=True)
    oc = o - mu
    var = jnp.mean(oc * oc, axis=0, keepdims=True)
    o_ref[...] = oc * lax.rsqrt(var + EPS) * nw_ref[vrows, :] * _silu(g_scr[vrows, :])


def _ret_step(layer, rt_in, cos_c, sin_c, gam, nw_c, s_all, acc):
    n = rt_in.shape[0]
    st = pl.BlockSpec((None, None, RET_DK, RET_DV, n), lambda h: (layer, h, 0, 0, 0))
    par = pl.BlockSpec((2 * RET_QK, 1), lambda h: (0, 0))
    return pl.pallas_call(
        _ret_step_kernel,
        grid=(RET_HEADS,),
        in_specs=[pl.BlockSpec((n, RT_COLS), lambda h: (0, 0)),
                  par, par,
                  pl.BlockSpec((RET_HEADS, 1, n), lambda h: (0, 0, 0)),
                  _layer_spec(layer, (RET_WIDTH, 1), 1),
                  st,
                  pl.BlockSpec(memory_space=pl.ANY)],
        out_specs=[pl.BlockSpec((RET_DV, n), lambda h: (h, 0)), st],
        out_shape=[jax.ShapeDtypeStruct((RET_WIDTH, n), F32),
                   jax.ShapeDtypeStruct(s_all.shape, F32)],
        scratch_shapes=[pltpu.VMEM((2 * RET_QK, n), F32),
                        pltpu.VMEM((RET_WIDTH, n), F32),
                        pltpu.VMEM((RET_WIDTH, n), F32)],
        input_output_aliases={6: 1},
        compiler_params=pltpu.CompilerParams(
            dimension_semantics=("arbitrary",), vmem_limit_bytes=VMEM_LIMIT),
        name="ret_step",
    )(rt_in, cos_c, sin_c, gam, nw_c, s_all, acc)


def _s5_step_kernel(u_ref, xr_ref, xi_ref, bt_ref, ct_ref, ab_ref, d_ref, gwt_ref, gb_ref,
                    accr_ref, acci_ref, y_ref, xrn_ref, xin_ref):
    del accr_ref, acci_ref
    n = S5_N
    ut = u_ref[...].T
    bu = _dot(bt_ref[...], ut.astype(BF16))
    mr, mi = _cmul(ab_ref[0:n], ab_ref[n:2 * n], xr_ref[...], xi_ref[...])
    xr = mr + bu[0:n]
    xi = mi + bu[n:2 * n]
    xrn_ref[...] = xr
    xin_ref[...] = xi
    y = _dot(ct_ref[...], jnp.concatenate([xr, xi], axis=0).astype(BF16)) + d_ref[...] * ut
    y = jax.nn.gelu(y, approximate=True)
    y_ref[...] = y * _sigmoid(_dot(gwt_ref[...], y.astype(BF16)) + gb_ref[...])


def _s5_step(layer, u, xr_all, xi_all, bt, ct, ab, dcol, gwt, gbcol, accr, acci):
    n = u.shape[0]
    st = _layer_spec(layer, (S5_N, n), 1)
    anyspace = pl.BlockSpec(memory_space=pl.ANY)
    return pl.pallas_call(
        _s5_step_kernel,
        grid=(1,),
        in_specs=[pl.BlockSpec((n, S5_WIDTH), lambda i: (0, 0)), st, st,
                  _layer_spec(layer, (2 * S5_N, S5_WIDTH), 1),
                  _layer_spec(layer, (S5_WIDTH, 2 * S5_N), 1),
                  _layer_spec(layer, (2 * S5_N, 1), 1),
                  _layer_spec(layer, (S5_WIDTH, 1), 1),
                  _layer_spec(layer, (S5_WIDTH, S5_WIDTH), 1),
                  _layer_spec(layer, (S5_WIDTH, 1), 1),
                  anyspace, anyspace],
        out_specs=[pl.BlockSpec((S5_WIDTH, n), lambda i: (0, 0)), st, st],
        out_shape=[jax.ShapeDtypeStruct((S5_WIDTH, n), F32),
                   jax.ShapeDtypeStruct(xr_all.shape, F32),
                   jax.ShapeDtypeStruct(xi_all.shape, F32)],
        input_output_aliases={9: 1, 10: 2},
        compiler_params=pltpu.CompilerParams(
            dimension_semantics=("arbitrary",), vmem_limit_bytes=VMEM_LIMIT),
        name="s5_step",
    )(u, xr_all, xi_all, bt, ct, ab, dcol, gwt, gbcol, accr, acci)


def _tile_order():
    rows = np.arange(TIME_BLOCK)
    return BLOCK_LEN * (rows % SUBLANES) + rows // SUBLANES


def _permute_tokens(x, inverse=False):
    bsz, tlen, dm = x.shape
    a, b = (BLOCK_LEN, SUBLANES) if inverse else (SUBLANES, BLOCK_LEN)
    x = x.reshape(bsz, tlen // TIME_BLOCK, a, b, dm)
    return jnp.swapaxes(x, 2, 3).reshape(bsz, tlen, dm)


def _rope_tables(pos):
    half = RET_DK // 2
    inv = ROPE_BASE ** (-np.arange(half, dtype=np.float64) / half)
    ang = np.asarray(pos, np.float64)[:, None] * inv[None, :]
    cos_h = np.concatenate([np.cos(ang), np.cos(ang)], axis=1)
    sin_h = np.concatenate([-np.sin(ang), np.sin(ang)], axis=1)
    return (np.tile(cos_h, (1, RET_HEADS)).astype(np.float32),
            np.tile(sin_h, (1, RET_HEADS)).astype(np.float32))


def _position_tables(tlen):
    tb = TIME_BLOCK
    order = _tile_order()
    pos_p = np.arange(tlen) // tb * tb + np.tile(order, tlen // tb)
    cos_p, sin_p = _rope_tables(pos_p)
    cos_s, sin_s = _rope_tables(np.array([PAST_LEN]))
    log_gamma = np.log1p(-np.exp2(-5.0 - np.arange(RET_HEADS, dtype=np.float64)))
    lg_lane = np.repeat(log_gamma, RET_DK)[None, :]
    tt = order.astype(np.float64)
    diff = tt[:, None] - tt[None, :]
    f32 = lambda a: jnp.asarray(np.asarray(a, np.float32))
    return dict(
        cos_qk=f32(np.concatenate([cos_p, cos_p], axis=1)),
        sin_qk=f32(np.concatenate([sin_p, sin_p], axis=1)),
        cos_col=f32(np.concatenate([cos_s, cos_s], axis=1).reshape(2 * RET_QK, 1)),
        sin_col=f32(np.concatenate([sin_s, sin_s], axis=1).reshape(2 * RET_QK, 1)),
        dq=f32(np.exp((tt[:, None] + 1.0) * lg_lane)),
        dk=f32(np.exp((tb - 1.0 - tt[:, None]) * lg_lane)),
        dmat=f32(np.where(diff >= 0, np.exp(np.maximum(diff, 0.0)[None] * log_gamma[:, None, None]), 0.0)),
        gam_tbl=f32(np.broadcast_to(np.exp(tb * np.repeat(log_gamma, RET_DK))[:, None],
                                    (RET_QK, RET_WIDTH))),
        gamma=np.exp(log_gamma),
    )


def _block_diag_mask(rows_per, cols_per, nblk):
    r = np.arange(rows_per * nblk)[:, None] // rows_per
    c = np.arange(cols_per * nblk)[None, :] // cols_per
    return (r == c).astype(np.float32)


def _s5_params(log_dt, a_re, a_im, b_re, b_im, c_re, c_im):
    nl = log_dt.shape[0]
    dt = jnp.exp(log_dt)[..., None]
    mag = jnp.exp(dt * a_re)
    ab_re = mag * jnp.cos(dt * a_im)
    ab_im = mag * jnp.sin(dt * a_im)
    den = a_re * a_re + a_im * a_im
    nr = ab_re - 1.0
    g_re = (nr * a_re + ab_im * a_im) / den
    g_im = (ab_im * a_re - nr * a_im) / den
    bb_re = g_re[..., None] * b_re - g_im[..., None] * b_im
    bb_im = g_re[..., None] * b_im + g_im[..., None] * b_re
    eye = jnp.eye(S5_GROUPS, dtype=F32)[None, :, None, :, None]

    def in_blk(bb):
        m = eye * jnp.swapaxes(bb, 2, 3)[:, :, :, None, :]
        return m.reshape(nl, S5_WIDTH, S5_N)

    def out_blk(c):
        m = eye * jnp.swapaxes(c, 2, 3)[:, :, :, None, :]
        return m.reshape(nl, S5_N, S5_WIDTH)

    bblk = jnp.concatenate([in_blk(bb_re), in_blk(bb_im)], axis=2)
    cblk = jnp.concatenate([out_blk(c_re), -out_blk(c_im)], axis=1)
    m = jnp.arange(1, TIME_BLOCK + 1, dtype=F32)[None, :, None, None]
    pmag = jnp.exp(m * (dt * a_re)[:, None])
    pang = m * (dt * a_im)[:, None]
    apow = jnp.concatenate([(pmag * jnp.cos(pang)).reshape(nl, TIME_BLOCK, S5_N),
                            (pmag * jnp.sin(pang)).reshape(nl, TIME_BLOCK, S5_N)], axis=2)
    ab = jnp.concatenate([ab_re.reshape(nl, S5_N, 1), ab_im.reshape(nl, S5_N, 1)], axis=1)
    return bblk, cblk, apow, ab


def kernel(x_prompt, x_sample, state_hgrn, state_ret, state_s5_re, state_s5_im, w_in, hgrn_lb_logits, hgrn_norm_w, ret_norm_w, s5_log_dt, s5_a_re, s5_a_im, s5_b_re, s5_b_im, s5_c_re, s5_c_im, s5_d, s5_glu_w, s5_glu_b, w_out, ln1_w, ln1_b, w_up, w_down, ln2_w, ln2_b):
    bp, tp, _ = x_prompt.shape
    bs = x_sample.shape[0]
    tb = TIME_BLOCK

    lb_prob = jax.nn.softmax(hgrn_lb_logits.astype(F32), axis=0)
    lower_bounds = jnp.cumsum(lb_prob, axis=0) - lb_prob[0:1]

    tbl = _position_tables(tp)
    ret_bd = jnp.asarray(_block_diag_mask(RET_DK, RET_DV, RET_HEADS))
    hg_bd = jnp.asarray(_block_diag_mask(HG_DV, HG_DK, 2))
    ones256 = jnp.asarray(_block_diag_mask(64, 64, 4)).astype(BF16)
    ones128 = jnp.asarray(_block_diag_mask(64, 64, 2)).astype(BF16)
    gam_lanes = jnp.asarray(np.broadcast_to(tbl["gamma"][:, None, None],
                                            (RET_HEADS, 1, bs)).astype(np.float32))

    wp = w_in.astype(BF16)
    wo = w_out.astype(BF16)
    ffn_par = (wo, ln1_w[:, None, :], ln1_b[:, None, :], w_up.astype(BF16), w_down.astype(BF16),
               ln2_w[:, None, :], ln2_b[:, None, :])
    lb_row = lower_bounds[:, None, :]
    hnw_row = jnp.tile(hgrn_norm_w, (1, HG_HEADS))[:, None, :]
    rnw_row = ret_norm_w[:, None, :]
    bblk, cblk, apow, ab_col = _s5_params(s5_log_dt, s5_a_re, s5_a_im, s5_b_re, s5_b_im,
                                          s5_c_re, s5_c_im)
    s5_par = (bblk.astype(BF16), cblk.astype(BF16), apow, apow[:, BLOCK_LEN - 1::BLOCK_LEN],
              s5_d[:, None, :], s5_glu_w.astype(BF16), s5_glu_b[:, None, :])
    s5_step_par = (jnp.swapaxes(s5_par[0], 1, 2), jnp.swapaxes(s5_par[1], 1, 2), ab_col,
                   s5_d[:, :, None],
                   jnp.swapaxes(s5_glu_w, 1, 2).astype(BF16), s5_glu_b[:, :, None])

    hg_s = jnp.transpose(state_hgrn, (0, 2, 3, 4, 1))
    rt_s = jnp.transpose(state_ret, (0, 2, 3, 4, 1))
    re_s = jnp.transpose(state_s5_re, (0, 2, 3, 1)).reshape(DEPTH, S5_N, bs)
    im_s = jnp.transpose(state_s5_im, (0, 2, 3, 1)).reshape(DEPTH, S5_N, bs)
    new_hg_s = jnp.zeros(hg_s.shape, F32)
    new_rt_s = jnp.zeros(rt_s.shape, F32)
    new_re_s = jnp.zeros(re_s.shape, F32)
    new_im_s = jnp.zeros(im_s.shape, F32)

    xp = _permute_tokens(x_prompt).reshape(bp * tp, D_MODEL)
    xs = x_sample.reshape(bs, D_MODEL)
    outs = {k: [] for k in ("hg_p", "rt_p", "s5_p")}
    for l in range(DEPTH):
        hg_in, rt_in, s5_in = _proj(xp, wp, l, 1024)
        o_hg, st_hg = _hgrn(hg_in, lb_row, hnw_row, ones256, ones128, hg_bd, l, bp, tp)
        o_rt, st_rt = _ret(rt_in, tbl["cos_qk"], tbl["sin_qk"], tbl["dq"], tbl["dk"], tbl["dmat"],
                           tbl["gam_tbl"], ret_bd, rnw_row, ones256, ones128, l, bp, tp)
        y5, st_s5 = _s5(s5_in, *s5_par, l, bp, tp)
        xp = _ffn(xp, o_hg, o_rt, y5, *ffn_par, l, 512)
        outs["hg_p"].append(st_hg)
        outs["rt_p"].append(st_rt)
        outs["s5_p"].append(st_s5)

        hg_in, rt_in, s5_in = _proj(xs, wp, l, bs)
        o_hg_t, new_hg_s = _hgrn_step(l, hg_in, lower_bounds[:, :, None],
                                      hgrn_norm_w[:, :, None], hg_s, new_hg_s)
        o_rt_t, new_rt_s = _ret_step(l, rt_in, tbl["cos_col"], tbl["sin_col"], gam_lanes,
                                     ret_norm_w[:, :, None], rt_s, new_rt_s)
        y5_t, new_re_s, new_im_s = _s5_step(l, s5_in, re_s, im_s, *s5_step_par, new_re_s, new_im_s)
        xs = _ffn(xs, o_hg_t.T, o_rt_t.T, y5_t.T, *ffn_par, l, bs)

    yp = _permute_tokens(xp.reshape(bp, tp, D_MODEL), inverse=True)

    def s5_out(a):
        return jnp.transpose(a.reshape(DEPTH, S5_GROUPS, S5_STATE, bs), (0, 3, 1, 2))

    hg7 = jnp.stack(outs["hg_p"]).reshape(DEPTH, bp, HG_HEADS // 2, 2, HG_DV, 2, HG_DK)
    hg_p = jnp.stack([hg7[:, :, :, e, :, e, :] for e in range(2)], axis=3)
    hg_p = jnp.swapaxes(hg_p, -1, -2).reshape(DEPTH, bp, HG_HEADS, HG_DK, HG_DV)
    rt6 = jnp.stack(outs["rt_p"]).reshape(DEPTH, bp, RET_HEADS, RET_DK, RET_HEADS, RET_DV)
    rt_p = jnp.stack([rt6[:, :, h, :, h, :] for h in range(RET_HEADS)], axis=2)
    s5_p = jnp.stack(outs["s5_p"]).reshape(DEPTH, bp, 2, S5_GROUPS, S5_STATE)

    return (yp, xs.reshape(bs, 1, D_MODEL), hg_p, rt_p, s5_p[:, :, 0], s5_p[:, :, 1],
            jnp.transpose(new_hg_s, (0, 4, 1, 2, 3)), jnp.transpose(new_rt_s, (0, 4, 1, 2, 3)),
            s5_out(new_re_s), s5_out(new_im_s))
```

```python
import jax
import jax.numpy as jnp
import numpy as np
from jax import lax
from jax.experimental import pallas as pl
from jax.experimental.pallas import tpu as pltpu

F32 = jnp.float32
BF16 = jnp.bfloat16

D_MODEL = 1024
DEPTH = 4
PAST_LEN = 16384
HG_WIDTH = 384
HG_HEADS = 6
HG_DK = 64
HG_DV = 64
RET_WIDTH = 384
RET_HEADS = 6
RET_DK = 32
RET_DV = 64
RET_QK = RET_HEADS * RET_DK
S5_WIDTH = 256
S5_GROUP = 16
S5_GROUPS = 16
S5_STATE = 64
S5_N = S5_GROUPS * S5_STATE
D_FF = 4 * D_MODEL
ROPE_BASE = 10000.0
EPS = 1e-5
ALPHA = (2.0 * DEPTH) ** 0.25

HG_COLS = 4 * HG_WIDTH
RT_COLS = 2 * RET_QK + 2 * RET_WIDTH
PROJ_COLS = HG_COLS + RT_COLS + S5_WIDTH
LANES = 128

SUBLANES = 8
TIME_BLOCK = 128
BLOCK_LEN = TIME_BLOCK // SUBLANES
SEQS_PER_STEP = 8
FF_CHUNK = 1024
FFN_ROW_GROUPS = 2
VMEM_LIMIT = 56 * 1024 * 1024


def _dot(a, b):
    return jnp.dot(a, b, preferred_element_type=F32)


def _dot_nt(a, b):
    return lax.dot_general(a, b, (((1,), (1,)), ((), ())), preferred_element_type=F32)


def _dot_tn(a, b):
    return lax.dot_general(a, b, (((0,), (0,)), ((), ())), preferred_element_type=F32)


def _sigmoid(x):
    return jax.nn.sigmoid(x)


def _silu(x):
    return x * jax.nn.sigmoid(x)


def _layer_norm(y, w, b):
    mu = jnp.mean(y, -1, keepdims=True)
    yc = y - mu
    var = jnp.mean(yc * yc, -1, keepdims=True)
    return yc * lax.rsqrt(var + EPS) * w + b


def _head_sum_bf16(x, ones256, ones128):
    return jnp.concatenate([_dot(x[:, 0:256], ones256), _dot(x[:, 256:384], ones128)], axis=1)


def _head_sum(x, ones256, ones128):
    hi = x.astype(BF16)
    lo = (x - hi.astype(F32)).astype(BF16)
    return _head_sum_bf16(hi, ones256, ones128) + _head_sum_bf16(lo, ones256, ones128)


def _proj_rows(x_ref, w_ref, hg_ref, rt_ref, s5_ref):
    x = x_ref[...].astype(BF16)
    hg_ref[...] = _dot(x, w_ref[:, 0:HG_COLS].astype(BF16))
    rt_ref[...] = _dot(x, w_ref[:, HG_COLS:HG_COLS + RT_COLS].astype(BF16))
    s5_ref[...] = _dot(x, w_ref[:, HG_COLS + RT_COLS:PROJ_COLS].astype(BF16))


def _proj_kernel(xp_ref, xs_ref, w_ref, hgp_ref, rtp_ref, s5p_ref, hgs_ref, rts_ref, s5s_ref):
    i = pl.program_id(0)
    last = pl.num_programs(0) - 1

    @pl.when(i < last)
    def _():
        _proj_rows(xp_ref, w_ref, hgp_ref, rtp_ref, s5p_ref)

    @pl.when(i == last)
    def _():
        _proj_rows(xs_ref, w_ref, hgs_ref, rts_ref, s5s_ref)


def _layer_spec(layer, shape, grid_rank, **kw):
    zeros = (0,) * len(shape)
    if grid_rank == 1:
        return pl.BlockSpec((None,) + shape, lambda i: (layer,) + zeros, **kw)
    return pl.BlockSpec((None,) + shape, lambda b, t: (layer,) + zeros, **kw)


def _two_path_specs(n_tiles, tm, ms, widths):
    tile = lambda i: (jnp.minimum(i, n_tiles - 1), 0)
    return ([pl.BlockSpec((tm, w), tile) for w in widths],
            [pl.BlockSpec((ms, w), lambda i: (0, 0)) for w in widths])


def _proj(xp, xs, w, layer, tm):
    m, ms = xp.shape[0], xs.shape[0]
    n_tiles = m // tm
    widths = (HG_COLS, RT_COLS, S5_WIDTH)
    (xp_spec,), (xs_spec,) = _two_path_specs(n_tiles, tm, ms, (D_MODEL,))
    outp, outs = _two_path_specs(n_tiles, tm, ms, widths)
    return pl.pallas_call(
        _proj_kernel,
        grid=(n_tiles + 1,),
        in_specs=[xp_spec, xs_spec,
                  _layer_spec(layer, (D_MODEL, PROJ_COLS), 1, pipeline_mode=pl.Buffered(1))],
        out_specs=outp + outs,
        out_shape=([jax.ShapeDtypeStruct((m, c), F32) for c in widths]
                   + [jax.ShapeDtypeStruct((ms, c), F32) for c in widths]),
        compiler_params=pltpu.CompilerParams(
            dimension_semantics=("arbitrary",), vmem_limit_bytes=VMEM_LIMIT),
        name="proj",
    )(xp, xs, w)


def _ffn_rows(rows, x_ref, a_ref, b_ref, c_ref, wo_ref, l1w_ref, l1b_ref, wu_ref, wd_ref,
              l2w_ref, l2b_ref, o_ref):
    x = x_ref[rows, :]
    mixed = (_dot(a_ref[rows, :].astype(BF16), wo_ref[0:HG_WIDTH, :])
             + _dot(b_ref[rows, :].astype(BF16), wo_ref[HG_WIDTH:HG_WIDTH + RET_WIDTH, :])
             + _dot(c_ref[rows, :].astype(BF16), wo_ref[HG_WIDTH + RET_WIDTH:D_MODEL, :]))
    yield
    x1 = _layer_norm(ALPHA * x + mixed, l1w_ref[...], l1b_ref[...])
    xb = x1.astype(BF16)
    yield
    ff = None
    for c in range(D_FF // FF_CHUNK):
        h = _dot(xb, wu_ref[:, c * FF_CHUNK:(c + 1) * FF_CHUNK])
        yield
        h = jnp.square(jnp.maximum(h, 0.0)).astype(BF16)
        yield
        d = _dot(h, wd_ref[c * FF_CHUNK:(c + 1) * FF_CHUNK, :])
        ff = d if ff is None else ff + d
        yield
    o_ref[rows, :] = _layer_norm(ALPHA * x1 + ff, l2w_ref[...], l2b_ref[...])


def _ffn_kernel(xp_ref, ap_ref, bp_ref, cp_ref, xs_ref, as_ref, bs_ref, cs_ref, *refs):
    weights, (op_ref, os_ref) = refs[:-2], refs[-2:]
    i = pl.program_id(0)
    last = pl.num_programs(0) - 1

    @pl.when(i < last)
    def _():
        _ffn_tile(xp_ref, ap_ref, bp_ref, cp_ref, *weights, op_ref)

    @pl.when(i == last)
    def _():
        _ffn_tile(xs_ref, as_ref, bs_ref, cs_ref, *weights, os_ref)


def _ffn_tile(x_ref, *refs):
    tm = x_ref.shape[0]
    groups = FFN_ROW_GROUPS if tm % (FFN_ROW_GROUPS * SUBLANES) == 0 and tm >= 256 else 1
    step = tm // groups
    live = [_ffn_rows(slice(g * step, (g + 1) * step), x_ref, *refs) for g in range(groups)]
    for g in range(groups):
        for gen in live[:g]:
            next(gen, _DONE)
    while live:
        live = [gen for gen in live if next(gen, _DONE) is not _DONE]


def _ffn(prompt, sample, wo, l1w, l1b, wu, wd, l2w, l2b, layer, tm):
    m, ms = prompt[0].shape[0], sample[0].shape[0]
    n_tiles = m // tm
    widths = (D_MODEL, HG_WIDTH, RET_WIDTH, S5_WIDTH)
    inp, ins = _two_path_specs(n_tiles, tm, ms, widths)
    (outp,), (outs,) = _two_path_specs(n_tiles, tm, ms, (D_MODEL,))
    once = pl.Buffered(1)
    vec = _layer_spec(layer, (1, D_MODEL), 1)
    return pl.pallas_call(
        _ffn_kernel,
        grid=(n_tiles + 1,),
        in_specs=inp + ins + [
            _layer_spec(layer, (D_MODEL, D_MODEL), 1, pipeline_mode=once),
            vec, vec,
            _layer_spec(layer, (D_MODEL, D_FF), 1, pipeline_mode=once),
            _layer_spec(layer, (D_FF, D_MODEL), 1, pipeline_mode=once),
            vec, vec],
        out_specs=[outp, outs],
        out_shape=[jax.ShapeDtypeStruct((m, D_MODEL), F32),
                   jax.ShapeDtypeStruct((ms, D_MODEL), F32)],
        compiler_params=pltpu.CompilerParams(
            dimension_semantics=("arbitrary",), vmem_limit_bytes=VMEM_LIMIT),
        name="ffn",
    )(*prompt, *sample, wo, l1w, l1b, wu, wd, l2w, l2b)


_DONE = object()


def _per_sequence(body, n_shared_in):
    def kern(x_ref, *refs):
        shared = refs[:n_shared_in]
        per_seq = refs[n_shared_in:]
        state_out, state_scr = per_seq[1], per_seq[2]
        t = pl.program_id(1)

        @pl.when(t == 0)
        def _():
            state_scr[...] = jnp.zeros(state_scr.shape, F32)

        live = [body(x_ref.at[i], *shared, *(r.at[i] for r in per_seq))
                for i in range(SEQS_PER_STEP)]
        while live:
            live = [g for g in live if next(g, _DONE) is not _DONE]

        @pl.when(t == pl.num_programs(1) - 1)
        def _():
            state_out[...] = state_scr[...]
    return kern


def _seq_major(a):
    return a.reshape(SEQS_PER_STEP, a.shape[0] // SEQS_PER_STEP, a.shape[1])


def _hgrn_body(x_ref, lb_ref, nw_ref, ones256_ref, ones128_ref, bd_ref, o_ref, st_ref,
               s_scr, qt_scr, kt_scr, v_scr, oi_scr):
    del st_ref
    npos = BLOCK_LEN
    sl = SUBLANES
    lb = lb_ref[...]
    ones256 = ones256_ref[...]
    ones128 = ones128_ref[...]
    f, kk, q, v = [], [], [], []
    for p in range(npos):
        rows = slice(sl * p, sl * (p + 1))
        hq = x_ref[rows, 0:HG_WIDTH]
        z = x_ref[rows, HG_WIDTH:2 * HG_WIDTH]
        f.append(lb + (1.0 - lb) * _sigmoid(z))
        kk.append(1.0 - f[-1])
        q.append(_silu(hq))
        v.append(x_ref[rows, 2 * HG_WIDTH:3 * HG_WIDTH])
        if p % 4 == 3:
            yield

    g = list(kk)
    e_rows = []
    for d in range(npos):
        if d > 0:
            g = [None] * d + [f[p] * g[p - 1] for p in range(d, npos)]
        e_rows.extend(q[p] * g[p] for p in range(d, npos))
        if d % 4 == 3:
            yield
    e_all = jnp.concatenate(e_rows, axis=0).astype(BF16)
    p_all = _head_sum_bf16(e_all, ones256, ones128)
    yield
    o = [None] * npos
    i = 0
    for d in range(npos):
        for p in range(d, npos):
            term = p_all[sl * i:sl * (i + 1)] * v[p - d]
            o[p] = term if o[p] is None else o[p] + term
            i += 1
        if d % 4 == 3:
            yield

    a = [f[0]]
    for p in range(1, npos):
        a.append(a[-1] * f[p])
    r = [None] * npos
    r[npos - 1] = jnp.ones_like(f[0])
    for p in range(npos - 2, -1, -1):
        r[p] = r[p + 1] * f[p + 1]
    npair = HG_HEADS // 2
    for p in range(npos):
        rows = slice(sl * p, sl * (p + 1))
        qt = q[p] * a[p]
        kt = kk[p] * r[p]
        for pr in range(npair):
            lanes = slice(128 * pr, 128 * (pr + 1))
            qt_scr[pr, rows, :] = qt[:, lanes]
            kt_scr[pr, rows, :] = kt[:, lanes]
            v_scr[pr, rows, :] = v[p][:, lanes]
    cd = a[npos - 1]
    for j in range(sl):
        blk = pl.ds(j, npos, stride=sl)
        for pr in range(npair):
            lanes = slice(128 * pr, 128 * (pr + 1))
            st = s_scr[pr]
            oi_scr[pr, blk, :] = _dot_nt(qt_scr[pr, blk, :].astype(BF16), st.astype(BF16))
            u = _dot_tn(v_scr[pr, blk, :].astype(BF16), kt_scr[pr, blk, :].astype(BF16))
            s_scr[pr] = st * cd[j:j + 1, lanes] + u * bd_ref[...]
        yield

    ot = (jnp.concatenate(o, axis=0)
          + jnp.concatenate([oi_scr[pr] for pr in range(npair)], axis=1))
    ms = _head_sum(ot * ot, ones256, ones128) * (1.0 / HG_DV)
    gate = x_ref[:, 3 * HG_WIDTH:4 * HG_WIDTH]
    o_ref[...] = ot * lax.rsqrt(ms + EPS) * nw_ref[...] * _silu(gate)


def _hgrn(hg_in, lb, nw, ones256, ones128, bd, layer, bsz, tlen):
    tb = TIME_BLOCK
    nt = tlen // tb
    ns = SEQS_PER_STEP
    npair = HG_HEADS // 2
    fixed2 = lambda b, t: (0, 0)
    tile = lambda b, t: (0, b * nt + t, 0)
    slab = pltpu.VMEM((ns, npair, tb, 128), F32)
    o, st = pl.pallas_call(
        _per_sequence(_hgrn_body, 5),
        grid=(bsz // ns, nt),
        in_specs=[pl.BlockSpec((ns, tb, HG_COLS), tile),
                  _layer_spec(layer, (1, HG_WIDTH), 2),
                  _layer_spec(layer, (1, HG_WIDTH), 2),
                  pl.BlockSpec((256, 256), fixed2),
                  pl.BlockSpec((128, 128), fixed2),
                  pl.BlockSpec((128, 128), fixed2)],
        out_specs=[pl.BlockSpec((ns, tb, HG_WIDTH), tile),
                   pl.BlockSpec((ns, None, npair, 128, 128), lambda b, t: (0, b, 0, 0, 0))],
        out_shape=[jax.ShapeDtypeStruct((ns, bsz // ns * tlen, HG_WIDTH), F32),
                   jax.ShapeDtypeStruct((ns, bsz // ns, npair, 128, 128), F32)],
        scratch_shapes=[pltpu.VMEM((ns, npair, 128, 128), F32), slab, slab, slab, slab],
        compiler_params=pltpu.CompilerParams(
            dimension_semantics=("parallel", "arbitrary"), vmem_limit_bytes=VMEM_LIMIT),
        name="hgrn",
    )(_seq_major(hg_in), lb, nw, ones256, ones128, bd)
    return o.reshape(bsz * tlen, HG_WIDTH), st.reshape(bsz, npair, 128, 128)


def _ret_body(x_ref, cos_ref, sin_ref, dq_ref, dk_ref, dmat_ref, gam_ref, bd_ref, nw_ref,
              ones256_ref, ones128_ref, o_ref, st_ref, s_scr):
    del st_ref
    tb = TIME_BLOCK
    w = RET_QK
    half = RET_DK // 2
    in_first_half = (lax.broadcasted_iota(jnp.int32, (tb, LANES), 1) & (RET_DK - 1)) < half
    rot = []
    for c in range(2 * w // LANES):
        lanes = slice(LANES * c, LANES * (c + 1))
        x = x_ref[:, lanes]
        partner = jnp.where(in_first_half, pltpu.roll(x, LANES - half, 1), pltpu.roll(x, half, 1))
        rot.append(x * cos_ref[:, lanes] + partner * sin_ref[:, lanes])
    rot = jnp.concatenate(rot, axis=1)
    q = rot[:, 0:w]
    k = rot[:, w:2 * w] * (RET_DK ** -0.5)
    v = x_ref[:, 2 * w:2 * w + RET_WIDTH]
    gate = x_ref[:, 2 * w + RET_WIDTH:2 * w + 2 * RET_WIDTH]
    vb = v.astype(BF16)

    lane = lax.broadcasted_iota(jnp.int32, (tb, w), 1)
    qs = jnp.concatenate(
        [jnp.where((lane >= RET_DK * h) & (lane < RET_DK * (h + 1)), q, 0.0)
         for h in range(RET_HEADS)], axis=0).astype(BF16)
    yield
    sc = _dot_nt(qs, k.astype(BF16))
    yield
    lane128 = lax.broadcasted_iota(jnp.int32, (tb, 128), 1)
    outs = []
    for p in range(RET_HEADS // 2):
        p0 = sc[(2 * p) * tb:(2 * p + 1) * tb] * dmat_ref[2 * p]
        p1 = sc[(2 * p + 1) * tb:(2 * p + 2) * tb] * dmat_ref[2 * p + 1]
        pc = jnp.concatenate([p0, p1], axis=1).astype(BF16)
        vp = v[:, 128 * p:128 * (p + 1)]
        vs = jnp.concatenate([jnp.where(lane128 < RET_DV, vp, 0.0),
                              jnp.where(lane128 >= RET_DV, vp, 0.0)], axis=0).astype(BF16)
        outs.append(_dot(pc, vs))
    o = jnp.concatenate(outs, axis=1)
    yield

    s = s_scr[...]
    o = o + _dot((q * dq_ref[...]).astype(BF16), s.astype(BF16))
    u = _dot_tn((k * dk_ref[...]).astype(BF16), vb)
    s_scr[...] = s * gam_ref[...] + u * bd_ref[...]

    ones256 = ones256_ref[...]
    ones128 = ones128_ref[...]
    mu = _head_sum(o, ones256, ones128) * (1.0 / RET_DV)
    oc = o - mu
    var = _head_sum(oc * oc, ones256, ones128) * (1.0 / RET_DV)
    o_ref[...] = oc * lax.rsqrt(var + EPS) * nw_ref[...] * _silu(gate)


def _ret(rt_in, cos, sin, dq, dk, dmat, gam, bd, nw, ones256, ones128, layer, bsz, tlen):
    tb = TIME_BLOCK
    nt = tlen // tb
    ns = SEQS_PER_STEP
    fixed2 = lambda b, t: (0, 0)
    tile = lambda b, t: (0, b * nt + t, 0)
    o, st = pl.pallas_call(
        _per_sequence(_ret_body, 10),
        grid=(bsz // ns, nt),
        in_specs=[pl.BlockSpec((ns, tb, RT_COLS), tile),
                  pl.BlockSpec((tb, 2 * RET_QK), lambda b, t: (t, 0)),
                  pl.BlockSpec((tb, 2 * RET_QK), lambda b, t: (t, 0)),
                  pl.BlockSpec((tb, RET_QK), fixed2),
                  pl.BlockSpec((tb, RET_QK), fixed2),
                  pl.BlockSpec((RET_HEADS, tb, tb), lambda b, t: (0, 0, 0)),
                  pl.BlockSpec((RET_QK, RET_WIDTH), fixed2),
                  pl.BlockSpec((RET_QK, RET_WIDTH), fixed2),
                  _layer_spec(layer, (1, RET_WIDTH), 2),
                  pl.BlockSpec((256, 256), fixed2),
                  pl.BlockSpec((128, 128), fixed2)],
        out_specs=[pl.BlockSpec((ns, tb, RET_WIDTH), tile),
                   pl.BlockSpec((ns, None, RET_QK, RET_WIDTH), lambda b, t: (0, b, 0, 0))],
        out_shape=[jax.ShapeDtypeStruct((ns, bsz // ns * tlen, RET_WIDTH), F32),
                   jax.ShapeDtypeStruct((ns, bsz // ns, RET_QK, RET_WIDTH), F32)],
        scratch_shapes=[pltpu.VMEM((ns, RET_QK, RET_WIDTH), F32)],
        compiler_params=pltpu.CompilerParams(
            dimension_semantics=("parallel", "arbitrary"), vmem_limit_bytes=VMEM_LIMIT),
        name="ret",
    )(_seq_major(rt_in), cos, sin, dq, dk, dmat, gam, bd, nw, ones256, ones128)
    return o.reshape(bsz * tlen, RET_WIDTH), st.reshape(bsz, RET_QK, RET_WIDTH)


def _cmul(ar, ai, xr, xi):
    return ar * xr - ai * xi, ar * xi + ai * xr


def _s5_body(u_ref, bblk_ref, cblk_ref, apow_ref, aend_ref, d_ref, gw_ref, gb_ref, y_ref, st_ref,
             x_scr):
    del st_ref
    npos = BLOCK_LEN
    sl = SUBLANES
    n = S5_N

    def power(m):
        return apow_ref[m - 1:m, 0:n], apow_ref[m - 1:m, n:2 * n]

    u = u_ref[...]
    bu = _dot(u.astype(BF16), bblk_ref[...])
    a1r, a1i = power(1)
    xr = [bu[0:sl, 0:n]]
    xi = [bu[0:sl, n:2 * n]]
    for p in range(1, npos):
        rows = slice(sl * p, sl * (p + 1))
        mr, mi = _cmul(a1r, a1i, xr[-1], xi[-1])
        xr.append(bu[rows, 0:n] + mr)
        xi.append(bu[rows, n:2 * n] + mi)

    yield
    blk = lax.broadcasted_iota(jnp.int32, (sl, n), 0)
    cr, ci = xr[-1], xi[-1]
    s = 1
    while s < sl:
        pr_, pi_ = power(npos * s)
        sr = jnp.where(blk >= s, pltpu.roll(cr, s, 0), 0.0)
        si = jnp.where(blk >= s, pltpu.roll(ci, s, 0), 0.0)
        mr, mi = _cmul(pr_, pi_, sr, si)
        cr, ci = cr + mr, ci + mi
        s *= 2
    x0r = x_scr[:, 0:n]
    x0i = x_scr[:, n:2 * n]
    mr, mi = _cmul(aend_ref[:, 0:n], aend_ref[:, n:2 * n], x0r, x0i)
    cr, ci = cr + mr, ci + mi
    x_scr[:, 0:n] = cr[sl - 1:sl]
    x_scr[:, n:2 * n] = ci[sl - 1:sl]
    inr = jnp.where(blk >= 1, pltpu.roll(cr, 1, 0), x0r)
    ini = jnp.where(blk >= 1, pltpu.roll(ci, 1, 0), x0i)
    rows_out = []
    for p in range(npos):
        pr_, pi_ = power(p + 1)
        mr, mi = _cmul(pr_, pi_, inr, ini)
        rows_out.append(jnp.concatenate([xr[p] + mr, xi[p] + mi], axis=1))
    xc = jnp.concatenate(rows_out, axis=0).astype(BF16)
    yield

    y = _dot(xc, cblk_ref[...]) + d_ref[...] * u
    y = jax.nn.gelu(y, approximate=True)
    y_ref[...] = y * _sigmoid(_dot(y.astype(BF16), gw_ref[...]) + gb_ref[...])


def _s5(u, bblk, cblk, apow, aend, dvec, gw, gb, layer, bsz, tlen):
    tb = TIME_BLOCK
    nt = tlen // tb
    ns = SEQS_PER_STEP
    tile = lambda b, t: (0, b * nt + t, 0)
    y, st = pl.pallas_call(
        _per_sequence(_s5_body, 7),
        grid=(bsz // ns, nt),
        in_specs=[pl.BlockSpec((ns, tb, S5_WIDTH), tile),
                  _layer_spec(layer, (S5_WIDTH, 2 * S5_N), 2),
                  _layer_spec(layer, (2 * S5_N, S5_WIDTH), 2),
                  _layer_spec(layer, (tb, 2 * S5_N), 2),
                  _layer_spec(layer, (SUBLANES, 2 * S5_N), 2),
                  _layer_spec(layer, (1, S5_WIDTH), 2),
                  _layer_spec(layer, (S5_WIDTH, S5_WIDTH), 2),
                  _layer_spec(layer, (1, S5_WIDTH), 2)],
        out_specs=[pl.BlockSpec((ns, tb, S5_WIDTH), tile),
                   pl.BlockSpec((ns, None, 1, 2 * S5_N), lambda b, t: (0, b, 0, 0))],
        out_shape=[jax.ShapeDtypeStruct((ns, bsz // ns * tlen, S5_WIDTH), F32),
                   jax.ShapeDtypeStruct((ns, bsz // ns, 1, 2 * S5_N), F32)],
        scratch_shapes=[pltpu.VMEM((ns, 1, 2 * S5_N), F32)],
        compiler_params=pltpu.CompilerParams(
            dimension_semantics=("parallel", "arbitrary"), vmem_limit_bytes=VMEM_LIMIT),
        name="s5",
    )(_seq_major(u), bblk, cblk, apow, aend, dvec, gw, gb)
    return y.reshape(bsz * tlen, S5_WIDTH), st.reshape(bsz, 1, 2 * S5_N)


def _hgrn_step_kernel(x_ref, lb_ref, nw_ref, s_ref, acc_ref, o_ref, sn_ref,
                      f_scr, k_scr, q_scr, v_scr, g_scr):
    del acc_ref
    h = pl.program_id(0)

    @pl.when(h == 0)
    def _():
        lb = lb_ref[...]
        z = x_ref[:, HG_WIDTH:2 * HG_WIDTH].T
        f_scr[...] = lb + (1.0 - lb) * _sigmoid(z)
        k_scr[...] = (1.0 - lb) * _sigmoid(-z)
        q_scr[...] = _silu(x_ref[:, 0:HG_WIDTH].T)
        v_scr[...] = x_ref[:, 2 * HG_WIDTH:3 * HG_WIDTH].T
        g_scr[...] = x_ref[:, 3 * HG_WIDTH:4 * HG_WIDTH].T

    rows = pl.ds(pl.multiple_of(h * HG_DK, HG_DK), HG_DK)
    f = f_scr[rows, :]
    kk = k_scr[rows, :]
    q = q_scr[rows, :]
    v = v_scr[rows, :]
    o = jnp.zeros_like(v)
    for k in range(HG_DK):
        sn = f[k:k + 1] * s_ref[k] + kk[k:k + 1] * v
        sn_ref[k] = sn
        o = o + q[k:k + 1] * sn
    ms = jnp.mean(o * o, axis=0, keepdims=True)
    o_ref[...] = o * lax.rsqrt(ms + EPS) * nw_ref[...] * _silu(g_scr[rows, :])


def _hgrn_step(layer, hg_in, lb_c, nw_c, s_all, acc):
    n = hg_in.shape[0]
    st = pl.BlockSpec((None, None, HG_DK, HG_DV, n), lambda h: (layer, h, 0, 0, 0))
    chan = pltpu.VMEM((HG_WIDTH, n), F32)
    return pl.pallas_call(
        _hgrn_step_kernel,
        grid=(HG_HEADS,),
        in_specs=[pl.BlockSpec((n, HG_COLS), lambda h: (0, 0)),
                  _layer_spec(layer, (HG_WIDTH, 1), 1),
                  _layer_spec(layer, (HG_DV, 1), 1),
                  st,
                  pl.BlockSpec(memory_space=pl.ANY)],
        out_specs=[pl.BlockSpec((HG_DV, n), lambda h: (h, 0)), st],
        out_shape=[jax.ShapeDtypeStruct((HG_WIDTH, n), F32),
                   jax.ShapeDtypeStruct(s_all.shape, F32)],
        scratch_shapes=[chan, chan, chan, chan, chan],
        input_output_aliases={4: 1},
        compiler_params=pltpu.CompilerParams(
            dimension_semantics=("arbitrary",), vmem_limit_bytes=VMEM_LIMIT),
        name="hgrn_step",
    )(hg_in, lb_c, nw_c, s_all, acc)


def _ret_step_kernel(x_ref, cos_ref, sin_ref, gam_ref, nw_ref, s_ref, acc_ref, o_ref, sn_ref,
                     qk_scr, v_scr, g_scr):
    del acc_ref
    half = RET_DK // 2
    h = pl.program_id(0)

    @pl.when(h == 0)
    def _():
        x = x_ref[:, 0:2 * RET_QK].T
        parts = []
        for i in range(2 * RET_HEADS):
            parts.append(x[RET_DK * i + half:RET_DK * (i + 1)])
            parts.append(x[RET_DK * i:RET_DK * i + half])
        qk_scr[...] = x * cos_ref[...] + jnp.concatenate(parts, axis=0) * sin_ref[...]
        v_scr[...] = x_ref[:, 2 * RET_QK:2 * RET_QK + RET_WIDTH].T
        g_scr[...] = x_ref[:, 2 * RET_QK + RET_WIDTH:2 * RET_QK + 2 * RET_WIDTH].T

    q = qk_scr[pl.ds(pl.multiple_of(h * RET_DK, RET_DK), RET_DK), :]
    kk = qk_scr[pl.ds(pl.multiple_of(RET_QK + h * RET_DK, RET_DK), RET_DK), :] * (RET_DK ** -0.5)
    vrows = pl.ds(pl.multiple_of(h * RET_DV, RET_DV), RET_DV)
    v = v_scr[vrows, :]
    gam = gam_ref[h]
    o = jnp.zeros_like(v)
    for k in range(RET_DK):
        sn = gam * s_ref[k] + kk[k:k + 1] * v
        sn_ref[k] = sn
        o = o + q[k:k + 1] * sn
    mu = jnp.mean(o, axis=0, keepdims=True)
    oc = o - mu
    var = jnp.mean(oc * oc, axis=0, keepdims=True)
    o_ref[...] = oc * lax.rsqrt(var + EPS) * nw_ref[vrows, :] * _silu(g_scr[vrows, :])


def _ret_step(layer, rt_in, cos_c, sin_c, gam, nw_c, s_all, acc):
    n = rt_in.shape[0]
    st = pl.BlockSpec((None, None, RET_DK, RET_DV, n), lambda h: (layer, h, 0, 0, 0))
    par = pl.BlockSpec((2 * RET_QK, 1), lambda h: (0, 0))
    return pl.pallas_call(
        _ret_step_kernel,
        grid=(RET_HEADS,),
        in_specs=[pl.BlockSpec((n, RT_COLS), lambda h: (0, 0)),
                  par, par,
                  pl.BlockSpec((RET_HEADS, 1, n), lambda h: (0, 0, 0)),
                  _layer_spec(layer, (RET_WIDTH, 1), 1),
                  st,
                  pl.BlockSpec(memory_space=pl.ANY)],
        out_specs=[pl.BlockSpec((RET_DV, n), lambda h: (h, 0)), st],
        out_shape=[jax.ShapeDtypeStruct((RET_WIDTH, n), F32),
                   jax.ShapeDtypeStruct(s_all.shape, F32)],
        scratch_shapes=[pltpu.VMEM((2 * RET_QK, n), F32),
                        pltpu.VMEM((RET_WIDTH, n), F32),
                        pltpu.VMEM((RET_WIDTH, n), F32)],
        input_output_aliases={6: 1},
        compiler_params=pltpu.CompilerParams(
            dimension_semantics=("arbitrary",), vmem_limit_bytes=VMEM_LIMIT),
        name="ret_step",
    )(rt_in, cos_c, sin_c, gam, nw_c, s_all, acc)


def _s5_step_kernel(u_ref, xr_ref, xi_ref, bt_ref, ct_ref, ab_ref, d_ref, gwt_ref, gb_ref,
                    accr_ref, acci_ref, y_ref, xrn_ref, xin_ref):
    del accr_ref, acci_ref
    n = S5_N
    ut = u_ref[...].T
    bu = _dot(bt_ref[...], ut.astype(BF16))
    mr, mi = _cmul(ab_ref[0:n], ab_ref[n:2 * n], xr_ref[...], xi_ref[...])
    xr = mr + bu[0:n]
    xi = mi + bu[n:2 * n]
    xrn_ref[...] = xr
    xin_ref[...] = xi
    y = _dot(ct_ref[...], jnp.concatenate([xr, xi], axis=0).astype(BF16)) + d_ref[...] * ut
    y = jax.nn.gelu(y, approximate=True)
    y_ref[...] = y * _sigmoid(_dot(gwt_ref[...], y.astype(BF16)) + gb_ref[...])


def _s5_step(layer, u, xr_all, xi_all, bt, ct, ab, dcol, gwt, gbcol, accr, acci):
    n = u.shape[0]
    st = _layer_spec(layer, (S5_N, n), 1)
    anyspace = pl.BlockSpec(memory_space=pl.ANY)
    return pl.pallas_call(
        _s5_step_kernel,
        grid=(1,),
        in_specs=[pl.BlockSpec((n, S5_WIDTH), lambda i: (0, 0)), st, st,
                  _layer_spec(layer, (2 * S5_N, S5_WIDTH), 1),
                  _layer_spec(layer, (S5_WIDTH, 2 * S5_N), 1),
                  _layer_spec(layer, (2 * S5_N, 1), 1),
                  _layer_spec(layer, (S5_WIDTH, 1), 1),
                  _layer_spec(layer, (S5_WIDTH, S5_WIDTH), 1),
                  _layer_spec(layer, (S5_WIDTH, 1), 1),
                  anyspace, anyspace],
        out_specs=[pl.BlockSpec((S5_WIDTH, n), lambda i: (0, 0)), st, st],
        out_shape=[jax.ShapeDtypeStruct((S5_WIDTH, n), F32),
                   jax.ShapeDtypeStruct(xr_all.shape, F32),
                   jax.ShapeDtypeStruct(xi_all.shape, F32)],
        input_output_aliases={9: 1, 10: 2},
        compiler_params=pltpu.CompilerParams(
            dimension_semantics=("arbitrary",), vmem_limit_bytes=VMEM_LIMIT),
        name="s5_step",
    )(u, xr_all, xi_all, bt, ct, ab, dcol, gwt, gbcol, accr, acci)


def _tile_order():
    rows = np.arange(TIME_BLOCK)
    return BLOCK_LEN * (rows % SUBLANES) + rows // SUBLANES


def _permute_tokens(x, inverse=False):
    bsz, tlen, dm = x.shape
    a, b = (BLOCK_LEN, SUBLANES) if inverse else (SUBLANES, BLOCK_LEN)
    x = x.reshape(bsz, tlen // TIME_BLOCK, a, b, dm)
    return jnp.swapaxes(x, 2, 3).reshape(bsz, tlen, dm)


def _rope_tables(pos):
    half = RET_DK // 2
    inv = ROPE_BASE ** (-np.arange(half, dtype=np.float64) / half)
    ang = np.asarray(pos, np.float64)[:, None] * inv[None, :]
    cos_h = np.concatenate([np.cos(ang), np.cos(ang)], axis=1)
    sin_h = np.concatenate([-np.sin(ang), np.sin(ang)], axis=1)
    return (np.tile(cos_h, (1, RET_HEADS)).astype(np.float32),
            np.tile(sin_h, (1, RET_HEADS)).astype(np.float32))


def _position_tables(tlen):
    tb = TIME_BLOCK
    order = _tile_order()
    pos_p = np.arange(tlen) // tb * tb + np.tile(order, tlen // tb)
    cos_p, sin_p = _rope_tables(pos_p)
    cos_s, sin_s = _rope_tables(np.array([PAST_LEN]))
    log_gamma = np.log1p(-np.exp2(-5.0 - np.arange(RET_HEADS, dtype=np.float64)))
    lg_lane = np.repeat(log_gamma, RET_DK)[None, :]
    tt = order.astype(np.float64)
    diff = tt[:, None] - tt[None, :]
    f32 = lambda a: jnp.asarray(np.asarray(a, np.float32))
    return dict(
        cos_qk=f32(np.concatenate([cos_p, cos_p], axis=1)),
        sin_qk=f32(np.concatenate([sin_p, sin_p], axis=1)),
        cos_col=f32(np.concatenate([cos_s, cos_s], axis=1).reshape(2 * RET_QK, 1)),
        sin_col=f32(np.concatenate([sin_s, sin_s], axis=1).reshape(2 * RET_QK, 1)),
        dq=f32(np.exp((tt[:, None] + 1.0) * lg_lane)),
        dk=f32(np.exp((tb - 1.0 - tt[:, None]) * lg_lane)),
        dmat=f32(np.where(diff >= 0, np.exp(np.maximum(diff, 0.0)[None] * log_gamma[:, None, None]), 0.0)),
        gam_tbl=f32(np.broadcast_to(np.exp(tb * np.repeat(log_gamma, RET_DK))[:, None],
                                    (RET_QK, RET_WIDTH))),
        gamma=np.exp(log_gamma),
    )


def _block_diag_mask(rows_per, cols_per, nblk):
    r = np.arange(rows_per * nblk)[:, None] // rows_per
    c = np.arange(cols_per * nblk)[None, :] // cols_per
    return (r == c).astype(np.float32)


def _s5_params(log_dt, a_re, a_im, b_re, b_im, c_re, c_im):
    nl = log_dt.shape[0]
    dt = jnp.exp(log_dt)[..., None]
    mag = jnp.exp(dt * a_re)
    ab_re = mag * jnp.cos(dt * a_im)
    ab_im = mag * jnp.sin(dt * a_im)
    den = a_re * a_re + a_im * a_im
    nr = ab_re - 1.0
    g_re = (nr * a_re + ab_im * a_im) / den
    g_im = (ab_im * a_re - nr * a_im) / den
    bb_re = g_re[..., None] * b_re - g_im[..., None] * b_im
    bb_im = g_re[..., None] * b_im + g_im[..., None] * b_re
    eye = jnp.eye(S5_GROUPS, dtype=F32)[None, :, None, :, None]

    def in_blk(bb):
        m = eye * jnp.swapaxes(bb, 2, 3)[:, :, :, None, :]
        return m.reshape(nl, S5_WIDTH, S5_N)

    def out_blk(c):
        m = eye * jnp.swapaxes(c, 2, 3)[:, :, :, None, :]
        return m.reshape(nl, S5_N, S5_WIDTH)

    bblk = jnp.concatenate([in_blk(bb_re), in_blk(bb_im)], axis=2)
    cblk = jnp.concatenate([out_blk(c_re), -out_blk(c_im)], axis=1)
    m = jnp.arange(1, TIME_BLOCK + 1, dtype=F32)[None, :, None, None]
    pmag = jnp.exp(m * (dt * a_re)[:, None])
    pang = m * (dt * a_im)[:, None]
    apow = jnp.concatenate([(pmag * jnp.cos(pang)).reshape(nl, TIME_BLOCK, S5_N),
                            (pmag * jnp.sin(pang)).reshape(nl, TIME_BLOCK, S5_N)], axis=2)
    ab = jnp.concatenate([ab_re.reshape(nl, S5_N, 1), ab_im.reshape(nl, S5_N, 1)], axis=1)
    return bblk, cblk, apow, ab


def kernel(x_prompt, x_sample, state_hgrn, state_ret, state_s5_re, state_s5_im, w_in, hgrn_lb_logits, hgrn_norm_w, ret_norm_w, s5_log_dt, s5_a_re, s5_a_im, s5_b_re, s5_b_im, s5_c_re, s5_c_im, s5_d, s5_glu_w, s5_glu_b, w_out, ln1_w, ln1_b, w_up, w_down, ln2_w, ln2_b):
    bp, tp, _ = x_prompt.shape
    bs = x_sample.shape[0]
    tb = TIME_BLOCK

    lb_prob = jax.nn.softmax(hgrn_lb_logits.astype(F32), axis=0)
    lower_bounds = jnp.cumsum(lb_prob, axis=0) - lb_prob[0:1]

    tbl = _position_tables(tp)
    ret_bd = jnp.asarray(_block_diag_mask(RET_DK, RET_DV, RET_HEADS))
    hg_bd = jnp.asarray(_block_diag_mask(HG_DV, HG_DK, 2))
    ones256 = jnp.asarray(_block_diag_mask(64, 64, 4)).astype(BF16)
    ones128 = jnp.asarray(_block_diag_mask(64, 64, 2)).astype(BF16)
    gam_lanes = jnp.asarray(np.broadcast_to(tbl["gamma"][:, None, None],
                                            (RET_HEADS, 1, bs)).astype(np.float32))

    wo = w_out.astype(BF16)
    ffn_par = (wo, ln1_w[:, None, :], ln1_b[:, None, :], w_up.astype(BF16), w_down.astype(BF16),
               ln2_w[:, None, :], ln2_b[:, None, :])
    lb_row = lower_bounds[:, None, :]
    hnw_row = jnp.tile(hgrn_norm_w, (1, HG_HEADS))[:, None, :]
    rnw_row = ret_norm_w[:, None, :]
    bblk, cblk, apow, ab_col = _s5_params(s5_log_dt, s5_a_re, s5_a_im, s5_b_re, s5_b_im,
                                          s5_c_re, s5_c_im)
    s5_par = (bblk.astype(BF16), cblk.astype(BF16), apow, apow[:, BLOCK_LEN - 1::BLOCK_LEN],
              s5_d[:, None, :], s5_glu_w.astype(BF16), s5_glu_b[:, None, :])
    s5_step_par = (jnp.swapaxes(s5_par[0], 1, 2), jnp.swapaxes(s5_par[1], 1, 2), ab_col,
                   s5_d[:, :, None],
                   jnp.swapaxes(s5_glu_w, 1, 2).astype(BF16), s5_glu_b[:, :, None])

    hg_s = jnp.transpose(state_hgrn, (0, 2, 3, 4, 1))
    rt_s = jnp.transpose(state_ret, (0, 2, 3, 4, 1))
    re_s = jnp.transpose(state_s5_re, (0, 2, 3, 1)).reshape(DEPTH, S5_N, bs)
    im_s = jnp.transpose(state_s5_im, (0, 2, 3, 1)).reshape(DEPTH, S5_N, bs)
    new_hg_s = jnp.zeros(hg_s.shape, F32)
    new_rt_s = jnp.zeros(rt_s.shape, F32)
    new_re_s = jnp.zeros(re_s.shape, F32)
    new_im_s = jnp.zeros(im_s.shape, F32)

    xp = _permute_tokens(x_prompt).reshape(bp * tp, D_MODEL)
    xs = x_sample.reshape(bs, D_MODEL)
    outs = {k: [] for k in ("hg_p", "rt_p", "s5_p")}
    for l in range(DEPTH):
        hg_in, rt_in, s5_in, hg_in_s, rt_in_s, s5_in_s = _proj(xp, xs, w_in, l, 512)
        o_hg, st_hg = _hgrn(hg_in, lb_row, hnw_row, ones256, ones128, hg_bd, l, bp, tp)
        o_rt, st_rt = _ret(rt_in, tbl["cos_qk"], tbl["sin_qk"], tbl["dq"], tbl["dk"], tbl["dmat"],
                           tbl["gam_tbl"], ret_bd, rnw_row, ones256, ones128, l, bp, tp)
        y5, st_s5 = _s5(s5_in, *s5_par, l, bp, tp)
        outs["hg_p"].append(st_hg)
        outs["rt_p"].append(st_rt)
        outs["s5_p"].append(st_s5)
        o_hg_t, new_hg_s = _hgrn_step(l, hg_in_s, lower_bounds[:, :, None],
                                      hgrn_norm_w[:, :, None], hg_s, new_hg_s)
        o_rt_t, new_rt_s = _ret_step(l, rt_in_s, tbl["cos_col"], tbl["sin_col"], gam_lanes,
                                     ret_norm_w[:, :, None], rt_s, new_rt_s)
        y5_t, new_re_s, new_im_s = _s5_step(l, s5_in_s, re_s, im_s, *s5_step_par,
                                            new_re_s, new_im_s)
        xp, xs = _ffn((xp, o_hg, o_rt, y5), (xs, o_hg_t.T, o_rt_t.T, y5_t.T), *ffn_par, l, 512)

    yp = _permute_tokens(xp.reshape(bp, tp, D_MODEL), inverse=True)

    def s5_out(a):
        return jnp.transpose(a.reshape(DEPTH, S5_GROUPS, S5_STATE, bs), (0, 3, 1, 2))

    hg7 = jnp.stack(outs["hg_p"]).reshape(DEPTH, bp, HG_HEADS // 2, 2, HG_DV, 2, HG_DK)
    hg_p = jnp.stack([hg7[:, :, :, e, :, e, :] for e in range(2)], axis=3)
    hg_p = jnp.swapaxes(hg_p, -1, -2).reshape(DEPTH, bp, HG_HEADS, HG_DK, HG_DV)
    rt6 = jnp.stack(outs["rt_p"]).reshape(DEPTH, bp, RET_HEADS, RET_DK, RET_HEADS, RET_DV)
    rt_p = jnp.stack([rt6[:, :, h, :, h, :] for h in range(RET_HEADS)], axis=2)
    s5_p = jnp.stack(outs["s5_p"]).reshape(DEPTH, bp, 2, S5_GROUPS, S5_STATE)

    return (yp, xs.reshape(bs, 1, D_MODEL), hg_p, rt_p, s5_p[:, :, 0], s5_p[:, :, 1],
            jnp.transpose(new_hg_s, (0, 4, 1, 2, 3)), jnp.transpose(new_rt_s, (0, 4, 1, 2, 3)),
            s5_out(new_re_s), s5_out(new_im_s))
```

```python
import jax
import jax.numpy as jnp
import numpy as np
from jax import lax
from jax.experimental import pallas as pl
from jax.experimental.pallas import tpu as pltpu

F32 = jnp.float32
BF16 = jnp.bfloat16

D_MODEL = 1024
DEPTH = 4
PAST_LEN = 16384
HG_WIDTH = 384
HG_HEADS = 6
HG_DK = 64
HG_DV = 64
RET_WIDTH = 384
RET_HEADS = 6
RET_DK = 32
RET_DV = 64
RET_QK = RET_HEADS * RET_DK
S5_WIDTH = 256
S5_GROUP = 16
S5_GROUPS = 16
S5_STATE = 64
S5_N = S5_GROUPS * S5_STATE
D_FF = 4 * D_MODEL
ROPE_BASE = 10000.0
EPS = 1e-5
ALPHA = (2.0 * DEPTH) ** 0.25

HG_COLS = 4 * HG_WIDTH
RT_COLS = 2 * RET_QK + 2 * RET_WIDTH
PROJ_COLS = HG_COLS + RT_COLS + S5_WIDTH
LANES = 128

SUBLANES = 8
TIME_BLOCK = 128
BLOCK_LEN = TIME_BLOCK // SUBLANES
SEQS_PER_STEP = 8
FF_CHUNK = 1024
FFN_ROW_GROUPS = 2
VMEM_LIMIT = 56 * 1024 * 1024


def _dot(a, b):
    return jnp.dot(a, b, preferred_element_type=F32)


def _dot_nt(a, b):
    return lax.dot_general(a, b, (((1,), (1,)), ((), ())), preferred_element_type=F32)


def _dot_tn(a, b):
    return lax.dot_general(a, b, (((0,), (0,)), ((), ())), preferred_element_type=F32)


def _sigmoid(x):
    return jax.nn.sigmoid(x)


def _silu(x):
    return x * jax.nn.sigmoid(x)


def _layer_norm(y, w, b):
    mu = jnp.mean(y, -1, keepdims=True)
    yc = y - mu
    var = jnp.mean(yc * yc, -1, keepdims=True)
    return yc * lax.rsqrt(var + EPS) * w + b


def _head_sum_bf16(x, ones256, ones128):
    return jnp.concatenate([_dot(x[:, 0:256], ones256), _dot(x[:, 256:384], ones128)], axis=1)


def _head_sum(x, ones256, ones128):
    hi = x.astype(BF16)
    lo = (x - hi.astype(F32)).astype(BF16)
    return _head_sum_bf16(hi, ones256, ones128) + _head_sum_bf16(lo, ones256, ones128)


def _proj_rows(x_ref, w_ref, hg_ref, rt_ref, s5_ref):
    x = x_ref[...].astype(BF16)
    hg_ref[...] = _dot(x, w_ref[:, 0:HG_COLS].astype(BF16))
    rt_ref[...] = _dot(x, w_ref[:, HG_COLS:HG_COLS + RT_COLS].astype(BF16))
    s5_ref[...] = _dot(x, w_ref[:, HG_COLS + RT_COLS:PROJ_COLS].astype(BF16))


def _proj_kernel(xp_ref, xs_ref, w_ref, hgp_ref, rtp_ref, s5p_ref, hgs_ref, rts_ref, s5s_ref):
    i = pl.program_id(0)
    last = pl.num_programs(0) - 1

    @pl.when(i < last)
    def _():
        _proj_rows(xp_ref, w_ref, hgp_ref, rtp_ref, s5p_ref)

    @pl.when(i == last)
    def _():
        _proj_rows(xs_ref, w_ref, hgs_ref, rts_ref, s5s_ref)


def _layer_spec(layer, shape, grid_rank, **kw):
    zeros = (0,) * len(shape)
    if grid_rank == 1:
        return pl.BlockSpec((None,) + shape, lambda i: (layer,) + zeros, **kw)
    return pl.BlockSpec((None,) + shape, lambda b, t: (layer,) + zeros, **kw)


def _two_path_specs(n_tiles, tm, ms, widths):
    tile = lambda i: (jnp.minimum(i, n_tiles - 1), 0)
    return ([pl.BlockSpec((tm, w), tile) for w in widths],
            [pl.BlockSpec((ms, w), lambda i: (0, 0)) for w in widths])


def _proj(xp, xs, w, layer, tm):
    m, ms = xp.shape[0], xs.shape[0]
    n_tiles = m // tm
    widths = (HG_COLS, RT_COLS, S5_WIDTH)
    (xp_spec,), (xs_spec,) = _two_path_specs(n_tiles, tm, ms, (D_MODEL,))
    outp, outs = _two_path_specs(n_tiles, tm, ms, widths)
    return pl.pallas_call(
        _proj_kernel,
        grid=(n_tiles + 1,),
        in_specs=[xp_spec, xs_spec,
                  _layer_spec(layer, (D_MODEL, PROJ_COLS), 1, pipeline_mode=pl.Buffered(1))],
        out_specs=outp + outs,
        out_shape=([jax.ShapeDtypeStruct((m, c), F32) for c in widths]
                   + [jax.ShapeDtypeStruct((ms, c), F32) for c in widths]),
        compiler_params=pltpu.CompilerParams(
            dimension_semantics=("arbitrary",), vmem_limit_bytes=VMEM_LIMIT),
        name="proj",
    )(xp, xs, w)


def _ffn_rows(rows, channel_major, x_ref, a_ref, b_ref, c_ref, wo_ref, l1w_ref, l1b_ref, wu_ref,
              wd_ref, l2w_ref, l2b_ref, o_ref):
    take = (lambda r: r[...].T) if channel_major else (lambda r: r[rows, :])
    x = x_ref[rows, :]
    mixed = (_dot(take(a_ref).astype(BF16), wo_ref[0:HG_WIDTH, :])
             + _dot(take(b_ref).astype(BF16), wo_ref[HG_WIDTH:HG_WIDTH + RET_WIDTH, :])
             + _dot(take(c_ref).astype(BF16), wo_ref[HG_WIDTH + RET_WIDTH:D_MODEL, :]))
    yield
    x1 = _layer_norm(ALPHA * x + mixed, l1w_ref[...], l1b_ref[...])
    xb = x1.astype(BF16)
    yield
    ff = None
    for c in range(D_FF // FF_CHUNK):
        h = _dot(xb, wu_ref[:, c * FF_CHUNK:(c + 1) * FF_CHUNK])
        yield
        h = jnp.square(jnp.maximum(h, 0.0)).astype(BF16)
        yield
        d = _dot(h, wd_ref[c * FF_CHUNK:(c + 1) * FF_CHUNK, :])
        ff = d if ff is None else ff + d
        yield
    o_ref[rows, :] = _layer_norm(ALPHA * x1 + ff, l2w_ref[...], l2b_ref[...])


def _ffn_kernel(xp_ref, ap_ref, bp_ref, cp_ref, xs_ref, as_ref, bs_ref, cs_ref, *refs):
    weights, (op_ref, os_ref) = refs[:-2], refs[-2:]
    i = pl.program_id(0)
    last = pl.num_programs(0) - 1

    @pl.when(i < last)
    def _():
        _ffn_tile(FFN_ROW_GROUPS, False, xp_ref, ap_ref, bp_ref, cp_ref, *weights, op_ref)

    @pl.when(i == last)
    def _():
        _ffn_tile(1, True, xs_ref, as_ref, bs_ref, cs_ref, *weights, os_ref)


def _ffn_tile(groups, channel_major, x_ref, *refs):
    step = x_ref.shape[0] // groups
    live = [_ffn_rows(slice(g * step, (g + 1) * step), channel_major, x_ref, *refs)
            for g in range(groups)]
    for g in range(groups):
        for gen in live[:g]:
            next(gen, _DONE)
    while live:
        live = [gen for gen in live if next(gen, _DONE) is not _DONE]


def _ffn(prompt, sample, wo, l1w, l1b, wu, wd, l2w, l2b, layer, tm):
    m, ms = prompt[0].shape[0], sample[0].shape[0]
    n_tiles = m // tm
    widths = (D_MODEL, HG_WIDTH, RET_WIDTH, S5_WIDTH)
    inp, ins = _two_path_specs(n_tiles, tm, ms, widths)
    ins = ins[:1] + [pl.BlockSpec((w, ms), lambda i: (0, 0)) for w in widths[1:]]
    (outp,), (outs,) = _two_path_specs(n_tiles, tm, ms, (D_MODEL,))
    once = pl.Buffered(1)
    vec = _layer_spec(layer, (1, D_MODEL), 1)
    return pl.pallas_call(
        _ffn_kernel,
        grid=(n_tiles + 1,),
        in_specs=inp + ins + [
            _layer_spec(layer, (D_MODEL, D_MODEL), 1, pipeline_mode=once),
            vec, vec,
            _layer_spec(layer, (D_MODEL, D_FF), 1, pipeline_mode=once),
            _layer_spec(layer, (D_FF, D_MODEL), 1, pipeline_mode=once),
            vec, vec],
        out_specs=[outp, outs],
        out_shape=[jax.ShapeDtypeStruct((m, D_MODEL), F32),
                   jax.ShapeDtypeStruct((ms, D_MODEL), F32)],
        compiler_params=pltpu.CompilerParams(
            dimension_semantics=("arbitrary",), vmem_limit_bytes=VMEM_LIMIT),
        name="ffn",
    )(*prompt, *sample, wo, l1w, l1b, wu, wd, l2w, l2b)


_DONE = object()


def _per_sequence(body, n_shared_in):
    def kern(x_ref, *refs):
        shared = refs[:n_shared_in]
        per_seq = refs[n_shared_in:]
        state_out, state_scr = per_seq[1], per_seq[2]
        t = pl.program_id(1)

        @pl.when(t == 0)
        def _():
            state_scr[...] = jnp.zeros(state_scr.shape, F32)

        live = [body(x_ref.at[i], *shared, *(r.at[i] for r in per_seq))
                for i in range(SEQS_PER_STEP)]
        while live:
            live = [g for g in live if next(g, _DONE) is not _DONE]

        @pl.when(t == pl.num_programs(1) - 1)
        def _():
            state_out[...] = state_scr[...]
    return kern


def _seq_major(a):
    return a.reshape(SEQS_PER_STEP, a.shape[0] // SEQS_PER_STEP, a.shape[1])


def _hgrn_body(x_ref, lb_ref, nw_ref, ones256_ref, ones128_ref, bd_ref, o_ref, st_ref,
               s_scr, qt_scr, kt_scr, v_scr, oi_scr):
    del st_ref
    npos = BLOCK_LEN
    sl = SUBLANES
    lb = lb_ref[...]
    ones256 = ones256_ref[...]
    ones128 = ones128_ref[...]
    f, kk, q, v = [], [], [], []
    for p in range(npos):
        rows = slice(sl * p, sl * (p + 1))
        hq = x_ref[rows, 0:HG_WIDTH]
        z = x_ref[rows, HG_WIDTH:2 * HG_WIDTH]
        f.append(lb + (1.0 - lb) * _sigmoid(z))
        kk.append(1.0 - f[-1])
        q.append(_silu(hq))
        v.append(x_ref[rows, 2 * HG_WIDTH:3 * HG_WIDTH])
        if p % 4 == 3:
            yield

    g = list(kk)
    e_rows = []
    for d in range(npos):
        if d > 0:
            g = [None] * d + [f[p] * g[p - 1] for p in range(d, npos)]
        e_rows.extend(q[p] * g[p] for p in range(d, npos))
        if d % 4 == 3:
            yield
    e_all = jnp.concatenate(e_rows, axis=0).astype(BF16)
    p_all = _head_sum_bf16(e_all, ones256, ones128)
    yield
    o = [None] * npos
    i = 0
    for d in range(npos):
        for p in range(d, npos):
            term = p_all[sl * i:sl * (i + 1)] * v[p - d]
            o[p] = term if o[p] is None else o[p] + term
            i += 1
        if d % 4 == 3:
            yield

    a = [f[0]]
    for p in range(1, npos):
        a.append(a[-1] * f[p])
    r = [None] * npos
    r[npos - 1] = jnp.ones_like(f[0])
    for p in range(npos - 2, -1, -1):
        r[p] = r[p + 1] * f[p + 1]
    npair = HG_HEADS // 2
    for p in range(npos):
        rows = slice(sl * p, sl * (p + 1))
        qt = q[p] * a[p]
        kt = kk[p] * r[p]
        for pr in range(npair):
            lanes = slice(128 * pr, 128 * (pr + 1))
            qt_scr[pr, rows, :] = qt[:, lanes]
            kt_scr[pr, rows, :] = kt[:, lanes]
            v_scr[pr, rows, :] = v[p][:, lanes]
    cd = a[npos - 1]
    for j in range(sl):
        blk = pl.ds(j, npos, stride=sl)
        for pr in range(npair):
            lanes = slice(128 * pr, 128 * (pr + 1))
            st = s_scr[pr]
            oi_scr[pr, blk, :] = _dot_nt(qt_scr[pr, blk, :].astype(BF16), st.astype(BF16))
            u = _dot_tn(v_scr[pr, blk, :].astype(BF16), kt_scr[pr, blk, :].astype(BF16))
            s_scr[pr] = st * cd[j:j + 1, lanes] + u * bd_ref[...]
        yield

    ot = (jnp.concatenate(o, axis=0)
          + jnp.concatenate([oi_scr[pr] for pr in range(npair)], axis=1))
    ms = _head_sum(ot * ot, ones256, ones128) * (1.0 / HG_DV)
    gate = x_ref[:, 3 * HG_WIDTH:4 * HG_WIDTH]
    o_ref[...] = ot * lax.rsqrt(ms + EPS) * nw_ref[...] * _silu(gate)


def _hgrn(hg_in, lb, nw, ones256, ones128, bd, layer, bsz, tlen):
    tb = TIME_BLOCK
    nt = tlen // tb
    ns = SEQS_PER_STEP
    npair = HG_HEADS // 2
    fixed2 = lambda b, t: (0, 0)
    tile = lambda b, t: (0, b * nt + t, 0)
    slab = pltpu.VMEM((ns, npair, tb, 128), F32)
    o, st = pl.pallas_call(
        _per_sequence(_hgrn_body, 5),
        grid=(bsz // ns, nt),
        in_specs=[pl.BlockSpec((ns, tb, HG_COLS), tile),
                  _layer_spec(layer, (1, HG_WIDTH), 2),
                  _layer_spec(layer, (1, HG_WIDTH), 2),
                  pl.BlockSpec((256, 256), fixed2),
                  pl.BlockSpec((128, 128), fixed2),
                  pl.BlockSpec((128, 128), fixed2)],
        out_specs=[pl.BlockSpec((ns, tb, HG_WIDTH), tile),
                   pl.BlockSpec((ns, None, npair, 128, 128), lambda b, t: (0, b, 0, 0, 0))],
        out_shape=[jax.ShapeDtypeStruct((ns, bsz // ns * tlen, HG_WIDTH), F32),
                   jax.ShapeDtypeStruct((ns, bsz // ns, npair, 128, 128), F32)],
        scratch_shapes=[pltpu.VMEM((ns, npair, 128, 128), F32), slab, slab, slab, slab],
        compiler_params=pltpu.CompilerParams(
            dimension_semantics=("parallel", "arbitrary"), vmem_limit_bytes=VMEM_LIMIT),
        name="hgrn",
    )(_seq_major(hg_in), lb, nw, ones256, ones128, bd)
    return o.reshape(bsz * tlen, HG_WIDTH), st.reshape(bsz, npair, 128, 128)


def _ret_body(x_ref, cos_ref, sin_ref, dq_ref, dk_ref, dmat_ref, gam_ref, bd_ref, nw_ref,
              ones256_ref, ones128_ref, o_ref, st_ref, s_scr):
    del st_ref
    tb = TIME_BLOCK
    w = RET_QK
    half = RET_DK // 2
    in_first_half = (lax.broadcasted_iota(jnp.int32, (tb, LANES), 1) & (RET_DK - 1)) < half
    rot = []
    for c in range(2 * w // LANES):
        lanes = slice(LANES * c, LANES * (c + 1))
        x = x_ref[:, lanes]
        partner = jnp.where(in_first_half, pltpu.roll(x, LANES - half, 1), pltpu.roll(x, half, 1))
        rot.append(x * cos_ref[:, lanes] + partner * sin_ref[:, lanes])
    rot = jnp.concatenate(rot, axis=1)
    q = rot[:, 0:w]
    k = rot[:, w:2 * w] * (RET_DK ** -0.5)
    v = x_ref[:, 2 * w:2 * w + RET_WIDTH]
    gate = x_ref[:, 2 * w + RET_WIDTH:2 * w + 2 * RET_WIDTH]
    vb = v.astype(BF16)

    lane = lax.broadcasted_iota(jnp.int32, (tb, w), 1)
    qs = jnp.concatenate(
        [jnp.where((lane >= RET_DK * h) & (lane < RET_DK * (h + 1)), q, 0.0)
         for h in range(RET_HEADS)], axis=0).astype(BF16)
    yield
    sc = _dot_nt(qs, k.astype(BF16))
    yield
    lane128 = lax.broadcasted_iota(jnp.int32, (tb, 128), 1)
    outs = []
    for p in range(RET_HEADS // 2):
        p0 = sc[(2 * p) * tb:(2 * p + 1) * tb] * dmat_ref[2 * p]
        p1 = sc[(2 * p + 1) * tb:(2 * p + 2) * tb] * dmat_ref[2 * p + 1]
        pc = jnp.concatenate([p0, p1], axis=1).astype(BF16)
        vp = v[:, 128 * p:128 * (p + 1)]
        vs = jnp.concatenate([jnp.where(lane128 < RET_DV, vp, 0.0),
                              jnp.where(lane128 >= RET_DV, vp, 0.0)], axis=0).astype(BF16)
        outs.append(_dot(pc, vs))
    o = jnp.concatenate(outs, axis=1)
    yield

    s = s_scr[...]
    o = o + _dot((q * dq_ref[...]).astype(BF16), s.astype(BF16))
    u = _dot_tn((k * dk_ref[...]).astype(BF16), vb)
    s_scr[...] = s * gam_ref[...] + u * bd_ref[...]

    ones256 = ones256_ref[...]
    ones128 = ones128_ref[...]
    mu = _head_sum(o, ones256, ones128) * (1.0 / RET_DV)
    oc = o - mu
    var = _head_sum(oc * oc, ones256, ones128) * (1.0 / RET_DV)
    o_ref[...] = oc * lax.rsqrt(var + EPS) * nw_ref[...] * _silu(gate)


def _ret(rt_in, cos, sin, dq, dk, dmat, gam, bd, nw, ones256, ones128, layer, bsz, tlen):
    tb = TIME_BLOCK
    nt = tlen // tb
    ns = SEQS_PER_STEP
    fixed2 = lambda b, t: (0, 0)
    tile = lambda b, t: (0, b * nt + t, 0)
    o, st = pl.pallas_call(
        _per_sequence(_ret_body, 10),
        grid=(bsz // ns, nt),
        in_specs=[pl.BlockSpec((ns, tb, RT_COLS), tile),
                  pl.BlockSpec((tb, 2 * RET_QK), lambda b, t: (t, 0)),
                  pl.BlockSpec((tb, 2 * RET_QK), lambda b, t: (t, 0)),
                  pl.BlockSpec((tb, RET_QK), fixed2),
                  pl.BlockSpec((tb, RET_QK), fixed2),
                  pl.BlockSpec((RET_HEADS, tb, tb), lambda b, t: (0, 0, 0)),
                  pl.BlockSpec((RET_QK, RET_WIDTH), fixed2),
                  pl.BlockSpec((RET_QK, RET_WIDTH), fixed2),
                  _layer_spec(layer, (1, RET_WIDTH), 2),
                  pl.BlockSpec((256, 256), fixed2),
                  pl.BlockSpec((128, 128), fixed2)],
        out_specs=[pl.BlockSpec((ns, tb, RET_WIDTH), tile),
                   pl.BlockSpec((ns, None, RET_QK, RET_WIDTH), lambda b, t: (0, b, 0, 0))],
        out_shape=[jax.ShapeDtypeStruct((ns, bsz // ns * tlen, RET_WIDTH), F32),
                   jax.ShapeDtypeStruct((ns, bsz // ns, RET_QK, RET_WIDTH), F32)],
        scratch_shapes=[pltpu.VMEM((ns, RET_QK, RET_WIDTH), F32)],
        compiler_params=pltpu.CompilerParams(
            dimension_semantics=("parallel", "arbitrary"), vmem_limit_bytes=VMEM_LIMIT),
        name="ret",
    )(_seq_major(rt_in), cos, sin, dq, dk, dmat, gam, bd, nw, ones256, ones128)
    return o.reshape(bsz * tlen, RET_WIDTH), st.reshape(bsz, RET_QK, RET_WIDTH)


def _cmul(ar, ai, xr, xi):
    return ar * xr - ai * xi, ar * xi + ai * xr


def _s5_body(u_ref, bblk_ref, cblk_ref, apow_ref, aend_ref, d_ref, gw_ref, gb_ref, y_ref, st_ref,
             x_scr):
    del st_ref
    npos = BLOCK_LEN
    sl = SUBLANES
    n = S5_N

    def power(m):
        return apow_ref[m - 1:m, 0:n], apow_ref[m - 1:m, n:2 * n]

    u = u_ref[...]
    bu = _dot(u.astype(BF16), bblk_ref[...])
    a1r, a1i = power(1)
    xr = [bu[0:sl, 0:n]]
    xi = [bu[0:sl, n:2 * n]]
    for p in range(1, npos):
        rows = slice(sl * p, sl * (p + 1))
        mr, mi = _cmul(a1r, a1i, xr[-1], xi[-1])
        xr.append(bu[rows, 0:n] + mr)
        xi.append(bu[rows, n:2 * n] + mi)

    yield
    blk = lax.broadcasted_iota(jnp.int32, (sl, n), 0)
    cr, ci = xr[-1], xi[-1]
    s = 1
    while s < sl:
        pr_, pi_ = power(npos * s)
        sr = jnp.where(blk >= s, pltpu.roll(cr, s, 0), 0.0)
        si = jnp.where(blk >= s, pltpu.roll(ci, s, 0), 0.0)
        mr, mi = _cmul(pr_, pi_, sr, si)
        cr, ci = cr + mr, ci + mi
        s *= 2
    x0r = x_scr[:, 0:n]
    x0i = x_scr[:, n:2 * n]
    mr, mi = _cmul(aend_ref[:, 0:n], aend_ref[:, n:2 * n], x0r, x0i)
    cr, ci = cr + mr, ci + mi
    x_scr[:, 0:n] = cr[sl - 1:sl]
    x_scr[:, n:2 * n] = ci[sl - 1:sl]
    inr = jnp.where(blk >= 1, pltpu.roll(cr, 1, 0), x0r)
    ini = jnp.where(blk >= 1, pltpu.roll(ci, 1, 0), x0i)
    rows_out = []
    for p in range(npos):
        pr_, pi_ = power(p + 1)
        mr, mi = _cmul(pr_, pi_, inr, ini)
        rows_out.append(jnp.concatenate([xr[p] + mr, xi[p] + mi], axis=1))
    xc = jnp.concatenate(rows_out, axis=0).astype(BF16)
    yield

    y = _dot(xc, cblk_ref[...]) + d_ref[...] * u
    y = jax.nn.gelu(y, approximate=True)
    y_ref[...] = y * _sigmoid(_dot(y.astype(BF16), gw_ref[...]) + gb_ref[...])


def _s5(u, bblk, cblk, apow, aend, dvec, gw, gb, layer, bsz, tlen):
    tb = TIME_BLOCK
    nt = tlen // tb
    ns = SEQS_PER_STEP
    tile = lambda b, t: (0, b * nt + t, 0)
    y, st = pl.pallas_call(
        _per_sequence(_s5_body, 7),
        grid=(bsz // ns, nt),
        in_specs=[pl.BlockSpec((ns, tb, S5_WIDTH), tile),
                  _layer_spec(layer, (S5_WIDTH, 2 * S5_N), 2),
                  _layer_spec(layer, (2 * S5_N, S5_WIDTH), 2),
                  _layer_spec(layer, (tb, 2 * S5_N), 2),
                  _layer_spec(layer, (SUBLANES, 2 * S5_N), 2),
                  _layer_spec(layer, (1, S5_WIDTH), 2),
                  _layer_spec(layer, (S5_WIDTH, S5_WIDTH), 2),
                  _layer_spec(layer, (1, S5_WIDTH), 2)],
        out_specs=[pl.BlockSpec((ns, tb, S5_WIDTH), tile),
                   pl.BlockSpec((ns, None, 1, 2 * S5_N), lambda b, t: (0, b, 0, 0))],
        out_shape=[jax.ShapeDtypeStruct((ns, bsz // ns * tlen, S5_WIDTH), F32),
                   jax.ShapeDtypeStruct((ns, bsz // ns, 1, 2 * S5_N), F32)],
        scratch_shapes=[pltpu.VMEM((ns, 1, 2 * S5_N), F32)],
        compiler_params=pltpu.CompilerParams(
            dimension_semantics=("parallel", "arbitrary"), vmem_limit_bytes=VMEM_LIMIT),
        name="s5",
    )(_seq_major(u), bblk, cblk, apow, aend, dvec, gw, gb)
    return y.reshape(bsz * tlen, S5_WIDTH), st.reshape(bsz, 1, 2 * S5_N)


def _hgrn_step_kernel(x_ref, lb_ref, nw_ref, s_ref, acc_ref, o_ref, sn_ref,
                      f_scr, k_scr, q_scr, v_scr, g_scr):
    del acc_ref
    h = pl.program_id(0)

    @pl.when(h == 0)
    def _():
        lb = lb_ref[...]
        z = x_ref[:, HG_WIDTH:2 * HG_WIDTH].T
        f_scr[...] = lb + (1.0 - lb) * _sigmoid(z)
        k_scr[...] = (1.0 - lb) * _sigmoid(-z)
        q_scr[...] = _silu(x_ref[:, 0:HG_WIDTH].T)
        v_scr[...] = x_ref[:, 2 * HG_WIDTH:3 * HG_WIDTH].T
        g_scr[...] = x_ref[:, 3 * HG_WIDTH:4 * HG_WIDTH].T

    rows = pl.ds(pl.multiple_of(h * HG_DK, HG_DK), HG_DK)
    f = f_scr[rows, :]
    kk = k_scr[rows, :]
    q = q_scr[rows, :]
    v = v_scr[rows, :]
    o = jnp.zeros_like(v)
    for k in range(HG_DK):
        sn = f[k:k + 1] * s_ref[k] + kk[k:k + 1] * v
        sn_ref[k] = sn
        o = o + q[k:k + 1] * sn
    ms = jnp.mean(o * o, axis=0, keepdims=True)
    o_ref[...] = o * lax.rsqrt(ms + EPS) * nw_ref[...] * _silu(g_scr[rows, :])


def _ret_step_kernel(x_ref, cos_ref, sin_ref, gam_ref, nw_ref, s_ref, acc_ref, o_ref, sn_ref,
                     qk_scr, v_scr, g_scr):
    del acc_ref
    half = RET_DK // 2
    h = pl.program_id(0)

    @pl.when(h == 0)
    def _():
        x = x_ref[:, 0:2 * RET_QK].T
        parts = []
        for i in range(2 * RET_HEADS):
            parts.append(x[RET_DK * i + half:RET_DK * (i + 1)])
            parts.append(x[RET_DK * i:RET_DK * i + half])
        qk_scr[...] = x * cos_ref[...] + jnp.concatenate(parts, axis=0) * sin_ref[...]
        v_scr[...] = x_ref[:, 2 * RET_QK:2 * RET_QK + RET_WIDTH].T
        g_scr[...] = x_ref[:, 2 * RET_QK + RET_WIDTH:2 * RET_QK + 2 * RET_WIDTH].T

    q = qk_scr[pl.ds(pl.multiple_of(h * RET_DK, RET_DK), RET_DK), :]
    kk = qk_scr[pl.ds(pl.multiple_of(RET_QK + h * RET_DK, RET_DK), RET_DK), :] * (RET_DK ** -0.5)
    vrows = pl.ds(pl.multiple_of(h * RET_DV, RET_DV), RET_DV)
    v = v_scr[vrows, :]
    gam = gam_ref[h]
    o = jnp.zeros_like(v)
    for k in range(RET_DK):
        sn = gam * s_ref[k] + kk[k:k + 1] * v
        sn_ref[k] = sn
        o = o + q[k:k + 1] * sn
    mu = jnp.mean(o, axis=0, keepdims=True)
    oc = o - mu
    var = jnp.mean(oc * oc, axis=0, keepdims=True)
    o_ref[...] = oc * lax.rsqrt(var + EPS) * nw_ref[vrows, :] * _silu(g_scr[vrows, :])


def _s5_step_kernel(u_ref, xr_ref, xi_ref, bt_ref, ct_ref, ab_ref, d_ref, gwt_ref, gb_ref,
                    accr_ref, acci_ref, y_ref, xrn_ref, xin_ref):
    del accr_ref, acci_ref
    n = S5_N
    ut = u_ref[...].T
    bu = _dot(bt_ref[...], ut.astype(BF16))
    mr, mi = _cmul(ab_ref[0:n], ab_ref[n:2 * n], xr_ref[...], xi_ref[...])
    xr = mr + bu[0:n]
    xi = mi + bu[n:2 * n]
    xrn_ref[...] = xr
    xin_ref[...] = xi
    y = _dot(ct_ref[...], jnp.concatenate([xr, xi], axis=0).astype(BF16)) + d_ref[...] * ut
    y = jax.nn.gelu(y, approximate=True)
    y_ref[...] = y * _sigmoid(_dot(gwt_ref[...], y.astype(BF16)) + gb_ref[...])


N_HG_STEP_IN, N_RT_STEP_IN, N_S5_STEP_IN = 5, 7, 11


def _sample_mixers_kernel(*refs):
    n_in = N_HG_STEP_IN + N_RT_STEP_IN + N_S5_STEP_IN
    ins, outs, scr = refs[:n_in], refs[n_in:n_in + 7], refs[n_in + 7:]
    hg_in = ins[:N_HG_STEP_IN]
    rt_in = ins[N_HG_STEP_IN:N_HG_STEP_IN + N_RT_STEP_IN]
    s5_in = ins[N_HG_STEP_IN + N_RT_STEP_IN:]
    _hgrn_step_kernel(*hg_in, outs[0], outs[1], *scr[:5])
    _ret_step_kernel(*rt_in, outs[2], outs[3], *scr[5:])

    @pl.when(pl.program_id(0) == 0)
    def _():
        _s5_step_kernel(*s5_in, *outs[4:])


def _sample_mixers(layer, proj_out, hg_par, rt_par, s5_par, states, accs):
    hg_x, rt_x, s5_x = proj_out
    n = hg_x.shape[0]
    whole = lambda shape: pl.BlockSpec(shape, lambda h: (0,) * len(shape))
    anyspace = pl.BlockSpec(memory_space=pl.ANY)
    hg_st = pl.BlockSpec((None, None, HG_DK, HG_DV, n), lambda h: (layer, h, 0, 0, 0))
    rt_st = pl.BlockSpec((None, None, RET_DK, RET_DV, n), lambda h: (layer, h, 0, 0, 0))
    s5_st = _layer_spec(layer, (S5_N, n), 1)
    in_specs = [
        whole((n, HG_COLS)), _layer_spec(layer, (HG_WIDTH, 1), 1), _layer_spec(layer, (HG_DV, 1), 1),
        hg_st, anyspace,
        whole((n, RT_COLS)), whole((2 * RET_QK, 1)), whole((2 * RET_QK, 1)),
        whole((RET_HEADS, 1, n)), _layer_spec(layer, (RET_WIDTH, 1), 1), rt_st, anyspace,
        whole((n, S5_WIDTH)), s5_st, s5_st,
        _layer_spec(layer, (2 * S5_N, S5_WIDTH), 1), _layer_spec(layer, (S5_WIDTH, 2 * S5_N), 1),
        _layer_spec(layer, (2 * S5_N, 1), 1), _layer_spec(layer, (S5_WIDTH, 1), 1),
        _layer_spec(layer, (S5_WIDTH, S5_WIDTH), 1), _layer_spec(layer, (S5_WIDTH, 1), 1),
        anyspace, anyspace]
    head_rows = pl.BlockSpec((HG_DV, n), lambda h: (h, 0))
    chan = pltpu.VMEM((HG_WIDTH, n), F32)
    acc_at = {N_HG_STEP_IN - 1: 1, N_HG_STEP_IN + N_RT_STEP_IN - 1: 3,
              N_HG_STEP_IN + N_RT_STEP_IN + N_S5_STEP_IN - 2: 5,
              N_HG_STEP_IN + N_RT_STEP_IN + N_S5_STEP_IN - 1: 6}
    return pl.pallas_call(
        _sample_mixers_kernel,
        grid=(HG_HEADS,),
        in_specs=in_specs,
        out_specs=[head_rows, hg_st, head_rows, rt_st, whole((S5_WIDTH, n)), s5_st, s5_st],
        out_shape=[jax.ShapeDtypeStruct((HG_WIDTH, n), F32),
                   jax.ShapeDtypeStruct(states[0].shape, F32),
                   jax.ShapeDtypeStruct((RET_WIDTH, n), F32),
                   jax.ShapeDtypeStruct(states[1].shape, F32),
                   jax.ShapeDtypeStruct((S5_WIDTH, n), F32),
                   jax.ShapeDtypeStruct(states[2].shape, F32),
                   jax.ShapeDtypeStruct(states[3].shape, F32)],
        scratch_shapes=[chan, chan, chan, chan, chan,
                        pltpu.VMEM((2 * RET_QK, n), F32), chan, chan],
        input_output_aliases=acc_at,
        compiler_params=pltpu.CompilerParams(
            dimension_semantics=("arbitrary",), vmem_limit_bytes=VMEM_LIMIT),
        name="sample_mixers",
    )(hg_x, *hg_par, states[0], accs[0],
      rt_x, *rt_par, states[1], accs[1],
      s5_x, states[2], states[3], *s5_par, accs[2], accs[3])


def _tile_order():
    rows = np.arange(TIME_BLOCK)
    return BLOCK_LEN * (rows % SUBLANES) + rows // SUBLANES


def _permute_tokens(x, inverse=False):
    bsz, tlen, dm = x.shape
    a, b = (BLOCK_LEN, SUBLANES) if inverse else (SUBLANES, BLOCK_LEN)
    x = x.reshape(bsz, tlen // TIME_BLOCK, a, b, dm)
    return jnp.swapaxes(x, 2, 3).reshape(bsz, tlen, dm)


def _rope_tables(pos):
    half = RET_DK // 2
    inv = ROPE_BASE ** (-np.arange(half, dtype=np.float64) / half)
    ang = np.asarray(pos, np.float64)[:, None] * inv[None, :]
    cos_h = np.concatenate([np.cos(ang), np.cos(ang)], axis=1)
    sin_h = np.concatenate([-np.sin(ang), np.sin(ang)], axis=1)
    return (np.tile(cos_h, (1, RET_HEADS)).astype(np.float32),
            np.tile(sin_h, (1, RET_HEADS)).astype(np.float32))


def _position_tables(tlen):
    tb = TIME_BLOCK
    order = _tile_order()
    pos_p = np.arange(tlen) // tb * tb + np.tile(order, tlen // tb)
    cos_p, sin_p = _rope_tables(pos_p)
    cos_s, sin_s = _rope_tables(np.array([PAST_LEN]))
    log_gamma = np.log1p(-np.exp2(-5.0 - np.arange(RET_HEADS, dtype=np.float64)))
    lg_lane = np.repeat(log_gamma, RET_DK)[None, :]
    tt = order.astype(np.float64)
    diff = tt[:, None] - tt[None, :]
    f32 = lambda a: jnp.asarray(np.asarray(a, np.float32))
    return dict(
        cos_qk=f32(np.concatenate([cos_p, cos_p], axis=1)),
        sin_qk=f32(np.concatenate([sin_p, sin_p], axis=1)),
        cos_col=f32(np.concatenate([cos_s, cos_s], axis=1).reshape(2 * RET_QK, 1)),
        sin_col=f32(np.concatenate([sin_s, sin_s], axis=1).reshape(2 * RET_QK, 1)),
        dq=f32(np.exp((tt[:, None] + 1.0) * lg_lane)),
        dk=f32(np.exp((tb - 1.0 - tt[:, None]) * lg_lane)),
        dmat=f32(np.where(diff >= 0, np.exp(np.maximum(diff, 0.0)[None] * log_gamma[:, None, None]), 0.0)),
        gam_tbl=f32(np.broadcast_to(np.exp(tb * np.repeat(log_gamma, RET_DK))[:, None],
                                    (RET_QK, RET_WIDTH))),
        gamma=np.exp(log_gamma),
    )


def _block_diag_mask(rows_per, cols_per, nblk):
    r = np.arange(rows_per * nblk)[:, None] // rows_per
    c = np.arange(cols_per * nblk)[None, :] // cols_per
    return (r == c).astype(np.float32)


def _s5_params(log_dt, a_re, a_im, b_re, b_im, c_re, c_im):
    nl = log_dt.shape[0]
    dt = jnp.exp(log_dt)[..., None]
    mag = jnp.exp(dt * a_re)
    ab_re = mag * jnp.cos(dt * a_im)
    ab_im = mag * jnp.sin(dt * a_im)
    den = a_re * a_re + a_im * a_im
    nr = ab_re - 1.0
    g_re = (nr * a_re + ab_im * a_im) / den
    g_im = (ab_im * a_re - nr * a_im) / den
    bb_re = g_re[..., None] * b_re - g_im[..., None] * b_im
    bb_im = g_re[..., None] * b_im + g_im[..., None] * b_re
    eye = jnp.eye(S5_GROUPS, dtype=F32)[None, :, None, :, None]

    def in_blk(bb):
        m = eye * jnp.swapaxes(bb, 2, 3)[:, :, :, None, :]
        return m.reshape(nl, S5_WIDTH, S5_N)

    def out_blk(c):
        m = eye * jnp.swapaxes(c, 2, 3)[:, :, :, None, :]
        return m.reshape(nl, S5_N, S5_WIDTH)

    bblk = jnp.concatenate([in_blk(bb_re), in_blk(bb_im)], axis=2)
    cblk = jnp.concatenate([out_blk(c_re), -out_blk(c_im)], axis=1)
    m = jnp.arange(1, TIME_BLOCK + 1, dtype=F32)[None, :, None, None]
    pmag = jnp.exp(m * (dt * a_re)[:, None])
    pang = m * (dt * a_im)[:, None]
    apow = jnp.concatenate([(pmag * jnp.cos(pang)).reshape(nl, TIME_BLOCK, S5_N),
                            (pmag * jnp.sin(pang)).reshape(nl, TIME_BLOCK, S5_N)], axis=2)
    ab = jnp.concatenate([ab_re.reshape(nl, S5_N, 1), ab_im.reshape(nl, S5_N, 1)], axis=1)
    return bblk, cblk, apow, ab


def kernel(x_prompt, x_sample, state_hgrn, state_ret, state_s5_re, state_s5_im, w_in, hgrn_lb_logits, hgrn_norm_w, ret_norm_w, s5_log_dt, s5_a_re, s5_a_im, s5_b_re, s5_b_im, s5_c_re, s5_c_im, s5_d, s5_glu_w, s5_glu_b, w_out, ln1_w, ln1_b, w_up, w_down, ln2_w, ln2_b):
    bp, tp, _ = x_prompt.shape
    bs = x_sample.shape[0]
    tb = TIME_BLOCK

    lb_prob = jax.nn.softmax(hgrn_lb_logits.astype(F32), axis=0)
    lower_bounds = jnp.cumsum(lb_prob, axis=0) - lb_prob[0:1]

    tbl = _position_tables(tp)
    ret_bd = jnp.asarray(_block_diag_mask(RET_DK, RET_DV, RET_HEADS))
    hg_bd = jnp.asarray(_block_diag_mask(HG_DV, HG_DK, 2))
    ones256 = jnp.asarray(_block_diag_mask(64, 64, 4)).astype(BF16)
    ones128 = jnp.asarray(_block_diag_mask(64, 64, 2)).astype(BF16)
    gam_lanes = jnp.asarray(np.broadcast_to(tbl["gamma"][:, None, None],
                                            (RET_HEADS, 1, bs)).astype(np.float32))

    wo = w_out.astype(BF16)
    ffn_par = (wo, ln1_w[:, None, :], ln1_b[:, None, :], w_up.astype(BF16), w_down.astype(BF16),
               ln2_w[:, None, :], ln2_b[:, None, :])
    lb_row = lower_bounds[:, None, :]
    hnw_row = jnp.tile(hgrn_norm_w, (1, HG_HEADS))[:, None, :]
    rnw_row = ret_norm_w[:, None, :]
    bblk, cblk, apow, ab_col = _s5_params(s5_log_dt, s5_a_re, s5_a_im, s5_b_re, s5_b_im,
                                          s5_c_re, s5_c_im)
    s5_par = (bblk.astype(BF16), cblk.astype(BF16), apow, apow[:, BLOCK_LEN - 1::BLOCK_LEN],
              s5_d[:, None, :], s5_glu_w.astype(BF16), s5_glu_b[:, None, :])
    s5_step_par = (jnp.swapaxes(s5_par[0], 1, 2), jnp.swapaxes(s5_par[1], 1, 2), ab_col,
                   s5_d[:, :, None],
                   jnp.swapaxes(s5_glu_w, 1, 2).astype(BF16), s5_glu_b[:, :, None])

    hg_s = jnp.transpose(state_hgrn, (0, 2, 3, 4, 1))
    rt_s = jnp.transpose(state_ret, (0, 2, 3, 4, 1))
    re_s = jnp.transpose(state_s5_re, (0, 2, 3, 1)).reshape(DEPTH, S5_N, bs)
    im_s = jnp.transpose(state_s5_im, (0, 2, 3, 1)).reshape(DEPTH, S5_N, bs)
    new_hg_s = jnp.zeros(hg_s.shape, F32)
    new_rt_s = jnp.zeros(rt_s.shape, F32)
    new_re_s = jnp.zeros(re_s.shape, F32)
    new_im_s = jnp.zeros(im_s.shape, F32)

    xp = _permute_tokens(x_prompt).reshape(bp * tp, D_MODEL)
    xs = x_sample.reshape(bs, D_MODEL)
    outs = {k: [] for k in ("hg_p", "rt_p", "s5_p")}
    for l in range(DEPTH):
        hg_in, rt_in, s5_in, hg_in_s, rt_in_s, s5_in_s = _proj(xp, xs, w_in, l, 512)
        o_hg, st_hg = _hgrn(hg_in, lb_row, hnw_row, ones256, ones128, hg_bd, l, bp, tp)
        o_rt, st_rt = _ret(rt_in, tbl["cos_qk"], tbl["sin_qk"], tbl["dq"], tbl["dk"], tbl["dmat"],
                           tbl["gam_tbl"], ret_bd, rnw_row, ones256, ones128, l, bp, tp)
        y5, st_s5 = _s5(s5_in, *s5_par, l, bp, tp)
        outs["hg_p"].append(st_hg)
        outs["rt_p"].append(st_rt)
        outs["s5_p"].append(st_s5)
        o_hg_t, new_hg_s, o_rt_t, new_rt_s, y5_t, new_re_s, new_im_s = _sample_mixers(
            l, (hg_in_s, rt_in_s, s5_in_s),
            (lower_bounds[:, :, None], hgrn_norm_w[:, :, None]),
            (tbl["cos_col"], tbl["sin_col"], gam_lanes, ret_norm_w[:, :, None]),
            s5_step_par, (hg_s, rt_s, re_s, im_s), (new_hg_s, new_rt_s, new_re_s, new_im_s))
        xp, xs = _ffn((xp, o_hg, o_rt, y5), (xs, o_hg_t, o_rt_t, y5_t), *ffn_par, l, 512)

    yp = _permute_tokens(xp.reshape(bp, tp, D_MODEL), inverse=True)

    def s5_out(a):
        return jnp.transpose(a.reshape(DEPTH, S5_GROUPS, S5_STATE, bs), (0, 3, 1, 2))

    hg7 = jnp.stack(outs["hg_p"]).reshape(DEPTH, bp, HG_HEADS // 2, 2, HG_DV, 2, HG_DK)
    hg_p = jnp.stack([hg7[:, :, :, e, :, e, :] for e in range(2)], axis=3)
    hg_p = jnp.swapaxes(hg_p, -1, -2).reshape(DEPTH, bp, HG_HEADS, HG_DK, HG_DV)
    rt6 = jnp.stack(outs["rt_p"]).reshape(DEPTH, bp, RET_HEADS, RET_DK, RET_HEADS, RET_DV)
    rt_p = jnp.stack([rt6[:, :, h, :, h, :] for h in range(RET_HEADS)], axis=2)
    s5_p = jnp.stack(outs["s5_p"]).reshape(DEPTH, bp, 2, S5_GROUPS, S5_STATE)

    return (yp, xs.reshape(bs, 1, D_MODEL), hg_p, rt_p, s5_p[:, :, 0], s5_p[:, :, 1],
            jnp.transpose(new_hg_s, (0, 4, 1, 2, 3)), jnp.transpose(new_rt_s, (0, 4, 1, 2, 3)),
            s5_out(new_re_s), s5_out(new_im_s))
```

```python
import jax
import jax.numpy as jnp
import numpy as np
from jax import lax
from jax.experimental import pallas as pl
from jax.experimental.pallas import tpu as pltpu

F32 = jnp.float32
BF16 = jnp.bfloat16

D_MODEL = 1024
DEPTH = 4
PAST_LEN = 16384
HG_WIDTH = 384
HG_HEADS = 6
HG_DK = 64
HG_DV = 64
RET_WIDTH = 384
RET_HEADS = 6
RET_DK = 32
RET_DV = 64
RET_QK = RET_HEADS * RET_DK
S5_WIDTH = 256
S5_GROUP = 16
S5_GROUPS = 16
S5_STATE = 64
S5_N = S5_GROUPS * S5_STATE
D_FF = 4 * D_MODEL
ROPE_BASE = 10000.0
EPS = 1e-5
ALPHA = (2.0 * DEPTH) ** 0.25

HG_COLS = 4 * HG_WIDTH
RT_COLS = 2 * RET_QK + 2 * RET_WIDTH
PROJ_COLS = HG_COLS + RT_COLS + S5_WIDTH
LANES = 128

SUBLANES = 8
TIME_BLOCK = 128
BLOCK_LEN = TIME_BLOCK // SUBLANES
SEQS_PER_STEP = 8
FF_CHUNK = 1024
FFN_ROW_GROUPS = 2
VMEM_LIMIT = 56 * 1024 * 1024


def _dot(a, b):
    return jnp.dot(a, b, preferred_element_type=F32)


def _dot_nt(a, b):
    return lax.dot_general(a, b, (((1,), (1,)), ((), ())), preferred_element_type=F32)


def _dot_tn(a, b):
    return lax.dot_general(a, b, (((0,), (0,)), ((), ())), preferred_element_type=F32)


def _sigmoid(x):
    return jax.nn.sigmoid(x)


def _silu(x):
    return x * jax.nn.sigmoid(x)


def _layer_norm(y, w, b):
    mu = jnp.mean(y, -1, keepdims=True)
    yc = y - mu
    var = jnp.mean(yc * yc, -1, keepdims=True)
    return yc * lax.rsqrt(var + EPS) * w + b


def _head_sum_bf16(x, ones256, ones128):
    return jnp.concatenate([_dot(x[:, 0:256], ones256), _dot(x[:, 256:384], ones128)], axis=1)


def _head_sum(x, ones256, ones128):
    hi = x.astype(BF16)
    lo = (x - hi.astype(F32)).astype(BF16)
    return _head_sum_bf16(hi, ones256, ones128) + _head_sum_bf16(lo, ones256, ones128)


def _proj_rows(x_ref, w_ref, hg_ref, rt_ref, s5_ref):
    x = x_ref[...].astype(BF16)
    hg_ref[...] = _dot(x, w_ref[:, 0:HG_COLS].astype(BF16))
    rt_ref[...] = _dot(x, w_ref[:, HG_COLS:HG_COLS + RT_COLS].astype(BF16))
    s5_ref[...] = _dot(x, w_ref[:, HG_COLS + RT_COLS:PROJ_COLS].astype(BF16))


def _proj_kernel(xp_ref, xs_ref, w_ref, hgp_ref, rtp_ref, s5p_ref, hgs_ref, rts_ref, s5s_ref):
    i = pl.program_id(0)
    last = pl.num_programs(0) - 1

    @pl.when(i < last)
    def _():
        _proj_rows(xp_ref, w_ref, hgp_ref, rtp_ref, s5p_ref)

    @pl.when(i == last)
    def _():
        _proj_rows(xs_ref, w_ref, hgs_ref, rts_ref, s5s_ref)


def _layer_spec(layer, shape, grid_rank, **kw):
    zeros = (0,) * len(shape)
    if grid_rank == 1:
        return pl.BlockSpec((None,) + shape, lambda i: (layer,) + zeros, **kw)
    return pl.BlockSpec((None,) + shape, lambda b, t: (layer,) + zeros, **kw)


def _two_path_specs(n_tiles, tm, ms, widths):
    tile = lambda i: (jnp.minimum(i, n_tiles - 1), 0)
    return ([pl.BlockSpec((tm, w), tile) for w in widths],
            [pl.BlockSpec((ms, w), lambda i: (0, 0)) for w in widths])


def _proj(xp, xs, w, layer, tm):
    m, ms = xp.shape[0], xs.shape[0]
    n_tiles = m // tm
    widths = (HG_COLS, RT_COLS, S5_WIDTH)
    (xp_spec,), (xs_spec,) = _two_path_specs(n_tiles, tm, ms, (D_MODEL,))
    outp, outs = _two_path_specs(n_tiles, tm, ms, widths)
    return pl.pallas_call(
        _proj_kernel,
        grid=(n_tiles + 1,),
        in_specs=[xp_spec, xs_spec,
                  _layer_spec(layer, (D_MODEL, PROJ_COLS), 1, pipeline_mode=pl.Buffered(1))],
        out_specs=outp + outs,
        out_shape=([jax.ShapeDtypeStruct((m, c), F32) for c in widths]
                   + [jax.ShapeDtypeStruct((ms, c), F32) for c in widths]),
        compiler_params=pltpu.CompilerParams(
            dimension_semantics=("arbitrary",), vmem_limit_bytes=VMEM_LIMIT),
        name="proj",
    )(xp, xs, w)


def _ffn_rows(rows, channel_major, x_ref, a_ref, b_ref, c_ref, wo_ref, l1w_ref, l1b_ref, wu_ref,
              wd_ref, l2w_ref, l2b_ref, o_ref):
    take = (lambda r: r[...].T) if channel_major else (lambda r: r[rows, :])
    x = x_ref[rows, :]
    mixed = (_dot(take(a_ref).astype(BF16), wo_ref[0:HG_WIDTH, :])
             + _dot(take(b_ref).astype(BF16), wo_ref[HG_WIDTH:HG_WIDTH + RET_WIDTH, :])
             + _dot(take(c_ref).astype(BF16), wo_ref[HG_WIDTH + RET_WIDTH:D_MODEL, :]))
    yield
    x1 = _layer_norm(ALPHA * x + mixed, l1w_ref[...], l1b_ref[...])
    xb = x1.astype(BF16)
    yield
    ff = None
    for c in range(D_FF // FF_CHUNK):
        h = _dot(xb, wu_ref[:, c * FF_CHUNK:(c + 1) * FF_CHUNK])
        yield
        h = jnp.square(jnp.maximum(h, 0.0)).astype(BF16)
        yield
        d = _dot(h, wd_ref[c * FF_CHUNK:(c + 1) * FF_CHUNK, :])
        ff = d if ff is None else ff + d
        yield
    o_ref[rows, :] = _layer_norm(ALPHA * x1 + ff, l2w_ref[...], l2b_ref[...])


def _ffn_kernel(xp_ref, ap_ref, bp_ref, cp_ref, xs_ref, as_ref, bs_ref, cs_ref, *refs):
    weights, (op_ref, os_ref) = refs[:-2], refs[-2:]
    i = pl.program_id(0)
    last = pl.num_programs(0) - 1

    @pl.when(i < last)
    def _():
        _ffn_tile(FFN_ROW_GROUPS, False, xp_ref, ap_ref, bp_ref, cp_ref, *weights, op_ref)

    @pl.when(i == last)
    def _():
        _ffn_tile(1, True, xs_ref, as_ref, bs_ref, cs_ref, *weights, os_ref)


def _ffn_tile(groups, channel_major, x_ref, *refs):
    step = x_ref.shape[0] // groups
    live = [_ffn_rows(slice(g * step, (g + 1) * step), channel_major, x_ref, *refs)
            for g in range(groups)]
    for g in range(groups):
        for gen in live[:g]:
            next(gen, _DONE)
    while live:
        live = [gen for gen in live if next(gen, _DONE) is not _DONE]


def _ffn(prompt, sample, wo, l1w, l1b, wu, wd, l2w, l2b, layer, tm):
    m, ms = prompt[0].shape[0], sample[0].shape[0]
    n_tiles = m // tm
    widths = (D_MODEL, HG_WIDTH, RET_WIDTH, S5_WIDTH)
    inp, ins = _two_path_specs(n_tiles, tm, ms, widths)
    ins = ins[:1] + [pl.BlockSpec((w, ms), lambda i: (0, 0)) for w in widths[1:]]
    (outp,), (outs,) = _two_path_specs(n_tiles, tm, ms, (D_MODEL,))
    once = pl.Buffered(1)
    vec = _layer_spec(layer, (1, D_MODEL), 1)
    return pl.pallas_call(
        _ffn_kernel,
        grid=(n_tiles + 1,),
        in_specs=inp + ins + [
            _layer_spec(layer, (D_MODEL, D_MODEL), 1, pipeline_mode=once),
            vec, vec,
            _layer_spec(layer, (D_MODEL, D_FF), 1, pipeline_mode=once),
            _layer_spec(layer, (D_FF, D_MODEL), 1, pipeline_mode=once),
            vec, vec],
        out_specs=[outp, outs],
        out_shape=[jax.ShapeDtypeStruct((m, D_MODEL), F32),
                   jax.ShapeDtypeStruct((ms, D_MODEL), F32)],
        compiler_params=pltpu.CompilerParams(
            dimension_semantics=("arbitrary",), vmem_limit_bytes=VMEM_LIMIT),
        name="ffn",
    )(*prompt, *sample, wo, l1w, l1b, wu, wd, l2w, l2b)


_DONE = object()


def _copy_state(state_out, state_scr):
    state_out[...] = state_scr[...]


def _hgrn_state_out(st_ref, s_scr):
    for i in range(SEQS_PER_STEP):
        for h in range(HG_HEADS):
            e = h % 2
            st_ref[i, h] = s_scr[i, h // 2, HG_DV * e:HG_DV * (e + 1), HG_DK * e:HG_DK * (e + 1)].T


def _ret_state_out(st_ref, s_scr):
    for i in range(SEQS_PER_STEP):
        for h in range(RET_HEADS):
            st_ref[i, h] = s_scr[i, RET_DK * h:RET_DK * (h + 1), RET_DV * h:RET_DV * (h + 1)]


def _per_sequence(body, n_shared_in, write_state=_copy_state):
    def kern(x_ref, *refs):
        shared = refs[:n_shared_in]
        per_seq = refs[n_shared_in:]
        state_out, state_scr = per_seq[1], per_seq[2]
        t = pl.program_id(1)

        @pl.when(t == 0)
        def _():
            state_scr[...] = jnp.zeros(state_scr.shape, F32)

        live = [body(x_ref.at[i], *shared, *(r.at[i] for r in per_seq))
                for i in range(SEQS_PER_STEP)]
        while live:
            live = [g for g in live if next(g, _DONE) is not _DONE]

        @pl.when(t == pl.num_programs(1) - 1)
        def _():
            write_state(state_out, state_scr)
    return kern


def _seq_major(a):
    return a.reshape(SEQS_PER_STEP, a.shape[0] // SEQS_PER_STEP, a.shape[1])


def _hgrn_body(x_ref, lb_ref, nw_ref, ones256_ref, ones128_ref, bd_ref, o_ref, st_ref,
               s_scr, qt_scr, kt_scr, v_scr, oi_scr):
    del st_ref
    npos = BLOCK_LEN
    sl = SUBLANES
    lb = lb_ref[...]
    ones256 = ones256_ref[...]
    ones128 = ones128_ref[...]
    f, kk, q, v = [], [], [], []
    for p in range(npos):
        rows = slice(sl * p, sl * (p + 1))
        hq = x_ref[rows, 0:HG_WIDTH]
        z = x_ref[rows, HG_WIDTH:2 * HG_WIDTH]
        f.append(lb + (1.0 - lb) * _sigmoid(z))
        kk.append(1.0 - f[-1])
        q.append(_silu(hq))
        v.append(x_ref[rows, 2 * HG_WIDTH:3 * HG_WIDTH])
        if p % 4 == 3:
            yield

    g = list(kk)
    e_rows = []
    for d in range(npos):
        if d > 0:
            g = [None] * d + [f[p] * g[p - 1] for p in range(d, npos)]
        e_rows.extend(q[p] * g[p] for p in range(d, npos))
        if d % 4 == 3:
            yield
    e_all = jnp.concatenate(e_rows, axis=0).astype(BF16)
    p_all = _head_sum_bf16(e_all, ones256, ones128)
    yield
    o = [None] * npos
    i = 0
    for d in range(npos):
        for p in range(d, npos):
            term = p_all[sl * i:sl * (i + 1)] * v[p - d]
            o[p] = term if o[p] is None else o[p] + term
            i += 1
        if d % 4 == 3:
            yield

    a = [f[0]]
    for p in range(1, npos):
        a.append(a[-1] * f[p])
    r = [None] * npos
    r[npos - 1] = jnp.ones_like(f[0])
    for p in range(npos - 2, -1, -1):
        r[p] = r[p + 1] * f[p + 1]
    npair = HG_HEADS // 2
    for p in range(npos):
        rows = slice(sl * p, sl * (p + 1))
        qt = q[p] * a[p]
        kt = kk[p] * r[p]
        for pr in range(npair):
            lanes = slice(128 * pr, 128 * (pr + 1))
            qt_scr[pr, rows, :] = qt[:, lanes]
            kt_scr[pr, rows, :] = kt[:, lanes]
            v_scr[pr, rows, :] = v[p][:, lanes]
    cd = a[npos - 1]
    for j in range(sl):
        blk = pl.ds(j, npos, stride=sl)
        for pr in range(npair):
            lanes = slice(128 * pr, 128 * (pr + 1))
            st = s_scr[pr]
            oi_scr[pr, blk, :] = _dot_nt(qt_scr[pr, blk, :].astype(BF16), st.astype(BF16))
            u = _dot_tn(v_scr[pr, blk, :].astype(BF16), kt_scr[pr, blk, :].astype(BF16))
            s_scr[pr] = st * cd[j:j + 1, lanes] + u * bd_ref[...]
        yield

    ot = (jnp.concatenate(o, axis=0)
          + jnp.concatenate([oi_scr[pr] for pr in range(npair)], axis=1))
    ms = _head_sum(ot * ot, ones256, ones128) * (1.0 / HG_DV)
    gate = x_ref[:, 3 * HG_WIDTH:4 * HG_WIDTH]
    o_ref[...] = ot * lax.rsqrt(ms + EPS) * nw_ref[...] * _silu(gate)


def _hgrn(hg_in, lb, nw, ones256, ones128, bd, layer, bsz, tlen):
    tb = TIME_BLOCK
    nt = tlen // tb
    ns = SEQS_PER_STEP
    npair = HG_HEADS // 2
    fixed2 = lambda b, t: (0, 0)
    tile = lambda b, t: (0, b * nt + t, 0)
    slab = pltpu.VMEM((ns, npair, tb, 128), F32)
    o, st = pl.pallas_call(
        _per_sequence(_hgrn_body, 5, _hgrn_state_out),
        grid=(bsz // ns, nt),
        in_specs=[pl.BlockSpec((ns, tb, HG_COLS), tile),
                  _layer_spec(layer, (1, HG_WIDTH), 2),
                  _layer_spec(layer, (1, HG_WIDTH), 2),
                  pl.BlockSpec((256, 256), fixed2),
                  pl.BlockSpec((128, 128), fixed2),
                  pl.BlockSpec((128, 128), fixed2)],
        out_specs=[pl.BlockSpec((ns, tb, HG_WIDTH), tile),
                   pl.BlockSpec((ns, None, HG_HEADS, HG_DK, HG_DV), lambda b, t: (0, b, 0, 0, 0))],
        out_shape=[jax.ShapeDtypeStruct((ns, bsz // ns * tlen, HG_WIDTH), F32),
                   jax.ShapeDtypeStruct((ns, bsz // ns, HG_HEADS, HG_DK, HG_DV), F32)],
        scratch_shapes=[pltpu.VMEM((ns, npair, 128, 128), F32), slab, slab, slab, slab],
        compiler_params=pltpu.CompilerParams(
            dimension_semantics=("parallel", "arbitrary"), vmem_limit_bytes=VMEM_LIMIT),
        name="hgrn",
    )(_seq_major(hg_in), lb, nw, ones256, ones128, bd)
    return o.reshape(bsz * tlen, HG_WIDTH), st.reshape(bsz, HG_HEADS, HG_DK, HG_DV)


def _ret_body(x_ref, cos_ref, sin_ref, dq_ref, dk_ref, dmat_ref, gam_ref, bd_ref, nw_ref,
              ones256_ref, ones128_ref, o_ref, st_ref, s_scr):
    del st_ref
    tb = TIME_BLOCK
    w = RET_QK
    half = RET_DK // 2
    in_first_half = (lax.broadcasted_iota(jnp.int32, (tb, LANES), 1) & (RET_DK - 1)) < half
    rot = []
    for c in range(2 * w // LANES):
        lanes = slice(LANES * c, LANES * (c + 1))
        x = x_ref[:, lanes]
        partner = jnp.where(in_first_half, pltpu.roll(x, LANES - half, 1), pltpu.roll(x, half, 1))
        rot.append(x * cos_ref[:, lanes] + partner * sin_ref[:, lanes])
    rot = jnp.concatenate(rot, axis=1)
    q = rot[:, 0:w]
    k = rot[:, w:2 * w] * (RET_DK ** -0.5)
    v = x_ref[:, 2 * w:2 * w + RET_WIDTH]
    gate = x_ref[:, 2 * w + RET_WIDTH:2 * w + 2 * RET_WIDTH]
    vb = v.astype(BF16)

    lane = lax.broadcasted_iota(jnp.int32, (tb, w), 1)
    qs = jnp.concatenate(
        [jnp.where((lane >= RET_DK * h) & (lane < RET_DK * (h + 1)), q, 0.0)
         for h in range(RET_HEADS)], axis=0).astype(BF16)
    yield
    sc = _dot_nt(qs, k.astype(BF16))
    yield
    lane128 = lax.broadcasted_iota(jnp.int32, (tb, 128), 1)
    outs = []
    for p in range(RET_HEADS // 2):
        p0 = sc[(2 * p) * tb:(2 * p + 1) * tb] * dmat_ref[2 * p]
        p1 = sc[(2 * p + 1) * tb:(2 * p + 2) * tb] * dmat_ref[2 * p + 1]
        pc = jnp.concatenate([p0, p1], axis=1).astype(BF16)
        vp = v[:, 128 * p:128 * (p + 1)]
        vs = jnp.concatenate([jnp.where(lane128 < RET_DV, vp, 0.0),
                              jnp.where(lane128 >= RET_DV, vp, 0.0)], axis=0).astype(BF16)
        outs.append(_dot(pc, vs))
    o = jnp.concatenate(outs, axis=1)
    yield

    s = s_scr[...]
    o = o + _dot((q * dq_ref[...]).astype(BF16), s.astype(BF16))
    u = _dot_tn((k * dk_ref[...]).astype(BF16), vb)
    s_scr[...] = s * gam_ref[...] + u * bd_ref[...]

    ones256 = ones256_ref[...]
    ones128 = ones128_ref[...]
    mu = _head_sum(o, ones256, ones128) * (1.0 / RET_DV)
    oc = o - mu
    var = _head_sum(oc * oc, ones256, ones128) * (1.0 / RET_DV)
    o_ref[...] = oc * lax.rsqrt(var + EPS) * nw_ref[...] * _silu(gate)


def _ret(rt_in, cos, sin, dq, dk, dmat, gam, bd, nw, ones256, ones128, layer, bsz, tlen):
    tb = TIME_BLOCK
    nt = tlen // tb
    ns = SEQS_PER_STEP
    fixed2 = lambda b, t: (0, 0)
    tile = lambda b, t: (0, b * nt + t, 0)
    o, st = pl.pallas_call(
        _per_sequence(_ret_body, 10, _ret_state_out),
        grid=(bsz // ns, nt),
        in_specs=[pl.BlockSpec((ns, tb, RT_COLS), tile),
                  pl.BlockSpec((tb, 2 * RET_QK), lambda b, t: (t, 0)),
                  pl.BlockSpec((tb, 2 * RET_QK), lambda b, t: (t, 0)),
                  pl.BlockSpec((tb, RET_QK), fixed2),
                  pl.BlockSpec((tb, RET_QK), fixed2),
                  pl.BlockSpec((RET_HEADS, tb, tb), lambda b, t: (0, 0, 0)),
                  pl.BlockSpec((RET_QK, RET_WIDTH), fixed2),
                  pl.BlockSpec((RET_QK, RET_WIDTH), fixed2),
                  _layer_spec(layer, (1, RET_WIDTH), 2),
                  pl.BlockSpec((256, 256), fixed2),
                  pl.BlockSpec((128, 128), fixed2)],
        out_specs=[pl.BlockSpec((ns, tb, RET_WIDTH), tile),
                   pl.BlockSpec((ns, None, RET_HEADS, RET_DK, RET_DV), lambda b, t: (0, b, 0, 0, 0))],
        out_shape=[jax.ShapeDtypeStruct((ns, bsz // ns * tlen, RET_WIDTH), F32),
                   jax.ShapeDtypeStruct((ns, bsz // ns, RET_HEADS, RET_DK, RET_DV), F32)],
        scratch_shapes=[pltpu.VMEM((ns, RET_QK, RET_WIDTH), F32)],
        compiler_params=pltpu.CompilerParams(
            dimension_semantics=("parallel", "arbitrary"), vmem_limit_bytes=VMEM_LIMIT),
        name="ret",
    )(_seq_major(rt_in), cos, sin, dq, dk, dmat, gam, bd, nw, ones256, ones128)
    return o.reshape(bsz * tlen, RET_WIDTH), st.reshape(bsz, RET_HEADS, RET_DK, RET_DV)


def _cmul(ar, ai, xr, xi):
    return ar * xr - ai * xi, ar * xi + ai * xr


def _s5_body(u_ref, bblk_ref, cblk_ref, apow_ref, aend_ref, d_ref, gw_ref, gb_ref, y_ref, st_ref,
             x_scr):
    del st_ref
    npos = BLOCK_LEN
    sl = SUBLANES
    n = S5_N

    def power(m):
        return apow_ref[m - 1:m, 0:n], apow_ref[m - 1:m, n:2 * n]

    u = u_ref[...]
    bu = _dot(u.astype(BF16), bblk_ref[...])
    a1r, a1i = power(1)
    xr = [bu[0:sl, 0:n]]
    xi = [bu[0:sl, n:2 * n]]
    for p in range(1, npos):
        rows = slice(sl * p, sl * (p + 1))
        mr, mi = _cmul(a1r, a1i, xr[-1], xi[-1])
        xr.append(bu[rows, 0:n] + mr)
        xi.append(bu[rows, n:2 * n] + mi)

    yield
    blk = lax.broadcasted_iota(jnp.int32, (sl, n), 0)
    cr, ci = xr[-1], xi[-1]
    s = 1
    while s < sl:
        pr_, pi_ = power(npos * s)
        sr = jnp.where(blk >= s, pltpu.roll(cr, s, 0), 0.0)
        si = jnp.where(blk >= s, pltpu.roll(ci, s, 0), 0.0)
        mr, mi = _cmul(pr_, pi_, sr, si)
        cr, ci = cr + mr, ci + mi
        s *= 2
    x0r = x_scr[:, 0:n]
    x0i = x_scr[:, n:2 * n]
    mr, mi = _cmul(aend_ref[:, 0:n], aend_ref[:, n:2 * n], x0r, x0i)
    cr, ci = cr + mr, ci + mi
    x_scr[:, 0:n] = cr[sl - 1:sl]
    x_scr[:, n:2 * n] = ci[sl - 1:sl]
    inr = jnp.where(blk >= 1, pltpu.roll(cr, 1, 0), x0r)
    ini = jnp.where(blk >= 1, pltpu.roll(ci, 1, 0), x0i)
    rows_out = []
    for p in range(npos):
        pr_, pi_ = power(p + 1)
        mr, mi = _cmul(pr_, pi_, inr, ini)
        rows_out.append(jnp.concatenate([xr[p] + mr, xi[p] + mi], axis=1))
    xc = jnp.concatenate(rows_out, axis=0).astype(BF16)
    yield

    y = _dot(xc, cblk_ref[...]) + d_ref[...] * u
    y = jax.nn.gelu(y, approximate=True)
    y_ref[...] = y * _sigmoid(_dot(y.astype(BF16), gw_ref[...]) + gb_ref[...])


def _s5(u, bblk, cblk, apow, aend, dvec, gw, gb, layer, bsz, tlen):
    tb = TIME_BLOCK
    nt = tlen // tb
    ns = SEQS_PER_STEP
    tile = lambda b, t: (0, b * nt + t, 0)
    y, st = pl.pallas_call(
        _per_sequence(_s5_body, 7),
        grid=(bsz // ns, nt),
        in_specs=[pl.BlockSpec((ns, tb, S5_WIDTH), tile),
                  _layer_spec(layer, (S5_WIDTH, 2 * S5_N), 2),
                  _layer_spec(layer, (2 * S5_N, S5_WIDTH), 2),
                  _layer_spec(layer, (tb, 2 * S5_N), 2),
                  _layer_spec(layer, (SUBLANES, 2 * S5_N), 2),
                  _layer_spec(layer, (1, S5_WIDTH), 2),
                  _layer_spec(layer, (S5_WIDTH, S5_WIDTH), 2),
                  _layer_spec(layer, (1, S5_WIDTH), 2)],
        out_specs=[pl.BlockSpec((ns, tb, S5_WIDTH), tile),
                   pl.BlockSpec((ns, None, 1, 2 * S5_N), lambda b, t: (0, b, 0, 0))],
        out_shape=[jax.ShapeDtypeStruct((ns, bsz // ns * tlen, S5_WIDTH), F32),
                   jax.ShapeDtypeStruct((ns, bsz // ns, 1, 2 * S5_N), F32)],
        scratch_shapes=[pltpu.VMEM((ns, 1, 2 * S5_N), F32)],
        compiler_params=pltpu.CompilerParams(
            dimension_semantics=("parallel", "arbitrary"), vmem_limit_bytes=VMEM_LIMIT),
        name="s5",
    )(_seq_major(u), bblk, cblk, apow, aend, dvec, gw, gb)
    return y.reshape(bsz * tlen, S5_WIDTH), st.reshape(bsz, 1, 2 * S5_N)


def _hgrn_step_kernel(x_ref, lb_ref, nw_ref, s_ref, acc_ref, o_ref, sn_ref,
                      f_scr, k_scr, q_scr, v_scr, g_scr):
    del acc_ref
    h = pl.program_id(0)

    @pl.when(h == 0)
    def _():
        lb = lb_ref[...]
        z = x_ref[:, HG_WIDTH:2 * HG_WIDTH].T
        f_scr[...] = lb + (1.0 - lb) * _sigmoid(z)
        k_scr[...] = (1.0 - lb) * _sigmoid(-z)
        q_scr[...] = _silu(x_ref[:, 0:HG_WIDTH].T)
        v_scr[...] = x_ref[:, 2 * HG_WIDTH:3 * HG_WIDTH].T
        g_scr[...] = x_ref[:, 3 * HG_WIDTH:4 * HG_WIDTH].T

    rows = pl.ds(pl.multiple_of(h * HG_DK, HG_DK), HG_DK)
    f = f_scr[rows, :]
    kk = k_scr[rows, :]
    q = q_scr[rows, :]
    v = v_scr[rows, :]
    o = jnp.zeros_like(v)
    for k in range(HG_DK):
        sn = f[k:k + 1] * s_ref[k] + kk[k:k + 1] * v
        sn_ref[k] = sn
        o = o + q[k:k + 1] * sn
    ms = jnp.mean(o * o, axis=0, keepdims=True)
    o_ref[...] = o * lax.rsqrt(ms + EPS) * nw_ref[...] * _silu(g_scr[rows, :])


def _ret_step_kernel(x_ref, cos_ref, sin_ref, gam_ref, nw_ref, s_ref, acc_ref, o_ref, sn_ref,
                     qk_scr, v_scr, g_scr):
    del acc_ref
    half = RET_DK // 2
    h = pl.program_id(0)

    @pl.when(h == 0)
    def _():
        x = x_ref[:, 0:2 * RET_QK].T
        parts = []
        for i in range(2 * RET_HEADS):
            parts.append(x[RET_DK * i + half:RET_DK * (i + 1)])
            parts.append(x[RET_DK * i:RET_DK * i + half])
        qk_scr[...] = x * cos_ref[...] + jnp.concatenate(parts, axis=0) * sin_ref[...]
        v_scr[...] = x_ref[:, 2 * RET_QK:2 * RET_QK + RET_WIDTH].T
        g_scr[...] = x_ref[:, 2 * RET_QK + RET_WIDTH:2 * RET_QK + 2 * RET_WIDTH].T

    q = qk_scr[pl.ds(pl.multiple_of(h * RET_DK, RET_DK), RET_DK), :]
    kk = qk_scr[pl.ds(pl.multiple_of(RET_QK + h * RET_DK, RET_DK), RET_DK), :] * (RET_DK ** -0.5)
    vrows = pl.ds(pl.multiple_of(h * RET_DV, RET_DV), RET_DV)
    v = v_scr[vrows, :]
    gam = gam_ref[h]
    o = jnp.zeros_like(v)
    for k in range(RET_DK):
        sn = gam * s_ref[k] + kk[k:k + 1] * v
        sn_ref[k] = sn
        o = o + q[k:k + 1] * sn
    mu = jnp.mean(o, axis=0, keepdims=True)
    oc = o - mu
    var = jnp.mean(oc * oc, axis=0, keepdims=True)
    o_ref[...] = oc * lax.rsqrt(var + EPS) * nw_ref[vrows, :] * _silu(g_scr[vrows, :])


def _s5_step_kernel(u_ref, xr_ref, xi_ref, bt_ref, ct_ref, ab_ref, d_ref, gwt_ref, gb_ref,
                    accr_ref, acci_ref, y_ref, xrn_ref, xin_ref):
    del accr_ref, acci_ref
    n = S5_N
    ut = u_ref[...].T
    bu = _dot(bt_ref[...], ut.astype(BF16))
    mr, mi = _cmul(ab_ref[0:n], ab_ref[n:2 * n], xr_ref[...], xi_ref[...])
    xr = mr + bu[0:n]
    xi = mi + bu[n:2 * n]
    xrn_ref[...] = xr
    xin_ref[...] = xi
    y = _dot(ct_ref[...], jnp.concatenate([xr, xi], axis=0).astype(BF16)) + d_ref[...] * ut
    y = jax.nn.gelu(y, approximate=True)
    y_ref[...] = y * _sigmoid(_dot(gwt_ref[...], y.astype(BF16)) + gb_ref[...])


N_HG_STEP_IN, N_RT_STEP_IN, N_S5_STEP_IN = 5, 7, 11


def _sample_mixers_kernel(*refs):
    n_in = N_HG_STEP_IN + N_RT_STEP_IN + N_S5_STEP_IN
    ins, outs, scr = refs[:n_in], refs[n_in:n_in + 7], refs[n_in + 7:]
    hg_in = ins[:N_HG_STEP_IN]
    rt_in = ins[N_HG_STEP_IN:N_HG_STEP_IN + N_RT_STEP_IN]
    s5_in = ins[N_HG_STEP_IN + N_RT_STEP_IN:]
    _hgrn_step_kernel(*hg_in, outs[0], outs[1], *scr[:5])
    _ret_step_kernel(*rt_in, outs[2], outs[3], *scr[5:])

    @pl.when(pl.program_id(0) == 0)
    def _():
        _s5_step_kernel(*s5_in, *outs[4:])


def _sample_mixers(layer, proj_out, hg_par, rt_par, s5_par, states, accs):
    hg_x, rt_x, s5_x = proj_out
    n = hg_x.shape[0]
    whole = lambda shape: pl.BlockSpec(shape, lambda h: (0,) * len(shape))
    anyspace = pl.BlockSpec(memory_space=pl.ANY)
    hg_st = pl.BlockSpec((None, None, HG_DK, HG_DV, n), lambda h: (layer, h, 0, 0, 0))
    rt_st = pl.BlockSpec((None, None, RET_DK, RET_DV, n), lambda h: (layer, h, 0, 0, 0))
    s5_st = _layer_spec(layer, (S5_N, n), 1)
    in_specs = [
        whole((n, HG_COLS)), _layer_spec(layer, (HG_WIDTH, 1), 1), _layer_spec(layer, (HG_DV, 1), 1),
        hg_st, anyspace,
        whole((n, RT_COLS)), whole((2 * RET_QK, 1)), whole((2 * RET_QK, 1)),
        whole((RET_HEADS, 1, n)), _layer_spec(layer, (RET_WIDTH, 1), 1), rt_st, anyspace,
        whole((n, S5_WIDTH)), s5_st, s5_st,
        _layer_spec(layer, (2 * S5_N, S5_WIDTH), 1), _layer_spec(layer, (S5_WIDTH, 2 * S5_N), 1),
        _layer_spec(layer, (2 * S5_N, 1), 1), _layer_spec(layer, (S5_WIDTH, 1), 1),
        _layer_spec(layer, (S5_WIDTH, S5_WIDTH), 1), _layer_spec(layer, (S5_WIDTH, 1), 1),
        anyspace, anyspace]
    head_rows = pl.BlockSpec((HG_DV, n), lambda h: (h, 0))
    chan = pltpu.VMEM((HG_WIDTH, n), F32)
    acc_at = {N_HG_STEP_IN - 1: 1, N_HG_STEP_IN + N_RT_STEP_IN - 1: 3,
              N_HG_STEP_IN + N_RT_STEP_IN + N_S5_STEP_IN - 2: 5,
              N_HG_STEP_IN + N_RT_STEP_IN + N_S5_STEP_IN - 1: 6}
    return pl.pallas_call(
        _sample_mixers_kernel,
        grid=(HG_HEADS,),
        in_specs=in_specs,
        out_specs=[head_rows, hg_st, head_rows, rt_st, whole((S5_WIDTH, n)), s5_st, s5_st],
        out_shape=[jax.ShapeDtypeStruct((HG_WIDTH, n), F32),
                   jax.ShapeDtypeStruct(states[0].shape, F32),
                   jax.ShapeDtypeStruct((RET_WIDTH, n), F32),
                   jax.ShapeDtypeStruct(states[1].shape, F32),
                   jax.ShapeDtypeStruct((S5_WIDTH, n), F32),
                   jax.ShapeDtypeStruct(states[2].shape, F32),
                   jax.ShapeDtypeStruct(states[3].shape, F32)],
        scratch_shapes=[chan, chan, chan, chan, chan,
                        pltpu.VMEM((2 * RET_QK, n), F32), chan, chan],
        input_output_aliases=acc_at,
        compiler_params=pltpu.CompilerParams(
            dimension_semantics=("arbitrary",), vmem_limit_bytes=VMEM_LIMIT),
        name="sample_mixers",
    )(hg_x, *hg_par, states[0], accs[0],
      rt_x, *rt_par, states[1], accs[1],
      s5_x, states[2], states[3], *s5_par, accs[2], accs[3])


def _tile_order():
    rows = np.arange(TIME_BLOCK)
    return BLOCK_LEN * (rows % SUBLANES) + rows // SUBLANES


def _permute_tokens(x, inverse=False):
    bsz, tlen, dm = x.shape
    a, b = (BLOCK_LEN, SUBLANES) if inverse else (SUBLANES, BLOCK_LEN)
    x = x.reshape(bsz, tlen // TIME_BLOCK, a, b, dm)
    return jnp.swapaxes(x, 2, 3).reshape(bsz, tlen, dm)


def _rope_tables(pos):
    half = RET_DK // 2
    inv = ROPE_BASE ** (-np.arange(half, dtype=np.float64) / half)
    ang = np.asarray(pos, np.float64)[:, None] * inv[None, :]
    cos_h = np.concatenate([np.cos(ang), np.cos(ang)], axis=1)
    sin_h = np.concatenate([-np.sin(ang), np.sin(ang)], axis=1)
    return (np.tile(cos_h, (1, RET_HEADS)).astype(np.float32),
            np.tile(sin_h, (1, RET_HEADS)).astype(np.float32))


def _position_tables(tlen):
    tb = TIME_BLOCK
    order = _tile_order()
    pos_p = np.arange(tlen) // tb * tb + np.tile(order, tlen // tb)
    cos_p, sin_p = _rope_tables(pos_p)
    cos_s, sin_s = _rope_tables(np.array([PAST_LEN]))
    log_gamma = np.log1p(-np.exp2(-5.0 - np.arange(RET_HEADS, dtype=np.float64)))
    lg_lane = np.repeat(log_gamma, RET_DK)[None, :]
    tt = order.astype(np.float64)
    diff = tt[:, None] - tt[None, :]
    f32 = lambda a: jnp.asarray(np.asarray(a, np.float32))
    return dict(
        cos_qk=f32(np.concatenate([cos_p, cos_p], axis=1)),
        sin_qk=f32(np.concatenate([sin_p, sin_p], axis=1)),
        cos_col=f32(np.concatenate([cos_s, cos_s], axis=1).reshape(2 * RET_QK, 1)),
        sin_col=f32(np.concatenate([sin_s, sin_s], axis=1).reshape(2 * RET_QK, 1)),
        dq=f32(np.exp((tt[:, None] + 1.0) * lg_lane)),
        dk=f32(np.exp((tb - 1.0 - tt[:, None]) * lg_lane)),
        dmat=f32(np.where(diff >= 0, np.exp(np.maximum(diff, 0.0)[None] * log_gamma[:, None, None]), 0.0)),
        gam_tbl=f32(np.broadcast_to(np.exp(tb * np.repeat(log_gamma, RET_DK))[:, None],
                                    (RET_QK, RET_WIDTH))),
        gamma=np.exp(log_gamma),
    )


def _block_diag_mask(rows_per, cols_per, nblk):
    r = np.arange(rows_per * nblk)[:, None] // rows_per
    c = np.arange(cols_per * nblk)[None, :] // cols_per
    return (r == c).astype(np.float32)


def _s5_params(log_dt, a_re, a_im, b_re, b_im, c_re, c_im):
    nl = log_dt.shape[0]
    dt = jnp.exp(log_dt)[..., None]
    mag = jnp.exp(dt * a_re)
    ab_re = mag * jnp.cos(dt * a_im)
    ab_im = mag * jnp.sin(dt * a_im)
    den = a_re * a_re + a_im * a_im
    nr = ab_re - 1.0
    g_re = (nr * a_re + ab_im * a_im) / den
    g_im = (ab_im * a_re - nr * a_im) / den
    bb_re = g_re[..., None] * b_re - g_im[..., None] * b_im
    bb_im = g_re[..., None] * b_im + g_im[..., None] * b_re
    eye = jnp.eye(S5_GROUPS, dtype=F32)[None, :, None, :, None]

    def in_blk(bb):
        m = eye * jnp.swapaxes(bb, 2, 3)[:, :, :, None, :]
        return m.reshape(nl, S5_WIDTH, S5_N)

    def out_blk(c):
        m = eye * jnp.swapaxes(c, 2, 3)[:, :, :, None, :]
        return m.reshape(nl, S5_N, S5_WIDTH)

    bblk = jnp.concatenate([in_blk(bb_re), in_blk(bb_im)], axis=2)
    cblk = jnp.concatenate([out_blk(c_re), -out_blk(c_im)], axis=1)
    m = jnp.arange(1, TIME_BLOCK + 1, dtype=F32)[None, :, None, None]
    pmag = jnp.exp(m * (dt * a_re)[:, None])
    pang = m * (dt * a_im)[:, None]
    apow = jnp.concatenate([(pmag * jnp.cos(pang)).reshape(nl, TIME_BLOCK, S5_N),
                            (pmag * jnp.sin(pang)).reshape(nl, TIME_BLOCK, S5_N)], axis=2)
    ab = jnp.concatenate([ab_re.reshape(nl, S5_N, 1), ab_im.reshape(nl, S5_N, 1)], axis=1)
    return bblk, cblk, apow, ab


def kernel(x_prompt, x_sample, state_hgrn, state_ret, state_s5_re, state_s5_im, w_in, hgrn_lb_logits, hgrn_norm_w, ret_norm_w, s5_log_dt, s5_a_re, s5_a_im, s5_b_re, s5_b_im, s5_c_re, s5_c_im, s5_d, s5_glu_w, s5_glu_b, w_out, ln1_w, ln1_b, w_up, w_down, ln2_w, ln2_b):
    bp, tp, _ = x_prompt.shape
    bs = x_sample.shape[0]
    tb = TIME_BLOCK

    lb_prob = jax.nn.softmax(hgrn_lb_logits.astype(F32), axis=0)
    lower_bounds = jnp.cumsum(lb_prob, axis=0) - lb_prob[0:1]

    tbl = _position_tables(tp)
    ret_bd = jnp.asarray(_block_diag_mask(RET_DK, RET_DV, RET_HEADS))
    hg_bd = jnp.asarray(_block_diag_mask(HG_DV, HG_DK, 2))
    ones256 = jnp.asarray(_block_diag_mask(64, 64, 4)).astype(BF16)
    ones128 = jnp.asarray(_block_diag_mask(64, 64, 2)).astype(BF16)
    gam_lanes = jnp.asarray(np.broadcast_to(tbl["gamma"][:, None, None],
                                            (RET_HEADS, 1, bs)).astype(np.float32))

    wo = w_out.astype(BF16)
    ffn_par = (wo, ln1_w[:, None, :], ln1_b[:, None, :], w_up.astype(BF16), w_down.astype(BF16),
               ln2_w[:, None, :], ln2_b[:, None, :])
    lb_row = lower_bounds[:, None, :]
    hnw_row = jnp.tile(hgrn_norm_w, (1, HG_HEADS))[:, None, :]
    rnw_row = ret_norm_w[:, None, :]
    bblk, cblk, apow, ab_col = _s5_params(s5_log_dt, s5_a_re, s5_a_im, s5_b_re, s5_b_im,
                                          s5_c_re, s5_c_im)
    s5_par = (bblk.astype(BF16), cblk.astype(BF16), apow, apow[:, BLOCK_LEN - 1::BLOCK_LEN],
              s5_d[:, None, :], s5_glu_w.astype(BF16), s5_glu_b[:, None, :])
    s5_step_par = (jnp.swapaxes(s5_par[0], 1, 2), jnp.swapaxes(s5_par[1], 1, 2), ab_col,
                   s5_d[:, :, None],
                   jnp.swapaxes(s5_glu_w, 1, 2).astype(BF16), s5_glu_b[:, :, None])

    hg_s = jnp.transpose(state_hgrn, (0, 2, 3, 4, 1))
    rt_s = jnp.transpose(state_ret, (0, 2, 3, 4, 1))
    re_s = jnp.transpose(state_s5_re, (0, 2, 3, 1)).reshape(DEPTH, S5_N, bs)
    im_s = jnp.transpose(state_s5_im, (0, 2, 3, 1)).reshape(DEPTH, S5_N, bs)
    new_hg_s = jnp.zeros(hg_s.shape, F32)
    new_rt_s = jnp.zeros(rt_s.shape, F32)
    new_re_s = jnp.zeros(re_s.shape, F32)
    new_im_s = jnp.zeros(im_s.shape, F32)

    xp = _permute_tokens(x_prompt).reshape(bp * tp, D_MODEL)
    xs = x_sample.reshape(bs, D_MODEL)
    outs = {k: [] for k in ("hg_p", "rt_p", "s5_p")}
    for l in range(DEPTH):
        hg_in, rt_in, s5_in, hg_in_s, rt_in_s, s5_in_s = _proj(xp, xs, w_in, l, 512)
        o_hg, st_hg = _hgrn(hg_in, lb_row, hnw_row, ones256, ones128, hg_bd, l, bp, tp)
        o_rt, st_rt = _ret(rt_in, tbl["cos_qk"], tbl["sin_qk"], tbl["dq"], tbl["dk"], tbl["dmat"],
                           tbl["gam_tbl"], ret_bd, rnw_row, ones256, ones128, l, bp, tp)
        y5, st_s5 = _s5(s5_in, *s5_par, l, bp, tp)
        outs["hg_p"].append(st_hg)
        outs["rt_p"].append(st_rt)
        outs["s5_p"].append(st_s5)
        o_hg_t, new_hg_s, o_rt_t, new_rt_s, y5_t, new_re_s, new_im_s = _sample_mixers(
            l, (hg_in_s, rt_in_s, s5_in_s),
            (lower_bounds[:, :, None], hgrn_norm_w[:, :, None]),
            (tbl["cos_col"], tbl["sin_col"], gam_lanes, ret_norm_w[:, :, None]),
            s5_step_par, (hg_s, rt_s, re_s, im_s), (new_hg_s, new_rt_s, new_re_s, new_im_s))
        xp, xs = _ffn((xp, o_hg, o_rt, y5), (xs, o_hg_t, o_rt_t, y5_t), *ffn_par, l, 512)

    yp = _permute_tokens(xp.reshape(bp, tp, D_MODEL), inverse=True)

    def s5_out(a):
        return jnp.transpose(a.reshape(DEPTH, S5_GROUPS, S5_STATE, bs), (0, 3, 1, 2))

    s5_p = jnp.stack(outs["s5_p"]).reshape(DEPTH, bp, 2, S5_GROUPS, S5_STATE)

    return (yp, xs.reshape(bs, 1, D_MODEL), jnp.stack(outs["hg_p"]), jnp.stack(outs["rt_p"]),
            s5_p[:, :, 0], s5_p[:, :, 1],
            jnp.transpose(new_hg_s, (0, 4, 1, 2, 3)), jnp.transpose(new_rt_s, (0, 4, 1, 2, 3)),
            s5_out(new_re_s), s5_out(new_im_s))
```

```python
import jax
import jax.numpy as jnp
import numpy as np
from jax import lax
from jax.experimental import pallas as pl
from jax.experimental.pallas import tpu as pltpu

F32 = jnp.float32
BF16 = jnp.bfloat16

D_MODEL = 1024
DEPTH = 4
PAST_LEN = 16384
HG_WIDTH = 384
HG_HEADS = 6
HG_DK = 64
HG_DV = 64
RET_WIDTH = 384
RET_HEADS = 6
RET_DK = 32
RET_DV = 64
RET_QK = RET_HEADS * RET_DK
S5_WIDTH = 256
S5_GROUP = 16
S5_GROUPS = 16
S5_STATE = 64
S5_N = S5_GROUPS * S5_STATE
D_FF = 4 * D_MODEL
ROPE_BASE = 10000.0
EPS = 1e-5
ALPHA = (2.0 * DEPTH) ** 0.25

HG_COLS = 4 * HG_WIDTH
RT_COLS = 2 * RET_QK + 2 * RET_WIDTH
PROJ_COLS = HG_COLS + RT_COLS + S5_WIDTH
LANES = 128

SUBLANES = 8
TIME_BLOCK = 128
BLOCK_LEN = TIME_BLOCK // SUBLANES
SEQS_PER_STEP = 8
FF_CHUNK = 1024
FFN_ROW_GROUPS = 2
VMEM_LIMIT = 56 * 1024 * 1024


def _dot(a, b):
    return jnp.dot(a, b, preferred_element_type=F32)


def _dot_nt(a, b):
    return lax.dot_general(a, b, (((1,), (1,)), ((), ())), preferred_element_type=F32)


def _dot_tn(a, b):
    return lax.dot_general(a, b, (((0,), (0,)), ((), ())), preferred_element_type=F32)


def _sigmoid(x):
    return jax.nn.sigmoid(x)


def _silu(x):
    return x * jax.nn.sigmoid(x)


def _layer_norm(y, w, b):
    mu = jnp.mean(y, -1, keepdims=True)
    yc = y - mu
    var = jnp.mean(yc * yc, -1, keepdims=True)
    return yc * lax.rsqrt(var + EPS) * w + b


def _head_sum_bf16(x, ones256, ones128):
    return jnp.concatenate([_dot(x[:, 0:256], ones256), _dot(x[:, 256:384], ones128)], axis=1)


def _head_sum(x, ones256, ones128):
    hi = x.astype(BF16)
    lo = (x - hi.astype(F32)).astype(BF16)
    return _head_sum_bf16(hi, ones256, ones128) + _head_sum_bf16(lo, ones256, ones128)


def _proj_rows(x_ref, w_ref, hg_ref, rt_ref, s5_ref):
    x = x_ref[...].astype(BF16)
    hg_ref[...] = _dot(x, w_ref[:, 0:HG_COLS].astype(BF16))
    rt_ref[...] = _dot(x, w_ref[:, HG_COLS:HG_COLS + RT_COLS].astype(BF16))
    s5_ref[...] = _dot(x, w_ref[:, HG_COLS + RT_COLS:PROJ_COLS].astype(BF16))


def _proj_kernel(xp_ref, xs_ref, w_ref, hgp_ref, rtp_ref, s5p_ref, hgs_ref, rts_ref, s5s_ref):
    i = pl.program_id(0)
    last = pl.num_programs(0) - 1

    @pl.when(i < last)
    def _():
        _proj_rows(xp_ref, w_ref, hgp_ref, rtp_ref, s5p_ref)

    @pl.when(i == last)
    def _():
        _proj_rows(xs_ref, w_ref, hgs_ref, rts_ref, s5s_ref)


def _layer_spec(layer, shape, grid_rank, **kw):
    zeros = (0,) * len(shape)
    if grid_rank == 1:
        return pl.BlockSpec((None,) + shape, lambda i: (layer,) + zeros, **kw)
    return pl.BlockSpec((None,) + shape, lambda b, t: (layer,) + zeros, **kw)


def _two_path_specs(n_tiles, tm, ms, widths):
    tile = lambda i: (jnp.minimum(i, n_tiles - 1), 0)
    return ([pl.BlockSpec((tm, w), tile) for w in widths],
            [pl.BlockSpec((ms, w), lambda i: (0, 0)) for w in widths])


def _proj(xp, xs, w, layer, tm):
    m, ms = xp.shape[0], xs.shape[0]
    n_tiles = m // tm
    widths = (HG_COLS, RT_COLS, S5_WIDTH)
    (xp_spec,), (xs_spec,) = _two_path_specs(n_tiles, tm, ms, (D_MODEL,))
    outp, outs = _two_path_specs(n_tiles, tm, ms, widths)
    return pl.pallas_call(
        _proj_kernel,
        grid=(n_tiles + 1,),
        in_specs=[xp_spec, xs_spec,
                  _layer_spec(layer, (D_MODEL, PROJ_COLS), 1, pipeline_mode=pl.Buffered(1))],
        out_specs=outp + outs,
        out_shape=([jax.ShapeDtypeStruct((m, c), F32) for c in widths]
                   + [jax.ShapeDtypeStruct((ms, c), F32) for c in widths]),
        compiler_params=pltpu.CompilerParams(
            dimension_semantics=("arbitrary",), vmem_limit_bytes=VMEM_LIMIT),
        name="proj",
    )(xp, xs, w)


def _ffn_rows(rows, channel_major, x_ref, a_ref, b_ref, c_ref, wo_ref, l1w_ref, l1b_ref, wu_ref,
              wd_ref, l2w_ref, l2b_ref, o_ref):
    take = (lambda r: r[...].T) if channel_major else (lambda r: r[rows, :])
    x = x_ref[rows, :]
    mixed = (_dot(take(a_ref).astype(BF16), wo_ref[0:HG_WIDTH, :])
             + _dot(take(b_ref).astype(BF16), wo_ref[HG_WIDTH:HG_WIDTH + RET_WIDTH, :])
             + _dot(take(c_ref).astype(BF16), wo_ref[HG_WIDTH + RET_WIDTH:D_MODEL, :]))
    yield
    x1 = _layer_norm(ALPHA * x + mixed, l1w_ref[...], l1b_ref[...])
    xb = x1.astype(BF16)
    yield
    ff = None
    for c in range(D_FF // FF_CHUNK):
        h = _dot(xb, wu_ref[:, c * FF_CHUNK:(c + 1) * FF_CHUNK])
        yield
        h = jnp.square(jnp.maximum(h, 0.0)).astype(BF16)
        yield
        d = _dot(h, wd_ref[c * FF_CHUNK:(c + 1) * FF_CHUNK, :])
        ff = d if ff is None else ff + d
        yield
    o_ref[rows, :] = _layer_norm(ALPHA * x1 + ff, l2w_ref[...], l2b_ref[...])


def _ffn_kernel(xp_ref, ap_ref, bp_ref, cp_ref, xs_ref, as_ref, bs_ref, cs_ref, *refs):
    weights, (op_ref, os_ref) = refs[:-2], refs[-2:]
    i = pl.program_id(0)
    last = pl.num_programs(0) - 1

    @pl.when(i < last)
    def _():
        _ffn_tile(FFN_ROW_GROUPS, False, xp_ref, ap_ref, bp_ref, cp_ref, *weights, op_ref)

    @pl.when(i == last)
    def _():
        _ffn_tile(1, True, xs_ref, as_ref, bs_ref, cs_ref, *weights, os_ref)


def _ffn_tile(groups, channel_major, x_ref, *refs):
    step = x_ref.shape[0] // groups
    live = [_ffn_rows(slice(g * step, (g + 1) * step), channel_major, x_ref, *refs)
            for g in range(groups)]
    for g in range(groups):
        for gen in live[:g]:
            next(gen, _DONE)
    while live:
        live = [gen for gen in live if next(gen, _DONE) is not _DONE]


def _ffn(prompt, sample, wo, l1w, l1b, wu, wd, l2w, l2b, layer, tm):
    m, ms = prompt[0].shape[0], sample[0].shape[0]
    n_tiles = m // tm
    widths = (D_MODEL, HG_WIDTH, RET_WIDTH, S5_WIDTH)
    inp, ins = _two_path_specs(n_tiles, tm, ms, widths)
    ins = ins[:1] + [pl.BlockSpec((w, ms), lambda i: (0, 0)) for w in widths[1:]]
    (outp,), (outs,) = _two_path_specs(n_tiles, tm, ms, (D_MODEL,))
    once = pl.Buffered(1)
    vec = _layer_spec(layer, (1, D_MODEL), 1)
    return pl.pallas_call(
        _ffn_kernel,
        grid=(n_tiles + 1,),
        in_specs=inp + ins + [
            _layer_spec(layer, (D_MODEL, D_MODEL), 1, pipeline_mode=once),
            vec, vec,
            _layer_spec(layer, (D_MODEL, D_FF), 1, pipeline_mode=once),
            _layer_spec(layer, (D_FF, D_MODEL), 1, pipeline_mode=once),
            vec, vec],
        out_specs=[outp, outs],
        out_shape=[jax.ShapeDtypeStruct((m, D_MODEL), F32),
                   jax.ShapeDtypeStruct((ms, D_MODEL), F32)],
        compiler_params=pltpu.CompilerParams(
            dimension_semantics=("arbitrary",), vmem_limit_bytes=VMEM_LIMIT),
        name="ffn",
    )(*prompt, *sample, wo, l1w, l1b, wu, wd, l2w, l2b)


_DONE = object()


def _copy_state(state_out, state_scr):
    state_out[...] = state_scr[...]


def _hgrn_state_out(st_ref, s_scr):
    for i in range(SEQS_PER_STEP):
        for h in range(HG_HEADS):
            e = h % 2
            st_ref[i, h] = s_scr[i, h // 2, HG_DV * e:HG_DV * (e + 1), HG_DK * e:HG_DK * (e + 1)].T


def _ret_state_out(st_ref, s_scr):
    for i in range(SEQS_PER_STEP):
        for h in range(RET_HEADS):
            st_ref[i, h] = s_scr[i, RET_DK * h:RET_DK * (h + 1), RET_DV * h:RET_DV * (h + 1)]


def _per_sequence(body, n_shared_in, write_state=_copy_state):
    def kern(x_ref, *refs):
        shared = refs[:n_shared_in]
        per_seq = refs[n_shared_in:]
        state_out, state_scr = per_seq[1], per_seq[2]
        t = pl.program_id(1)

        @pl.when(t == 0)
        def _():
            state_scr[...] = jnp.zeros(state_scr.shape, F32)

        live = [body(x_ref.at[i], *shared, *(r.at[i] for r in per_seq))
                for i in range(SEQS_PER_STEP)]
        while live:
            live = [g for g in live if next(g, _DONE) is not _DONE]

        @pl.when(t == pl.num_programs(1) - 1)
        def _():
            write_state(state_out, state_scr)
    return kern


def _seq_major(a):
    return a.reshape(SEQS_PER_STEP, a.shape[0] // SEQS_PER_STEP, a.shape[1])


def _hgrn_body(x_ref, lb_ref, nw_ref, ones256_ref, ones128_ref, bd_ref, o_ref, st_ref,
               s_scr, qt_scr, kt_scr, v_scr, oi_scr):
    del st_ref
    npos = BLOCK_LEN
    sl = SUBLANES
    lb = lb_ref[...]
    ones256 = ones256_ref[...]
    ones128 = ones128_ref[...]
    f, kk, q, v = [], [], [], []
    for p in range(npos):
        rows = slice(sl * p, sl * (p + 1))
        hq = x_ref[rows, 0:HG_WIDTH]
        z = x_ref[rows, HG_WIDTH:2 * HG_WIDTH]
        f.append(lb + (1.0 - lb) * _sigmoid(z))
        kk.append(1.0 - f[-1])
        q.append(_silu(hq))
        v.append(x_ref[rows, 2 * HG_WIDTH:3 * HG_WIDTH])
        if p % 4 == 3:
            yield

    g = list(kk)
    e_rows = []
    for d in range(npos):
        if d > 0:
            g = [None] * d + [f[p] * g[p - 1] for p in range(d, npos)]
        e_rows.extend(q[p] * g[p] for p in range(d, npos))
        if d % 4 == 3:
            yield
    e_all = jnp.concatenate(e_rows, axis=0).astype(BF16)
    p_all = _head_sum_bf16(e_all, ones256, ones128)
    yield
    o = [None] * npos
    i = 0
    for d in range(npos):
        for p in range(d, npos):
            term = p_all[sl * i:sl * (i + 1)] * v[p - d]
            o[p] = term if o[p] is None else o[p] + term
            i += 1
        if d % 4 == 3:
            yield

    a = [f[0]]
    for p in range(1, npos):
        a.append(a[-1] * f[p])
    r = [None] * npos
    r[npos - 1] = jnp.ones_like(f[0])
    for p in range(npos - 2, -1, -1):
        r[p] = r[p + 1] * f[p + 1]
    npair = HG_HEADS // 2
    for p in range(npos):
        rows = slice(sl * p, sl * (p + 1))
        qt = q[p] * a[p]
        kt = kk[p] * r[p]
        for pr in range(npair):
            lanes = slice(128 * pr, 128 * (pr + 1))
            qt_scr[pr, rows, :] = qt[:, lanes]
            kt_scr[pr, rows, :] = kt[:, lanes]
            v_scr[pr, rows, :] = v[p][:, lanes]
    cd = a[npos - 1]
    for j in range(sl):
        blk = pl.ds(j, npos, stride=sl)
        for pr in range(npair):
            lanes = slice(128 * pr, 128 * (pr + 1))
            st = s_scr[pr]
            oi_scr[pr, blk, :] = _dot_nt(qt_scr[pr, blk, :].astype(BF16), st.astype(BF16))
            u = _dot_tn(v_scr[pr, blk, :].astype(BF16), kt_scr[pr, blk, :].astype(BF16))
            s_scr[pr] = st * cd[j:j + 1, lanes] + u * bd_ref[...]
        yield

    ot = (jnp.concatenate(o, axis=0)
          + jnp.concatenate([oi_scr[pr] for pr in range(npair)], axis=1))
    ms = _head_sum(ot * ot, ones256, ones128) * (1.0 / HG_DV)
    gate = x_ref[:, 3 * HG_WIDTH:4 * HG_WIDTH]
    o_ref[...] = ot * lax.rsqrt(ms + EPS) * nw_ref[...] * _silu(gate)


def _hgrn(hg_in, lb, nw, ones256, ones128, bd, layer, bsz, tlen):
    tb = TIME_BLOCK
    nt = tlen // tb
    ns = SEQS_PER_STEP
    npair = HG_HEADS // 2
    fixed2 = lambda b, t: (0, 0)
    tile = lambda b, t: (0, b * nt + t, 0)
    slab = pltpu.VMEM((ns, npair, tb, 128), F32)
    o, st = pl.pallas_call(
        _per_sequence(_hgrn_body, 5, _hgrn_state_out),
        grid=(bsz // ns, nt),
        in_specs=[pl.BlockSpec((ns, tb, HG_COLS), tile),
                  _layer_spec(layer, (1, HG_WIDTH), 2),
                  _layer_spec(layer, (1, HG_WIDTH), 2),
                  pl.BlockSpec((256, 256), fixed2),
                  pl.BlockSpec((128, 128), fixed2),
                  pl.BlockSpec((128, 128), fixed2)],
        out_specs=[pl.BlockSpec((ns, tb, HG_WIDTH), tile),
                   pl.BlockSpec((ns, None, HG_HEADS, HG_DK, HG_DV), lambda b, t: (0, b, 0, 0, 0))],
        out_shape=[jax.ShapeDtypeStruct((ns, bsz // ns * tlen, HG_WIDTH), F32),
                   jax.ShapeDtypeStruct((ns, bsz // ns, HG_HEADS, HG_DK, HG_DV), F32)],
        scratch_shapes=[pltpu.VMEM((ns, npair, 128, 128), F32), slab, slab, slab, slab],
        compiler_params=pltpu.CompilerParams(
            dimension_semantics=("parallel", "arbitrary"), vmem_limit_bytes=VMEM_LIMIT),
        name="hgrn",
    )(_seq_major(hg_in), lb, nw, ones256, ones128, bd)
    return o.reshape(bsz * tlen, HG_WIDTH), st.reshape(bsz, HG_HEADS, HG_DK, HG_DV)


def _ret_body(x_ref, cos_ref, sin_ref, dq_ref, dk_ref, dmat_ref, gam_ref, bd_ref, nw_ref,
              ones256_ref, ones128_ref, o_ref, st_ref, s_scr):
    del st_ref
    tb = TIME_BLOCK
    w = RET_QK
    half = RET_DK // 2
    in_first_half = (lax.broadcasted_iota(jnp.int32, (tb, LANES), 1) & (RET_DK - 1)) < half
    rot = []
    for c in range(2 * w // LANES):
        lanes = slice(LANES * c, LANES * (c + 1))
        x = x_ref[:, lanes]
        partner = jnp.where(in_first_half, pltpu.roll(x, LANES - half, 1), pltpu.roll(x, half, 1))
        rot.append(x * cos_ref[:, lanes] + partner * sin_ref[:, lanes])
    rot = jnp.concatenate(rot, axis=1)
    q = rot[:, 0:w]
    k = rot[:, w:2 * w] * (RET_DK ** -0.5)
    v = x_ref[:, 2 * w:2 * w + RET_WIDTH]
    gate = x_ref[:, 2 * w + RET_WIDTH:2 * w + 2 * RET_WIDTH]
    vb = v.astype(BF16)

    lane = lax.broadcasted_iota(jnp.int32, (tb, w), 1)
    qs = jnp.concatenate(
        [jnp.where((lane >= RET_DK * h) & (lane < RET_DK * (h + 1)), q, 0.0)
         for h in range(RET_HEADS)], axis=0).astype(BF16)
    yield
    sc = _dot_nt(qs, k.astype(BF16))
    yield
    lane128 = lax.broadcasted_iota(jnp.int32, (tb, 128), 1)
    outs = []
    for p in range(RET_HEADS // 2):
        p0 = sc[(2 * p) * tb:(2 * p + 1) * tb] * dmat_ref[2 * p]
        p1 = sc[(2 * p + 1) * tb:(2 * p + 2) * tb] * dmat_ref[2 * p + 1]
        pc = jnp.concatenate([p0, p1], axis=1).astype(BF16)
        vp = v[:, 128 * p:128 * (p + 1)]
        vs = jnp.concatenate([jnp.where(lane128 < RET_DV, vp, 0.0),
                              jnp.where(lane128 >= RET_DV, vp, 0.0)], axis=0).astype(BF16)
        outs.append(_dot(pc, vs))
    o = jnp.concatenate(outs, axis=1)
    yield

    s = s_scr[...]
    o = o + _dot((q * dq_ref[...]).astype(BF16), s.astype(BF16))
    u = _dot_tn((k * dk_ref[...]).astype(BF16), vb)
    s_scr[...] = s * gam_ref[...] + u * bd_ref[...]

    ones256 = ones256_ref[...]
    ones128 = ones128_ref[...]
    mu = _head_sum(o, ones256, ones128) * (1.0 / RET_DV)
    oc = o - mu
    var = _head_sum(oc * oc, ones256, ones128) * (1.0 / RET_DV)
    o_ref[...] = oc * lax.rsqrt(var + EPS) * nw_ref[...] * _silu(gate)


def _ret(rt_in, cos, sin, dq, dk, dmat, gam, bd, nw, ones256, ones128, layer, bsz, tlen):
    tb = TIME_BLOCK
    nt = tlen // tb
    ns = SEQS_PER_STEP
    fixed2 = lambda b, t: (0, 0)
    tile = lambda b, t: (0, b * nt + t, 0)
    o, st = pl.pallas_call(
        _per_sequence(_ret_body, 10, _ret_state_out),
        grid=(bsz // ns, nt),
        in_specs=[pl.BlockSpec((ns, tb, RT_COLS), tile),
                  pl.BlockSpec((tb, 2 * RET_QK), lambda b, t: (t, 0)),
                  pl.BlockSpec((tb, 2 * RET_QK), lambda b, t: (t, 0)),
                  pl.BlockSpec((tb, RET_QK), fixed2),
                  pl.BlockSpec((tb, RET_QK), fixed2),
                  pl.BlockSpec((RET_HEADS, tb, tb), lambda b, t: (0, 0, 0)),
                  pl.BlockSpec((RET_QK, RET_WIDTH), fixed2),
                  pl.BlockSpec((RET_QK, RET_WIDTH), fixed2),
                  _layer_spec(layer, (1, RET_WIDTH), 2),
                  pl.BlockSpec((256, 256), fixed2),
                  pl.BlockSpec((128, 128), fixed2)],
        out_specs=[pl.BlockSpec((ns, tb, RET_WIDTH), tile),
                   pl.BlockSpec((ns, None, RET_HEADS, RET_DK, RET_DV), lambda b, t: (0, b, 0, 0, 0))],
        out_shape=[jax.ShapeDtypeStruct((ns, bsz // ns * tlen, RET_WIDTH), F32),
                   jax.ShapeDtypeStruct((ns, bsz // ns, RET_HEADS, RET_DK, RET_DV), F32)],
        scratch_shapes=[pltpu.VMEM((ns, RET_QK, RET_WIDTH), F32)],
        compiler_params=pltpu.CompilerParams(
            dimension_semantics=("parallel", "arbitrary"), vmem_limit_bytes=VMEM_LIMIT),
        name="ret",
    )(_seq_major(rt_in), cos, sin, dq, dk, dmat, gam, bd, nw, ones256, ones128)
    return o.reshape(bsz * tlen, RET_WIDTH), st.reshape(bsz, RET_HEADS, RET_DK, RET_DV)


def _cmul(ar, ai, xr, xi):
    return ar * xr - ai * xi, ar * xi + ai * xr


def _s5_body(u_ref, bblk_ref, cblk_ref, apow_ref, aend_ref, d_ref, gw_ref, gb_ref, y_ref, st_ref,
             x_scr):
    del st_ref
    npos = BLOCK_LEN
    sl = SUBLANES
    n = S5_N

    def power(m):
        return apow_ref[m - 1:m, 0:n], apow_ref[m - 1:m, n:2 * n]

    u = u_ref[...]
    bu = _dot(u.astype(BF16), bblk_ref[...])
    a1r, a1i = power(1)
    xr = [bu[0:sl, 0:n]]
    xi = [bu[0:sl, n:2 * n]]
    for p in range(1, npos):
        rows = slice(sl * p, sl * (p + 1))
        mr, mi = _cmul(a1r, a1i, xr[-1], xi[-1])
        xr.append(bu[rows, 0:n] + mr)
        xi.append(bu[rows, n:2 * n] + mi)

    yield
    blk = lax.broadcasted_iota(jnp.int32, (sl, n), 0)
    cr, ci = xr[-1], xi[-1]
    s = 1
    while s < sl:
        pr_, pi_ = power(npos * s)
        sr = jnp.where(blk >= s, pltpu.roll(cr, s, 0), 0.0)
        si = jnp.where(blk >= s, pltpu.roll(ci, s, 0), 0.0)
        mr, mi = _cmul(pr_, pi_, sr, si)
        cr, ci = cr + mr, ci + mi
        s *= 2
    x0r = x_scr[:, 0:n]
    x0i = x_scr[:, n:2 * n]
    mr, mi = _cmul(aend_ref[:, 0:n], aend_ref[:, n:2 * n], x0r, x0i)
    cr, ci = cr + mr, ci + mi
    x_scr[:, 0:n] = cr[sl - 1:sl]
    x_scr[:, n:2 * n] = ci[sl - 1:sl]
    inr = jnp.where(blk >= 1, pltpu.roll(cr, 1, 0), x0r)
    ini = jnp.where(blk >= 1, pltpu.roll(ci, 1, 0), x0i)
    rows_out = []
    for p in range(npos):
        pr_, pi_ = power(p + 1)
        mr, mi = _cmul(pr_, pi_, inr, ini)
        rows_out.append(jnp.concatenate([xr[p] + mr, xi[p] + mi], axis=1))
    xc = jnp.concatenate(rows_out, axis=0).astype(BF16)
    yield

    y = _dot(xc, cblk_ref[...]) + d_ref[...] * u
    y = jax.nn.gelu(y, approximate=True)
    y_ref[...] = y * _sigmoid(_dot(y.astype(BF16), gw_ref[...]) + gb_ref[...])


def _s5(u, bblk, cblk, apow, aend, dvec, gw, gb, layer, bsz, tlen):
    tb = TIME_BLOCK
    nt = tlen // tb
    ns = SEQS_PER_STEP
    tile = lambda b, t: (0, b * nt + t, 0)
    y, st = pl.pallas_call(
        _per_sequence(_s5_body, 7),
        grid=(bsz // ns, nt),
        in_specs=[pl.BlockSpec((ns, tb, S5_WIDTH), tile),
                  _layer_spec(layer, (S5_WIDTH, 2 * S5_N), 2),
                  _layer_spec(layer, (2 * S5_N, S5_WIDTH), 2),
                  _layer_spec(layer, (tb, 2 * S5_N), 2),
                  _layer_spec(layer, (SUBLANES, 2 * S5_N), 2),
                  _layer_spec(layer, (1, S5_WIDTH), 2),
                  _layer_spec(layer, (S5_WIDTH, S5_WIDTH), 2),
                  _layer_spec(layer, (1, S5_WIDTH), 2)],
        out_specs=[pl.BlockSpec((ns, tb, S5_WIDTH), tile),
                   pl.BlockSpec((ns, None, 1, 2 * S5_N), lambda b, t: (0, b, 0, 0))],
        out_shape=[jax.ShapeDtypeStruct((ns, bsz // ns * tlen, S5_WIDTH), F32),
                   jax.ShapeDtypeStruct((ns, bsz // ns, 1, 2 * S5_N), F32)],
        scratch_shapes=[pltpu.VMEM((ns, 1, 2 * S5_N), F32)],
        compiler_params=pltpu.CompilerParams(
            dimension_semantics=("parallel", "arbitrary"), vmem_limit_bytes=VMEM_LIMIT),
        name="s5",
    )(_seq_major(u), bblk, cblk, apow, aend, dvec, gw, gb)
    return y.reshape(bsz * tlen, S5_WIDTH), st.reshape(bsz, 1, 2 * S5_N)


def _hgrn_step_kernel(x_ref, lb_ref, nw_ref, s_ref, acc_ref, o_ref, sn_ref,
                      f_scr, k_scr, q_scr, v_scr, g_scr):
    del acc_ref
    h = pl.program_id(0)

    @pl.when(h == 0)
    def _():
        lb = lb_ref[...]
        z = x_ref[:, HG_WIDTH:2 * HG_WIDTH].T
        f_scr[...] = lb + (1.0 - lb) * _sigmoid(z)
        k_scr[...] = (1.0 - lb) * _sigmoid(-z)
        q_scr[...] = _silu(x_ref[:, 0:HG_WIDTH].T)
        v_scr[...] = x_ref[:, 2 * HG_WIDTH:3 * HG_WIDTH].T
        g_scr[...] = x_ref[:, 3 * HG_WIDTH:4 * HG_WIDTH].T

    rows = pl.ds(pl.multiple_of(h * HG_DK, HG_DK), HG_DK)
    f = f_scr[rows, :]
    kk = k_scr[rows, :]
    q = q_scr[rows, :]
    v = v_scr[rows, :]
    o = jnp.zeros_like(v)
    for k in range(HG_DK):
        sn = f[k:k + 1] * s_ref[k] + kk[k:k + 1] * v
        sn_ref[k] = sn
        o = o + q[k:k + 1] * sn
    ms = jnp.mean(o * o, axis=0, keepdims=True)
    o_ref[...] = o * lax.rsqrt(ms + EPS) * nw_ref[...] * _silu(g_scr[rows, :])


def _ret_step_kernel(x_ref, cos_ref, sin_ref, gam_ref, nw_ref, s_ref, acc_ref, o_ref, sn_ref,
                     qk_scr, v_scr, g_scr):
    del acc_ref
    half = RET_DK // 2
    h = pl.program_id(0)

    @pl.when(h == 0)
    def _():
        x = x_ref[:, 0:2 * RET_QK].T
        parts = []
        for i in range(2 * RET_HEADS):
            parts.append(x[RET_DK * i + half:RET_DK * (i + 1)])
            parts.append(x[RET_DK * i:RET_DK * i + half])
        qk_scr[...] = x * cos_ref[...] + jnp.concatenate(parts, axis=0) * sin_ref[...]
        v_scr[...] = x_ref[:, 2 * RET_QK:2 * RET_QK + RET_WIDTH].T
        g_scr[...] = x_ref[:, 2 * RET_QK + RET_WIDTH:2 * RET_QK + 2 * RET_WIDTH].T

    q = qk_scr[pl.ds(pl.multiple_of(h * RET_DK, RET_DK), RET_DK), :]
    kk = qk_scr[pl.ds(pl.multiple_of(RET_QK + h * RET_DK, RET_DK), RET_DK), :] * (RET_DK ** -0.5)
    vrows = pl.ds(pl.multiple_of(h * RET_DV, RET_DV), RET_DV)
    v = v_scr[vrows, :]
    gam = gam_ref[h]
    o = jnp.zeros_like(v)
    for k in range(RET_DK):
        sn = gam * s_ref[k] + kk[k:k + 1] * v
        sn_ref[k] = sn
        o = o + q[k:k + 1] * sn
    mu = jnp.mean(o, axis=0, keepdims=True)
    oc = o - mu
    var = jnp.mean(oc * oc, axis=0, keepdims=True)
    o_ref[...] = oc * lax.rsqrt(var + EPS) * nw_ref[vrows, :] * _silu(g_scr[vrows, :])


def _s5_step_kernel(u_ref, xr_ref, xi_ref, bt_ref, ct_ref, ab_ref, d_ref, gwt_ref, gb_ref,
                    accr_ref, acci_ref, y_ref, xrn_ref, xin_ref):
    del accr_ref, acci_ref
    n = S5_N
    ut = u_ref[...].T
    bu = _dot(bt_ref[...], ut.astype(BF16))
    mr, mi = _cmul(ab_ref[0:n], ab_ref[n:2 * n], xr_ref[...], xi_ref[...])
    xr = mr + bu[0:n]
    xi = mi + bu[n:2 * n]
    xrn_ref[...] = xr
    xin_ref[...] = xi
    y = _dot(ct_ref[...], jnp.concatenate([xr, xi], axis=0).astype(BF16)) + d_ref[...] * ut
    y = jax.nn.gelu(y, approximate=True)
    y_ref[...] = y * _sigmoid(_dot(gwt_ref[...], y.astype(BF16)) + gb_ref[...])


N_HG_STEP_IN, N_RT_STEP_IN, N_S5_STEP_IN = 5, 7, 11


def _sample_mixers_kernel(*refs):
    n_in = N_HG_STEP_IN + N_RT_STEP_IN + N_S5_STEP_IN
    ins, outs, scr = refs[:n_in], refs[n_in:n_in + 7], refs[n_in + 7:]
    hg_in = ins[:N_HG_STEP_IN]
    rt_in = ins[N_HG_STEP_IN:N_HG_STEP_IN + N_RT_STEP_IN]
    s5_in = ins[N_HG_STEP_IN + N_RT_STEP_IN:]
    _hgrn_step_kernel(*hg_in, outs[0], outs[1], *scr[:5])
    _ret_step_kernel(*rt_in, outs[2], outs[3], *scr[5:])

    @pl.when(pl.program_id(0) == 0)
    def _():
        _s5_step_kernel(*s5_in, *outs[4:])


def _sample_mixers(layer, proj_out, hg_par, rt_par, s5_par, states, accs):
    hg_x, rt_x, s5_x = proj_out
    n = hg_x.shape[0]
    whole = lambda shape: pl.BlockSpec(shape, lambda h: (0,) * len(shape))
    anyspace = pl.BlockSpec(memory_space=pl.ANY)
    hg_st = pl.BlockSpec((None, None, HG_DK, HG_DV, n), lambda h: (layer, h, 0, 0, 0))
    rt_st = pl.BlockSpec((None, None, RET_DK, RET_DV, n), lambda h: (layer, h, 0, 0, 0))
    s5_st = _layer_spec(layer, (S5_N, n), 1)
    in_specs = [
        whole((n, HG_COLS)), _layer_spec(layer, (HG_WIDTH, 1), 1), _layer_spec(layer, (HG_DV, 1), 1),
        hg_st, anyspace,
        whole((n, RT_COLS)), whole((2 * RET_QK, 1)), whole((2 * RET_QK, 1)),
        whole((RET_HEADS, 1, n)), _layer_spec(layer, (RET_WIDTH, 1), 1), rt_st, anyspace,
        whole((n, S5_WIDTH)), s5_st, s5_st,
        _layer_spec(layer, (2 * S5_N, S5_WIDTH), 1), _layer_spec(layer, (S5_WIDTH, 2 * S5_N), 1),
        _layer_spec(layer, (2 * S5_N, 1), 1), _layer_spec(layer, (S5_WIDTH, 1), 1),
        _layer_spec(layer, (S5_WIDTH, S5_WIDTH), 1), _layer_spec(layer, (S5_WIDTH, 1), 1),
        anyspace, anyspace]
    head_rows = pl.BlockSpec((HG_DV, n), lambda h: (h, 0))
    chan = pltpu.VMEM((HG_WIDTH, n), F32)
    acc_at = {N_HG_STEP_IN - 1: 1, N_HG_STEP_IN + N_RT_STEP_IN - 1: 3,
              N_HG_STEP_IN + N_RT_STEP_IN + N_S5_STEP_IN - 2: 5,
              N_HG_STEP_IN + N_RT_STEP_IN + N_S5_STEP_IN - 1: 6}
    return pl.pallas_call(
        _sample_mixers_kernel,
        grid=(HG_HEADS,),
        in_specs=in_specs,
        out_specs=[head_rows, hg_st, head_rows, rt_st, whole((S5_WIDTH, n)), s5_st, s5_st],
        out_shape=[jax.ShapeDtypeStruct((HG_WIDTH, n), F32),
                   jax.ShapeDtypeStruct(states[0].shape, F32),
                   jax.ShapeDtypeStruct((RET_WIDTH, n), F32),
                   jax.ShapeDtypeStruct(states[1].shape, F32),
                   jax.ShapeDtypeStruct((S5_WIDTH, n), F32),
                   jax.ShapeDtypeStruct(states[2].shape, F32),
                   jax.ShapeDtypeStruct(states[3].shape, F32)],
        scratch_shapes=[chan, chan, chan, chan, chan,
                        pltpu.VMEM((2 * RET_QK, n), F32), chan, chan],
        input_output_aliases=acc_at,
        compiler_params=pltpu.CompilerParams(
            dimension_semantics=("arbitrary",), vmem_limit_bytes=VMEM_LIMIT),
        name="sample_mixers",
    )(hg_x, *hg_par, states[0], accs[0],
      rt_x, *rt_par, states[1], accs[1],
      s5_x, states[2], states[3], *s5_par, accs[2], accs[3])


def _tile_order():
    rows = np.arange(TIME_BLOCK)
    return BLOCK_LEN * (rows % SUBLANES) + rows // SUBLANES


def _permute_tokens(x, inverse=False):
    bsz, tlen, dm = x.shape
    a, b = (BLOCK_LEN, SUBLANES) if inverse else (SUBLANES, BLOCK_LEN)
    x = x.reshape(bsz, tlen // TIME_BLOCK, a, b, dm)
    return jnp.swapaxes(x, 2, 3).reshape(bsz, tlen, dm)


def _rope_tables(pos):
    half = RET_DK // 2
    inv = ROPE_BASE ** (-np.arange(half, dtype=np.float64) / half)
    ang = np.asarray(pos, np.float64)[:, None] * inv[None, :]
    cos_h = np.concatenate([np.cos(ang), np.cos(ang)], axis=1)
    sin_h = np.concatenate([-np.sin(ang), np.sin(ang)], axis=1)
    return (np.tile(cos_h, (1, RET_HEADS)).astype(np.float32),
            np.tile(sin_h, (1, RET_HEADS)).astype(np.float32))


def _position_tables(tlen):
    tb = TIME_BLOCK
    order = _tile_order()
    pos_p = np.arange(tlen) // tb * tb + np.tile(order, tlen // tb)
    cos_p, sin_p = _rope_tables(pos_p)
    cos_s, sin_s = _rope_tables(np.array([PAST_LEN]))
    log_gamma = np.log1p(-np.exp2(-5.0 - np.arange(RET_HEADS, dtype=np.float64)))
    lg_lane = np.repeat(log_gamma, RET_DK)[None, :]
    tt = order.astype(np.float64)
    diff = tt[:, None] - tt[None, :]
    f32 = lambda a: jnp.asarray(np.asarray(a, np.float32))
    return dict(
        cos_qk=f32(np.concatenate([cos_p, cos_p], axis=1)),
        sin_qk=f32(np.concatenate([sin_p, sin_p], axis=1)),
        cos_col=f32(np.concatenate([cos_s, cos_s], axis=1).reshape(2 * RET_QK, 1)),
        sin_col=f32(np.concatenate([sin_s, sin_s], axis=1).reshape(2 * RET_QK, 1)),
        dq=f32(np.exp((tt[:, None] + 1.0) * lg_lane)),
        dk=f32(np.exp((tb - 1.0 - tt[:, None]) * lg_lane)),
        dmat=f32(np.where(diff >= 0, np.exp(np.maximum(diff, 0.0)[None] * log_gamma[:, None, None]), 0.0)),
        gam_tbl=f32(np.broadcast_to(np.exp(tb * np.repeat(log_gamma, RET_DK))[:, None],
                                    (RET_QK, RET_WIDTH))),
        gamma=np.exp(log_gamma),
    )


def _block_diag_mask(rows_per, cols_per, nblk):
    r = np.arange(rows_per * nblk)[:, None] // rows_per
    c = np.arange(cols_per * nblk)[None, :] // cols_per
    return (r == c).astype(np.float32)


def _s5_params(log_dt, a_re, a_im, b_re, b_im, c_re, c_im):
    nl = log_dt.shape[0]
    dt = jnp.exp(log_dt)[..., None]
    mag = jnp.exp(dt * a_re)
    ab_re = mag * jnp.cos(dt * a_im)
    ab_im = mag * jnp.sin(dt * a_im)
    den = a_re * a_re + a_im * a_im
    nr = ab_re - 1.0
    g_re = (nr * a_re + ab_im * a_im) / den
    g_im = (ab_im * a_re - nr * a_im) / den
    bb_re = g_re[..., None] * b_re - g_im[..., None] * b_im
    bb_im = g_re[..., None] * b_im + g_im[..., None] * b_re
    mask_in = jnp.asarray(_block_diag_mask(S5_GROUP, S5_STATE, S5_GROUPS))
    mask_out = jnp.asarray(_block_diag_mask(S5_STATE, S5_GROUP, S5_GROUPS))

    def in_blk(bb):
        rows = jnp.swapaxes(bb, 2, 3).reshape(nl, S5_WIDTH, S5_STATE)
        return jnp.concatenate([rows] * S5_GROUPS, axis=2) * mask_in

    def out_blk(c):
        rows = jnp.swapaxes(c, 2, 3).reshape(nl, S5_N, S5_GROUP)
        return jnp.concatenate([rows] * S5_GROUPS, axis=2) * mask_out

    bblk = jnp.concatenate([in_blk(bb_re), in_blk(bb_im)], axis=2)
    cblk = jnp.concatenate([out_blk(c_re), -out_blk(c_im)], axis=1)
    m = jnp.arange(1, TIME_BLOCK + 1, dtype=F32)[None, :, None, None]
    pmag = jnp.exp(m * (dt * a_re)[:, None])
    pang = m * (dt * a_im)[:, None]
    apow = jnp.concatenate([(pmag * jnp.cos(pang)).reshape(nl, TIME_BLOCK, S5_N),
                            (pmag * jnp.sin(pang)).reshape(nl, TIME_BLOCK, S5_N)], axis=2)
    ab = jnp.concatenate([ab_re.reshape(nl, S5_N, 1), ab_im.reshape(nl, S5_N, 1)], axis=1)
    return bblk, cblk, apow, ab


def kernel(x_prompt, x_sample, state_hgrn, state_ret, state_s5_re, state_s5_im, w_in, hgrn_lb_logits, hgrn_norm_w, ret_norm_w, s5_log_dt, s5_a_re, s5_a_im, s5_b_re, s5_b_im, s5_c_re, s5_c_im, s5_d, s5_glu_w, s5_glu_b, w_out, ln1_w, ln1_b, w_up, w_down, ln2_w, ln2_b):
    bp, tp, _ = x_prompt.shape
    bs = x_sample.shape[0]
    tb = TIME_BLOCK

    lb_prob = jax.nn.softmax(hgrn_lb_logits.astype(F32), axis=0)
    lower_bounds = jnp.cumsum(lb_prob, axis=0) - lb_prob[0:1]

    tbl = _position_tables(tp)
    ret_bd = jnp.asarray(_block_diag_mask(RET_DK, RET_DV, RET_HEADS))
    hg_bd = jnp.asarray(_block_diag_mask(HG_DV, HG_DK, 2))
    ones256 = jnp.asarray(_block_diag_mask(64, 64, 4)).astype(BF16)
    ones128 = jnp.asarray(_block_diag_mask(64, 64, 2)).astype(BF16)
    gam_lanes = jnp.asarray(np.broadcast_to(tbl["gamma"][:, None, None],
                                            (RET_HEADS, 1, bs)).astype(np.float32))

    wo = w_out.astype(BF16)
    ffn_par = (wo, ln1_w[:, None, :], ln1_b[:, None, :], w_up.astype(BF16), w_down.astype(BF16),
               ln2_w[:, None, :], ln2_b[:, None, :])
    lb_row = lower_bounds[:, None, :]
    hnw_row = jnp.tile(hgrn_norm_w, (1, HG_HEADS))[:, None, :]
    rnw_row = ret_norm_w[:, None, :]
    bblk, cblk, apow, ab_col = _s5_params(s5_log_dt, s5_a_re, s5_a_im, s5_b_re, s5_b_im,
                                          s5_c_re, s5_c_im)
    s5_par = (bblk.astype(BF16), cblk.astype(BF16), apow, apow[:, BLOCK_LEN - 1::BLOCK_LEN],
              s5_d[:, None, :], s5_glu_w.astype(BF16), s5_glu_b[:, None, :])
    s5_step_par = (jnp.swapaxes(s5_par[0], 1, 2), jnp.swapaxes(s5_par[1], 1, 2), ab_col,
                   s5_d[:, :, None],
                   jnp.swapaxes(s5_glu_w, 1, 2).astype(BF16), s5_glu_b[:, :, None])

    hg_s = jnp.transpose(state_hgrn, (0, 2, 3, 4, 1))
    rt_s = jnp.transpose(state_ret, (0, 2, 3, 4, 1))
    re_s = jnp.transpose(state_s5_re, (0, 2, 3, 1)).reshape(DEPTH, S5_N, bs)
    im_s = jnp.transpose(state_s5_im, (0, 2, 3, 1)).reshape(DEPTH, S5_N, bs)
    new_hg_s = jnp.zeros(hg_s.shape, F32)
    new_rt_s = jnp.zeros(rt_s.shape, F32)
    new_re_s = jnp.zeros(re_s.shape, F32)
    new_im_s = jnp.zeros(im_s.shape, F32)

    xp = _permute_tokens(x_prompt).reshape(bp * tp, D_MODEL)
    xs = x_sample.reshape(bs, D_MODEL)
    outs = {k: [] for k in ("hg_p", "rt_p", "s5_p")}
    for l in range(DEPTH):
        hg_in, rt_in, s5_in, hg_in_s, rt_in_s, s5_in_s = _proj(xp, xs, w_in, l, 512)
        o_hg, st_hg = _hgrn(hg_in, lb_row, hnw_row, ones256, ones128, hg_bd, l, bp, tp)
        o_rt, st_rt = _ret(rt_in, tbl["cos_qk"], tbl["sin_qk"], tbl["dq"], tbl["dk"], tbl["dmat"],
                           tbl["gam_tbl"], ret_bd, rnw_row, ones256, ones128, l, bp, tp)
        y5, st_s5 = _s5(s5_in, *s5_par, l, bp, tp)
        outs["hg_p"].append(st_hg)
        outs["rt_p"].append(st_rt)
        outs["s5_p"].append(st_s5)
        o_hg_t, new_hg_s, o_rt_t, new_rt_s, y5_t, new_re_s, new_im_s = _sample_mixers(
            l, (hg_in_s, rt_in_s, s5_in_s),
            (lower_bounds[:, :, None], hgrn_norm_w[:, :, None]),
            (tbl["cos_col"], tbl["sin_col"], gam_lanes, ret_norm_w[:, :, None]),
            s5_step_par, (hg_s, rt_s, re_s, im_s), (new_hg_s, new_rt_s, new_re_s, new_im_s))
        xp, xs = _ffn((xp, o_hg, o_rt, y5), (xs, o_hg_t, o_rt_t, y5_t), *ffn_par, l, 512)

    yp = _permute_tokens(xp.reshape(bp, tp, D_MODEL), inverse=True)

    def s5_out(a):
        return jnp.transpose(a.reshape(DEPTH, S5_GROUPS, S5_STATE, bs), (0, 3, 1, 2))

    s5_p = jnp.stack(outs["s5_p"]).reshape(DEPTH, bp, 2, S5_GROUPS, S5_STATE)

    return (yp, xs.reshape(bs, 1, D_MODEL), jnp.stack(outs["hg_p"]), jnp.stack(outs["rt_p"]),
            s5_p[:, :, 0], s5_p[:, :, 1],
            jnp.transpose(new_hg_s, (0, 4, 1, 2, 3)), jnp.transpose(new_rt_s, (0, 4, 1, 2, 3)),
            s5_out(new_re_s), s5_out(new_im_s))
```
